```python
import jax, jax.numpy as jnp
from jax import lax
import numpy as np

D_MODEL = 2048
BATCH = 4
SEQ = 8192
DEPTH = 1
DEC_BATCH = 8
DEC_SEQ = 32
PAST_LEN = 1024

CHUNK = 64
D_RNN = D_MODEL
RNN_BLOCKS = 16
RNN_BLOCK_DIM = D_RNN // RNN_BLOCKS
CONV_W = 4
RG_C = 8.0
N_HEADS = 16
HEAD_DIM = D_MODEL // N_HEADS
D_ATT = N_HEADS * HEAD_DIM
LEFT_CHUNKS = 8
BAND = (LEFT_CHUNKS + 1) * CHUNK
ATT_REACH = LEFT_CHUNKS * CHUNK
MAX_REL = 128
REL_BUCKETS = MAX_REL + CHUNK
N_KEYS = 128
N_EXPERTS = N_KEYS * N_KEYS
PEER_HEADS = 8
D_KEY = 256
PEER_TOPK = 16
PEER_BLOCK = 128
D_IN = 2 * D_RNN + 3 * D_ATT + 2 * D_MODEL
SPLIT_POINTS = (D_RNN, 2 * D_RNN, 2 * D_RNN + D_ATT, 2 * D_RNN + 2 * D_ATT, 2 * D_RNN + 3 * D_ATT, 2 * D_RNN + 3 * D_ATT + D_MODEL)
ALPHA = (2.0 * DEPTH) ** 0.25
BETA = (8.0 * DEPTH) ** -0.25
LN_EPS = 1e-5
NEG_INF = -1e30

kernel_name = 'hybrid_streaming_encoder_step'


def layer_norm(x):
    x32 = x.astype(jnp.float32)
    mu = jnp.mean(x32, axis=-1, keepdims=True)
    var = jnp.mean(jnp.square(x32 - mu), axis=-1, keepdims=True)
    return ((x32 - mu) * lax.rsqrt(var + LN_EPS)).astype(x.dtype)


def affine_layer_norm(x, g, b):
    return layer_norm(x) * g + b


def adaln_modulation(c, w_ada, b_ada):
    mod = (jax.nn.silu(c) @ w_ada + b_ada)[:, None, :]
    return jnp.split(mod, 6, axis=-1)


def causal_depthwise_conv(x, buf, w, b):
    t = x.shape[1]
    x_ext = jnp.concatenate([buf.astype(x.dtype), x], axis=1)
    y = b + w[0] * x_ext[:, 0:t]
    for j in range(1, CONV_W):
        y = y + w[j] * x_ext[:, j:j + t]
    return y, x_ext[:, -(CONV_W - 1):]


def block_diag_linear(x, w, b):
    bsz, t, _ = x.shape
    xb = x.reshape(bsz, t, RNN_BLOCKS, RNN_BLOCK_DIM)
    return jnp.einsum('btni,nio->btno', xb, w).reshape(bsz, t, D_RNN) + b


def rg_lru(xc, h0, w_ra, b_ra, w_ri, b_ri, lam):
    x32 = xc.astype(jnp.float32)
    r = jax.nn.sigmoid(block_diag_linear(xc, w_ra, b_ra).astype(jnp.float32))
    i = jax.nn.sigmoid(block_diag_linear(xc, w_ri, b_ri).astype(jnp.float32))
    log_a = -RG_C * r * jax.nn.softplus(-lam.astype(jnp.float32))
    a = jnp.exp(log_a)
    u = jnp.sqrt(-jnp.expm1(2.0 * log_a)) * (i * x32)

    def step(h, au):
        h = au[0] * h + au[1]
        return h, h

    h_last, hs = lax.scan(step, h0.astype(jnp.float32), (jnp.swapaxes(a, 0, 1), jnp.swapaxes(u, 0, 1)))
    return jnp.swapaxes(hs, 0, 1).astype(xc.dtype), h_last.astype(xc.dtype)


def band_attend(q, k, v, q_pos, k_pos, rel_bias):
    s = jnp.einsum('bqhd,bkhd->bhqk', q, k).astype(jnp.float32) * (HEAD_DIM ** -0.5)
    dist = q_pos[:, None] - k_pos[None, :]
    s = s + rel_bias[:, jnp.clip(dist, -(CHUNK - 1), MAX_REL) + (CHUNK - 1)].astype(jnp.float32)
    qc = q_pos[:, None] // CHUNK
    kc = k_pos[None, :] // CHUNK
    visible = (k_pos[None, :] >= 0) & (kc <= qc) & (kc >= qc - LEFT_CHUNKS)
    p = jax.nn.softmax(jnp.where(visible, s, NEG_INF), axis=-1).astype(v.dtype)
    return jnp.einsum('bhqk,bkhd->bqhd', p, v)


def prompt_band_attention(q, k, v, rel_bias):
    bsz, t, _, _ = q.shape
    n_chunks = t // CHUNK
    pad = ((0, 0), (ATT_REACH, 0), (0, 0), (0, 0))
    k_pad, v_pad = jnp.pad(k, pad), jnp.pad(v, pad)
    q_chunks = jnp.moveaxis(q.reshape(bsz, n_chunks, CHUNK, N_HEADS, HEAD_DIM), 1, 0)

    def one_chunk(args):
        idx, q_blk = args
        start = idx * CHUNK
        k_blk = lax.dynamic_slice_in_dim(k_pad, start, BAND, axis=1)
        v_blk = lax.dynamic_slice_in_dim(v_pad, start, BAND, axis=1)
        q_pos = start + jnp.arange(CHUNK)
        k_pos = start - ATT_REACH + jnp.arange(BAND)
        return band_attend(q_blk, k_blk, v_blk, q_pos, k_pos, rel_bias)

    out = lax.map(one_chunk, (jnp.arange(n_chunks), q_chunks))
    return jnp.moveaxis(out, 0, 1).reshape(bsz, t, D_ATT)


def make_sample_attention(cache_k, cache_v):
    def attend(q, k, v, rel_bias):
        bsz, t, _, _ = q.shape
        n_past = cache_k.shape[1]
        k_all = jnp.concatenate([cache_k.astype(k.dtype), k], axis=1)
        v_all = jnp.concatenate([cache_v.astype(v.dtype), v], axis=1)
        q_pos = PAST_LEN + jnp.arange(t)
        k_pos = PAST_LEN - n_past + jnp.arange(n_past + t)
        return band_attend(q, k_all, v_all, q_pos, k_pos, rel_bias).reshape(bsz, t, D_ATT)
    return attend


def peer_ffn(h, w_pq, peer_keys, peer_u, peer_v):
    bsz, t, d = h.shape
    n_tok = bsz * t
    n_blocks = -(-n_tok // PEER_BLOCK)
    flat = jnp.pad(h.reshape(n_tok, d), ((0, n_blocks * PEER_BLOCK - n_tok), (0, 0)))

    def one_block(xb):
        q = (xb @ w_pq).reshape(PEER_BLOCK, PEER_HEADS, 2, D_KEY // 2)
        s = jnp.einsum('nhpd,hpkd->nhpk', q, peer_keys).astype(jnp.float32)
        s1, i1 = lax.top_k(s[:, :, 0], PEER_TOPK)
        s2, i2 = lax.top_k(s[:, :, 1], PEER_TOPK)
        cand_s = (s1[..., :, None] + s2[..., None, :]).reshape(PEER_BLOCK, PEER_HEADS, PEER_TOPK * PEER_TOPK)
        cand_i = (i1[..., :, None] * N_KEYS + i2[..., None, :]).reshape(PEER_BLOCK, PEER_HEADS, PEER_TOPK * PEER_TOPK)
        top_s, pos = lax.top_k(cand_s, PEER_TOPK)
        expert = jnp.take_along_axis(cand_i, pos, axis=-1)
        g = jax.nn.softmax(top_s, axis=-1).astype(xb.dtype)
        act = jax.nn.gelu(jnp.einsum('nhkd,nd->nhk', jnp.take(peer_u, expert, axis=0), xb), approximate=False)
        return jnp.einsum('nhk,nhkd->nd', g * act, jnp.take(peer_v, expert, axis=0))

    out = lax.map(one_block, flat.reshape(n_blocks, PEER_BLOCK, d))
    return out.reshape(n_blocks * PEER_BLOCK, d)[:n_tok].reshape(bsz, t, d)


def encoder_layer(x, c, conv_buf, rnn_h0, attend, w_ada, b_ada, w_in, conv_w, conv_b, w_ra, b_ra, w_ri, b_ri, rg_lambda, rel_bias, w_out, ln1_g, ln1_b, w_pq, peer_keys, peer_u, peer_v, ln2_g, ln2_b):
    bsz, t, _ = x.shape
    shift1, scale1, gate1, shift2, scale2, gate2 = adaln_modulation(c, w_ada, b_ada)
    h = layer_norm(x) * (1.0 + scale1) + shift1
    xr, yr, q, k, v, ga, gb = jnp.split(h @ w_in, SPLIT_POINTS, axis=-1)
    xc, conv_state = causal_depthwise_conv(xr, conv_buf, conv_w, conv_b)
    hr, rnn_state = rg_lru(xc, rnn_h0, w_ra, b_ra, w_ri, b_ri, rg_lambda)
    out_a = hr * jax.nn.gelu(yr, approximate=False)
    heads = (bsz, t, N_HEADS, HEAD_DIM)
    k, v = k.reshape(heads), v.reshape(heads)
    out_b = attend(q.reshape(heads), k, v, rel_bias)
    mixed = (jax.nn.sigmoid(ga) * out_a + jax.nn.sigmoid(gb) * out_b) @ w_out
    x = affine_layer_norm(ALPHA * x + gate1 * mixed, ln1_g, ln1_b)
    h2 = layer_norm(x) * (1.0 + scale2) + shift2
    x = affine_layer_norm(ALPHA * x + gate2 * peer_ffn(h2, w_pq, peer_keys, peer_u, peer_v), ln2_g, ln2_b)
    return x, conv_state, rnn_state, k, v


def setup_inputs(seed: int = 0) -> dict:
    key = jax.random.key(seed)
    ks = jax.random.split(key, 28)
    f32 = jnp.float32

    def nrm(k, shape, scale):
        return jax.random.normal(k, shape, f32) * scale

    att_cache = min(ATT_REACH, PAST_LEN)
    a = jax.random.uniform(ks[17], (DEPTH, D_RNN), f32, 0.9, 0.999) ** (1.0 / RG_C)
    return {
        'x_prompt': nrm(ks[0], (BATCH, SEQ, D_MODEL), 1.0),
        'x_sample': nrm(ks[1], (DEC_BATCH, DEC_SEQ, D_MODEL), 1.0),
        'c_prompt': nrm(ks[2], (BATCH, D_MODEL), 1.0),
        'c_sample': nrm(ks[3], (DEC_BATCH, D_MODEL), 1.0),
        'state_conv': nrm(ks[4], (DEPTH, DEC_BATCH, CONV_W - 1, D_RNN), 1.0),
        'state_rnn': nrm(ks[5], (DEPTH, DEC_BATCH, D_RNN), 1.0),
        'cache_k': nrm(ks[6], (DEPTH, DEC_BATCH, att_cache, N_HEADS, HEAD_DIM), 1.0),
        'cache_v': nrm(ks[7], (DEPTH, DEC_BATCH, att_cache, N_HEADS, HEAD_DIM), 1.0),
        'w_ada': nrm(ks[8], (DEPTH, D_MODEL, 6 * D_MODEL), 0.5 * D_MODEL ** -0.5),
        'b_ada': nrm(ks[9], (DEPTH, 6 * D_MODEL), 0.01),
        'w_in': nrm(ks[10], (DEPTH, D_MODEL, D_IN), D_MODEL ** -0.5),
        'conv_w': nrm(ks[11], (DEPTH, CONV_W, D_RNN), CONV_W ** -0.5),
        'conv_b': nrm(ks[12], (DEPTH, D_RNN), 0.01),
        'w_ra': nrm(ks[13], (DEPTH, RNN_BLOCKS, RNN_BLOCK_DIM, RNN_BLOCK_DIM), RNN_BLOCK_DIM ** -0.5),
        'b_ra': nrm(ks[14], (DEPTH, D_RNN), 0.01),
        'w_ri': nrm(ks[15], (DEPTH, RNN_BLOCKS, RNN_BLOCK_DIM, RNN_BLOCK_DIM), RNN_BLOCK_DIM ** -0.5),
        'b_ri': nrm(ks[16], (DEPTH, D_RNN), 0.01),
        'rg_lambda': jnp.log(a) - jnp.log1p(-a),
        'rel_bias': nrm(ks[18], (DEPTH, N_HEADS, REL_BUCKETS), 0.1),
        'w_out': nrm(ks[19], (DEPTH, D_MODEL, D_MODEL), BETA * D_MODEL ** -0.5),
        'ln1_g': 1.0 + nrm(ks[20], (DEPTH, D_MODEL), 0.01),
        'ln1_b': nrm(ks[21], (DEPTH, D_MODEL), 0.01),
        'w_pq': nrm(ks[22], (DEPTH, D_MODEL, PEER_HEADS * D_KEY), D_MODEL ** -0.5),
        'peer_keys': nrm(ks[23], (DEPTH, PEER_HEADS, 2, N_KEYS, D_KEY // 2), (D_KEY // 2) ** -0.5),
        'peer_u': nrm(ks[24], (DEPTH, N_EXPERTS, D_MODEL), D_MODEL ** -0.5),
        'peer_v': nrm(ks[25], (DEPTH, N_EXPERTS, D_MODEL), BETA * PEER_HEADS ** -0.5),
        'ln2_g': 1.0 + nrm(ks[26], (DEPTH, D_MODEL), 0.01),
        'ln2_b': nrm(ks[27], (DEPTH, D_MODEL), 0.01),
    }


def reference(x_prompt, x_sample, c_prompt, c_sample, state_conv, state_rnn, cache_k, cache_v, w_ada, b_ada, w_in, conv_w, conv_b, w_ra, b_ra, w_ri, b_ri, rg_lambda, rel_bias, w_out, ln1_g, ln1_b, w_pq, peer_keys, peer_u, peer_v, ln2_g, ln2_b):
    y_prompt, y_sample = x_prompt, x_sample
    p_conv, p_rnn, p_k, p_v = [], [], [], []
    s_conv, s_rnn, s_k, s_v = [], [], [], []
    bsz_p, t_p = x_prompt.shape[0], x_prompt.shape[1]
    keep = min(ATT_REACH, t_p)
    for l in range(DEPTH):
        w = (w_ada[l], b_ada[l], w_in[l], conv_w[l], conv_b[l], w_ra[l], b_ra[l], w_ri[l], b_ri[l], rg_lambda[l], rel_bias[l], w_out[l], ln1_g[l], ln1_b[l], w_pq[l], peer_keys[l], peer_u[l], peer_v[l], ln2_g[l], ln2_b[l])
        zero_conv = jnp.zeros((bsz_p, CONV_W - 1, D_RNN), x_prompt.dtype)
        zero_h = jnp.zeros((bsz_p, D_RNN), jnp.float32)
        y_prompt, cs, hs, ks_, vs_ = encoder_layer(y_prompt, c_prompt, zero_conv, zero_h, prompt_band_attention, *w)
        p_conv.append(cs)
        p_rnn.append(hs)
        p_k.append(ks_[:, -keep:])
        p_v.append(vs_[:, -keep:])
        y_sample, cs, hs, ks_, vs_ = encoder_layer(y_sample, c_sample, state_conv[l], state_rnn[l], make_sample_attention(cache_k[l], cache_v[l]), *w)
        s_conv.append(cs)
        s_rnn.append(hs)
        s_k.append(ks_)
        s_v.append(vs_)
    return (y_prompt, y_sample, jnp.stack(p_conv), jnp.stack(p_rnn), jnp.stack(p_k), jnp.stack(p_v), jnp.stack(s_conv), jnp.stack(s_rnn), jnp.stack(s_k), jnp.stack(s_v))
```

```python
import functools
import math

import jax
import jax.numpy as jnp
from jax import lax
from jax.experimental import pallas as pl
from jax.experimental.pallas import tpu as pltpu

F32 = jnp.float32
BF16 = jnp.bfloat16

CHUNK = 64
LEFT_CHUNKS = 8
ATT_REACH = LEFT_CHUNKS * CHUNK
MAX_REL = 128
REL_BUCKETS = MAX_REL + CHUNK
PAST_LEN = 1024
CONV_W = 4
RG_C = 8.0
RNN_BLOCK_DIM = 128
HEAD_DIM = 128
N_KEYS = 128
PEER_HEADS = 8
PEER_TOPK = 16
DEPTH = 1
ALPHA = (2.0 * DEPTH) ** 0.25
LN_EPS = 1e-5
NEG_INF = -1e30

LANES = 128
SUBLANES = 8
VMEM_LIMIT = 56 * 1024 * 1024

_CAND_COUNTS = tuple(PEER_TOPK // (r1 + 1) for r1 in range(PEER_TOPK))
_CAND_OFFSETS = tuple(sum(_CAND_COUNTS[:r1]) for r1 in range(PEER_TOPK))
_N_CAND = sum(_CAND_COUNTS)
_N_CAND_PAD = -(-_N_CAND // SUBLANES) * SUBLANES


def _params(n_grid):
    return pltpu.CompilerParams(dimension_semantics=("arbitrary",) * n_grid,
                                vmem_limit_bytes=VMEM_LIMIT)


def _normalize(x):
    mu = jnp.mean(x, axis=-1, keepdims=True)
    xc = x - mu
    var = jnp.mean(xc * xc, axis=-1, keepdims=True)
    return xc * lax.rsqrt(var + LN_EPS)


def _gelu(x):
    return 0.5 * x * (1.0 + lax.erf(x * (2.0 ** -0.5)))


def _ada_kernel(c_ref, w_ref, b_ref, o_ref):
    c = c_ref[...]
    s = c * jax.nn.sigmoid(c)
    o_ref[...] = jnp.dot(s, w_ref[...], preferred_element_type=F32) + b_ref[...]


def _ada(c, w_ada, b_ada):
    rows, d = c.shape
    n = w_ada.shape[1]
    tn = 1024
    return pl.pallas_call(
        _ada_kernel,
        grid=(n // tn,),
        in_specs=[pl.BlockSpec((rows, d), lambda j: (0, 0)),
                  pl.BlockSpec((d, tn), lambda j: (0, j)),
                  pl.BlockSpec((1, tn), lambda j: (0, j))],
        out_specs=pl.BlockSpec((rows, tn), lambda j: (0, j)),
        out_shape=jax.ShapeDtypeStruct((rows, n), F32),
        compiler_params=_params(1),
        name="ada",
    )(c, w_ada, b_ada.reshape(1, n))


def _proj_kernel(x_ref, sh_ref, sc_ref, w_ref, o_ref, h_ref):
    @pl.when(pl.program_id(2) == 0)
    def _():
        h = _normalize(x_ref[...]) * (1.0 + sc_ref[...]) + sh_ref[...]
        h_ref[...] = h.astype(BF16)

    o_ref[...] = jnp.dot(h_ref[...], w_ref[...], preferred_element_type=F32)


def _mod_spec(mod, tm, d):
    if mod.shape[1] == 1:
        return pl.BlockSpec((None, 1, d), lambda b, i, *_: (b, 0, 0))
    return pl.BlockSpec((None, tm, d), lambda b, i, *_: (b, i, 0))


def _proj(x, shift, scale, w_in_bf16, tm):
    bsz, t, d = x.shape
    n = w_in_bf16.shape[1]
    tn = 1024
    return pl.pallas_call(
        _proj_kernel,
        grid=(bsz, t // tm, n // tn),
        in_specs=[pl.BlockSpec((None, tm, d), lambda b, i, j: (b, i, 0)),
                  _mod_spec(shift, tm, d), _mod_spec(scale, tm, d),
                  pl.BlockSpec((d, tn), lambda b, i, j: (0, j))],
        out_specs=pl.BlockSpec((None, tm, tn), lambda b, i, j: (b, i, j)),
        out_shape=jax.ShapeDtypeStruct((bsz, t, n), F32),
        scratch_shapes=[pltpu.VMEM((tm, d), BF16)],
        compiler_params=_params(3),
        name="proj",
    )(x, shift, scale, w_in_bf16)


def _rnn_kernel(xr_ref, yr_ref, ga_ref, cbuf_ref, h0_ref, cw_ref, cb_ref, wra_ref, bra_ref,
                wri_ref, bri_ref, lam_ref, pa_ref, cst_ref, hst_ref,
                xbuf, a_s, u_s, hcar, *, tt, tc):
    t = pl.program_id(2)
    nt = pl.num_programs(2)
    pad = SUBLANES
    keep = CONV_W - 1

    @pl.when(t == 0)
    def _():
        xbuf[pad - keep:pad, :] = cbuf_ref[...]
        hcar[...] = h0_ref[...]

    xbuf[pad:pad + tt, :] = xr_ref[...]
    cw = cw_ref[...]
    xc = cb_ref[...] + cw[0:1, :] * xbuf[pad - keep:pad - keep + tt, :]
    for j in range(1, CONV_W):
        xc = xc + cw[j:j + 1, :] * xbuf[pad - keep + j:pad - keep + j + tt, :]
    tail = xbuf[pad + tt - keep:pad + tt, :]
    xbuf[pad - keep:pad, :] = tail

    lam = lam_ref[...]
    softplus_neg = jnp.maximum(-lam, 0.0) + jnp.log1p(jnp.exp(-jnp.abs(lam)))
    for blk in range(tc // RNN_BLOCK_DIM):
        cols = slice(blk * RNN_BLOCK_DIM, (blk + 1) * RNN_BLOCK_DIM)
        xb = xc[:, cols]
        xb16 = xb.astype(BF16)
        r = jax.nn.sigmoid(jnp.dot(xb16, wra_ref[blk], preferred_element_type=F32) + bra_ref[:, cols])
        i = jax.nn.sigmoid(jnp.dot(xb16, wri_ref[blk], preferred_element_type=F32) + bri_ref[:, cols])
        log_a = -RG_C * r * softplus_neg[:, cols]
        a = jnp.exp(log_a)
        u = jnp.sqrt(jnp.tanh(-log_a) * (1.0 + a * a)) * (i * xb)
        a_s[:, cols] = a
        u_s[:, cols] = u

    row = lax.broadcasted_iota(jnp.int32, (SUBLANES, tc), 0)

    def scan_group(g, h_prev):
        rows = pl.ds(pl.multiple_of(g * SUBLANES, SUBLANES), SUBLANES)
        a = a_s[rows, :]
        u = u_s[rows, :]
        for s in (1, 2, 4):
            a_sh = pltpu.roll(a, s, 0)
            u_sh = pltpu.roll(u, s, 0)
            live = row >= s
            u = jnp.where(live, a * u_sh + u, u)
            a = jnp.where(live, a * a_sh, a)
        h = a * h_prev + u
        u_s[rows, :] = h
        return h[SUBLANES - 1:SUBLANES, :]

    h_last = lax.fori_loop(0, tt // SUBLANES, scan_group, hcar[...])
    hcar[...] = h_last

    pa_ref[...] = u_s[...] * _gelu(yr_ref[...]) * jax.nn.sigmoid(ga_ref[...])

    @pl.when(t == nt - 1)
    def _():
        cst_ref[...] = tail
        hst_ref[...] = h_last


def _rnn(proj, conv_buf, h0, conv_w, conv_b, w_ra16, b_ra, w_ri16, b_ri, lam, d, tt, tc):
    bsz, t, _ = proj.shape
    nc = d // tc
    nb = tc // RNN_BLOCK_DIM
    keep = CONV_W - 1
    col = lambda off: (lambda b, c, i: (b, i, off * nc + c))
    vec = pl.BlockSpec((1, tc), lambda b, c, i: (0, c))
    blk = pl.BlockSpec((nb, RNN_BLOCK_DIM, RNN_BLOCK_DIM), lambda b, c, i: (c, 0, 0))
    pa, cst, hst = pl.pallas_call(
        functools.partial(_rnn_kernel, tt=tt, tc=tc),
        grid=(bsz, nc, t // tt),
        in_specs=[pl.BlockSpec((None, tt, tc), col(0)),
                  pl.BlockSpec((None, tt, tc), col(1)),
                  pl.BlockSpec((None, tt, tc), col(5)),
                  pl.BlockSpec((None, keep, tc), lambda b, c, i: (b, 0, c)),
                  pl.BlockSpec((None, 1, tc), lambda b, c, i: (b, 0, c)),
                  pl.BlockSpec((CONV_W, tc), lambda b, c, i: (0, c)),
                  vec, blk, vec, blk, vec, vec],
        out_specs=[pl.BlockSpec((None, tt, tc), lambda b, c, i: (b, i, c)),
                   pl.BlockSpec((None, keep, tc), lambda b, c, i: (b, 0, c)),
                   pl.BlockSpec((None, 1, tc), lambda b, c, i: (b, 0, c))],
        out_shape=[jax.ShapeDtypeStruct((bsz, t, d), F32),
                   jax.ShapeDtypeStruct((bsz, keep, d), F32),
                   jax.ShapeDtypeStruct((bsz, 1, d), F32)],
        scratch_shapes=[pltpu.VMEM((tt + SUBLANES, tc), F32),
                        pltpu.VMEM((tt, tc), F32),
                        pltpu.VMEM((tt, tc), F32),
                        pltpu.VMEM((1, tc), F32)],
        compiler_params=_params(3),
        name="rnn",
    )(proj, proj, proj, conv_buf, h0.reshape(bsz, 1, d), conv_w, conv_b.reshape(1, d),
      w_ra16, b_ra.reshape(1, d), w_ri16, b_ri.reshape(1, d), lam.reshape(1, d))
    return pa, cst, hst.reshape(bsz, d)


def _rel_bias_tile(rel_ref, head, dist, visible):
    idx = jnp.clip(dist, -(CHUNK - 1), MAX_REL) + (CHUNK - 1)

    def body(j, acc):
        return acc + jnp.where(idx == j, rel_ref[head, j], 0.0)

    bias = lax.fori_loop(0, REL_BUCKETS, body, jnp.zeros(dist.shape, F32))
    return jnp.where(visible, bias, NEG_INF)


def _softmax_pv(s, v16):
    m = jnp.max(s, axis=-1, keepdims=True)
    p = jnp.exp(s - m)
    l = jnp.sum(p, axis=-1, keepdims=True)
    return jnp.dot(p.astype(BF16), v16, preferred_element_type=F32) / l


_QSUB = 2 * CHUNK
_BAND = _QSUB + ATT_REACH


def _attn_prompt_kernel(rel_ref, q_ref, kp_ref, kc_ref, vp_ref, vc_ref, gb_ref, o_ref,
                        bias_s, kcat, vcat):
    head, b, i = pl.program_id(0), pl.program_id(1), pl.program_id(2)
    tq = ATT_REACH

    @pl.when((b == 0) & (i == 0))
    def _():
        r = lax.broadcasted_iota(jnp.int32, (_QSUB, _BAND), 0)
        c = lax.broadcasted_iota(jnp.int32, (_QSUB, _BAND), 1)
        qc = r // CHUNK + LEFT_CHUNKS
        kc = c // CHUNK
        visible = (kc <= qc) & (kc >= qc - LEFT_CHUNKS)
        bias_s[...] = _rel_bias_tile(rel_ref, head, r + ATT_REACH - c, visible)

    kcat[0:tq, :] = kp_ref[...].astype(BF16)
    kcat[tq:2 * tq, :] = kc_ref[...].astype(BF16)
    vcat[0:tq, :] = vp_ref[...].astype(BF16)
    vcat[tq:2 * tq, :] = vc_ref[...].astype(BF16)
    col = lax.broadcasted_iota(jnp.int32, (_QSUB, _BAND), 1)
    for s in range(tq // _QSUB):
        rows = slice(s * _QSUB, (s + 1) * _QSUB)
        band = slice(s * _QSUB, s * _QSUB + _BAND)
        q16 = q_ref[rows, :].astype(BF16)
        sc = lax.dot_general(q16, kcat[band, :], (((1,), (1,)), ((), ())),
                             preferred_element_type=F32) * (HEAD_DIM ** -0.5)
        sc = sc + bias_s[...]
        k_pos = col + (i * tq + s * _QSUB - ATT_REACH)
        sc = jnp.where(k_pos >= 0, sc, NEG_INF)
        o = _softmax_pv(sc, vcat[band, :])
        o_ref[rows, :] = o * jax.nn.sigmoid(gb_ref[rows, :])


def _attn_prompt(proj, rel_bias, d):
    bsz, t, _ = proj.shape
    nh = d // HEAD_DIM
    tq = ATT_REACH
    cur = lambda off: (lambda h, b, i: (b, i, off * nh + h))
    prev = lambda off: (lambda h, b, i: (b, jnp.maximum(i - 1, 0), off * nh + h))
    blk = lambda f: pl.BlockSpec((None, tq, HEAD_DIM), f)
    return pl.pallas_call(
        _attn_prompt_kernel,
        grid=(nh, bsz, t // tq),
        in_specs=[pl.BlockSpec(memory_space=pltpu.SMEM),
                  blk(cur(2)), blk(prev(3)), blk(cur(3)), blk(prev(4)), blk(cur(4)), blk(cur(6))],
        out_specs=pl.BlockSpec((None, tq, HEAD_DIM), lambda h, b, i: (b, i, h)),
        out_shape=jax.ShapeDtypeStruct((bsz, t, d), F32),
        scratch_shapes=[pltpu.VMEM((_QSUB, _BAND), F32),
                        pltpu.VMEM((2 * tq, HEAD_DIM), BF16),
                        pltpu.VMEM((2 * tq, HEAD_DIM), BF16)],
        compiler_params=_params(3),
        name="attn_prompt",
    )(rel_bias, proj, proj, proj, proj, proj, proj)


def _attn_sample_kernel(rel_ref, q_ref, kn_ref, vn_ref, gb_ref, ck_ref, cv_ref, o_ref, *, t, n_past):
    head = pl.program_id(1)

    def bias(n_keys, first_pos):
        r = lax.broadcasted_iota(jnp.int32, (t, n_keys), 0)
        c = lax.broadcasted_iota(jnp.int32, (t, n_keys), 1)
        q_pos = PAST_LEN + r
        k_pos = first_pos + c
        qc = q_pos // CHUNK
        kc = k_pos // CHUNK
        visible = (k_pos >= 0) & (kc <= qc) & (kc >= qc - LEFT_CHUNKS)
        return _rel_bias_tile(rel_ref, head, q_pos - k_pos, visible)

    scale = HEAD_DIM ** -0.5
    q16 = q_ref[...].astype(BF16)
    nt = (((1,), (1,)), ((), ()))
    s_past = lax.dot_general(q16, ck_ref[...].astype(BF16), nt, preferred_element_type=F32) * scale
    s_new = lax.dot_general(q16, kn_ref[...].astype(BF16), nt, preferred_element_type=F32) * scale
    s_past = s_past + bias(n_past, PAST_LEN - n_past)
    s_new = s_new + bias(t, PAST_LEN)
    m = jnp.maximum(jnp.max(s_past, axis=-1, keepdims=True), jnp.max(s_new, axis=-1, keepdims=True))
    p_past = jnp.exp(s_past - m)
    p_new = jnp.exp(s_new - m)
    l = jnp.sum(p_past, axis=-1, keepdims=True) + jnp.sum(p_new, axis=-1, keepdims=True)
    o = (jnp.dot(p_past.astype(BF16), cv_ref[...].astype(BF16), preferred_element_type=F32)
         + jnp.dot(p_new.astype(BF16), vn_ref[...].astype(BF16), preferred_element_type=F32)) / l
    o_ref[...] = o * jax.nn.sigmoid(gb_ref[...])


def _attn_sample(proj, cache_k, cache_v, rel_bias, d):
    bsz, t, _ = proj.shape
    nh = d // HEAD_DIM
    n_past = cache_k.shape[1]
    new = lambda off: pl.BlockSpec((None, t, HEAD_DIM), lambda b, h: (b, 0, off * nh + h))
    past = pl.BlockSpec((None, n_past, HEAD_DIM), lambda b, h: (b, 0, h))
    return pl.pallas_call(
        functools.partial(_attn_sample_kernel, t=t, n_past=n_past),
        grid=(bsz, nh),
        in_specs=[pl.BlockSpec(memory_space=pltpu.SMEM), new(2), new(3), new(4), new(6), past, past],
        out_specs=pl.BlockSpec((None, t, HEAD_DIM), lambda b, h: (b, 0, h)),
        out_shape=jax.ShapeDtypeStruct((bsz, t, d), F32),
        compiler_params=_params(2),
        name="attn_sample",
    )(rel_bias, proj, proj, proj, proj, cache_k.reshape(bsz, n_past, d), cache_v.reshape(bsz, n_past, d))


def _mix_kernel(pa_ref, pb_ref, x_ref, w_ref, g1_ref, sh2_ref, sc2_ref, lg_ref, lb_ref,
                x1_ref, h2_ref):
    mixed = jnp.dot((pa_ref[...] + pb_ref[...]).astype(BF16), w_ref[...], preferred_element_type=F32)
    x1 = _normalize(ALPHA * x_ref[...] + g1_ref[...] * mixed) * lg_ref[...] + lb_ref[...]
    x1_ref[...] = x1
    h2_ref[...] = (_normalize(x1) * (1.0 + sc2_ref[...]) + sh2_ref[...]).astype(BF16)


def _mix(pa, pb, x, w_out16, gate1, shift2, scale2, ln_g, ln_b, tm):
    bsz, t, d = x.shape
    tok = pl.BlockSpec((None, tm, d), lambda b, i: (b, i, 0))
    vec = pl.BlockSpec((1, d), lambda b, i: (0, 0))
    return pl.pallas_call(
        _mix_kernel,
        grid=(bsz, t // tm),
        in_specs=[tok, tok, tok, pl.BlockSpec((d, d), lambda b, i: (0, 0)),
                  _mod_spec(gate1, tm, d), _mod_spec(shift2, tm, d), _mod_spec(scale2, tm, d), vec, vec],
        out_specs=[tok, tok],
        out_shape=[jax.ShapeDtypeStruct((bsz, t, d), F32), jax.ShapeDtypeStruct((bsz, t, d), BF16)],
        compiler_params=_params(2),
        name="mix",
    )(pa, pb, x, w_out16, gate1, shift2, scale2, ln_g.reshape(1, d), ln_b.reshape(1, d))


def _take_top(s, n_iter, rows_f, on_pick):
    def body(it, s_rem):
        m = jnp.max(s_rem, axis=0, keepdims=True)
        first = jnp.min(jnp.where(s_rem == m, rows_f, float(s.shape[0])), axis=0, keepdims=True)
        pick = rows_f == first
        on_pick(it, m, pick)
        return jnp.where(pick, -jnp.inf, s_rem)

    lax.fori_loop(0, n_iter, body, s)


def _route_kernel(h2_ref, wpq_ref, keys_ref, lcol_ref, e1_ref, r2_ref, e2_ref,
                  q_s, rank_s, vals_s, cand_s, sel_s, *, tm):
    q = jnp.dot(h2_ref[...], wpq_ref[...], preferred_element_type=F32)
    for hp in range(2 * PEER_HEADS):
        q_s[hp] = q[:, hp * N_KEYS:(hp + 1) * N_KEYS].astype(BF16)
    rows_f = lax.broadcasted_iota(jnp.int32, (N_KEYS, tm), 0).astype(F32)
    crow_f = lax.broadcasted_iota(jnp.int32, (_N_CAND_PAD, tm), 0).astype(F32)
    nt = (((1,), (1,)), ((), ()))

    def per_head(h, carry):
        scores = []
        for p in range(2):
            s = lax.dot_general(keys_ref[h * 2 + p], q_s[h * 2 + p], nt,
                                preferred_element_type=F32)
            scores.append(s)
            rank_s[p] = jnp.full((N_KEYS, tm), float(PEER_TOPK), F32)

            def on_pick(it, m, pick, p=p):
                vals_s[p, pl.ds(it, 1), :] = m
                rank_s[p] = jnp.where(pick, it.astype(F32), rank_s[p])

            _take_top(s, PEER_TOPK, rows_f, on_pick)

        cand_s[...] = jnp.full((_N_CAND_PAD, tm), -jnp.inf, F32)
        for r1 in range(PEER_TOPK):
            off, cnt = _CAND_OFFSETS[r1], _CAND_COUNTS[r1]
            cand_s[off:off + cnt, :] = vals_s[0, r1:r1 + 1, :] + vals_s[1, 0:cnt, :]
        sel_s[...] = jnp.zeros((_N_CAND_PAD, tm), F32)

        def on_pick_cand(it, m, pick):
            sel_s[...] = jnp.where(pick, 1.0, sel_s[...])

        _take_top(cand_s[...], PEER_TOPK, crow_f, on_pick_cand)

        sel = sel_s[...]
        top = vals_s[0, 0:1, :] + vals_s[1, 0:1, :]
        z = jnp.sum(jnp.where(sel > 0.0, jnp.exp(cand_s[...] - top), 0.0), axis=0, keepdims=True)
        rank1 = rank_s[0]
        lcol = jnp.zeros((N_KEYS, tm), F32)
        for r1 in range(PEER_TOPK):
            off, cnt = _CAND_OFFSETS[r1], _CAND_COUNTS[r1]
            n_sel = jnp.sum(sel[off:off + cnt, :], axis=0, keepdims=True)
            lcol = lcol + jnp.where(rank1 == float(r1), n_sel, 0.0)
        lcol_ref[h] = lcol
        e1_ref[h] = jnp.exp(scores[0] - vals_s[0, 0:1, :]) / z
        r2_ref[h] = rank_s[1]
        e2_ref[h] = jnp.exp(scores[1] - vals_s[1, 0:1, :])
        return carry

    lax.fori_loop(0, PEER_HEADS, per_head, 0)


def _route(h2, w_pq16, keys16, tm):
    bsz, t, d = h2.shape
    nq = w_pq16.shape[1]
    out = pl.BlockSpec((None, PEER_HEADS, N_KEYS, tm), lambda b, i: (b, 0, 0, i))
    shape = jax.ShapeDtypeStruct((bsz, PEER_HEADS, N_KEYS, t), F32)
    return pl.pallas_call(
        functools.partial(_route_kernel, tm=tm),
        grid=(bsz, t // tm),
        in_specs=[pl.BlockSpec((None, tm, d), lambda b, i: (b, i, 0)),
                  pl.BlockSpec((d, nq), lambda b, i: (0, 0)),
                  pl.BlockSpec((2 * PEER_HEADS, N_KEYS, nq // (2 * PEER_HEADS)), lambda b, i: (0, 0, 0))],
        out_specs=[out, out, out, out],
        out_shape=[shape, shape, shape, shape],
        scratch_shapes=[pltpu.VMEM((2 * PEER_HEADS, tm, nq // (2 * PEER_HEADS)), BF16),
                        pltpu.VMEM((2, N_KEYS, tm), F32),
                        pltpu.VMEM((2, PEER_TOPK, tm), F32),
                        pltpu.VMEM((_N_CAND_PAD, tm), F32),
                        pltpu.VMEM((_N_CAND_PAD, tm), F32)],
        compiler_params=_params(2),
        name="route",
    )(h2, w_pq16, keys16)


def _peer_kernel(h2_ref, u_ref, vt_ref, lcol_ref, e1_ref, r2_ref, e2_ref, x1_ref, g2_ref,
                 lg_ref, lb_ref, y_ref, acc, p_s, *, te):
    j = pl.program_id(2)

    @pl.when(j == 0)
    def _():
        acc[...] = jnp.zeros(acc.shape, F32)

    act = lax.dot_general(u_ref[...], h2_ref[...], (((1,), (1,)), ((), ())),
                          preferred_element_type=F32)
    for al in range(te // N_KEYS):
        rows = slice(al * N_KEYS, (al + 1) * N_KEYS)
        w = jnp.zeros((N_KEYS, act.shape[1]), F32)
        for h in range(PEER_HEADS):
            keep = r2_ref[h] < lcol_ref[h, al:al + 1, :]
            w = w + jnp.where(keep, e2_ref[h] * e1_ref[h, al:al + 1, :], 0.0)
        p_s[rows, :] = (w * _gelu(act[rows, :])).astype(BF16)
    acc[...] += jnp.dot(vt_ref[...], p_s[...], preferred_element_type=F32)

    @pl.when(j == pl.num_programs(2) - 1)
    def _():
        ffn = acc[...].T
        y = _normalize(ALPHA * x1_ref[...] + g2_ref[...] * ffn)
        y_ref[...] = y * lg_ref[...] + lb_ref[...]


def _peer(h2, u16, vt16, lcol, e1, r2, e2, x1, gate2, ln_g, ln_b, tm, te):
    bsz, t, d = h2.shape
    n_exp = u16.shape[0]
    na = te // N_KEYS
    tok = pl.BlockSpec((None, tm, d), lambda b, i, j: (b, i, 0))
    vec = pl.BlockSpec((1, d), lambda b, i, j: (0, 0))
    by_a = pl.BlockSpec((None, PEER_HEADS, na, tm), lambda b, i, j: (b, 0, j, i))
    by_b = pl.BlockSpec((None, PEER_HEADS, N_KEYS, tm), lambda b, i, j: (b, 0, 0, i))
    return pl.pallas_call(
        functools.partial(_peer_kernel, te=te),
        grid=(bsz, t // tm, n_exp // te),
        in_specs=[tok,
                  pl.BlockSpec((te, d), lambda b, i, j: (j, 0)),
                  pl.BlockSpec((d, te), lambda b, i, j: (0, j)),
                  by_a, by_a, by_b, by_b, tok, _mod_spec(gate2, tm, d), vec, vec],
        out_specs=tok,
        out_shape=jax.ShapeDtypeStruct((bsz, t, d), F32),
        scratch_shapes=[pltpu.VMEM((d, tm), F32), pltpu.VMEM((te, tm), BF16)],
        compiler_params=_params(3),
        name="peer",
    )(h2, u16, vt16, lcol, e1, r2, e2, x1, gate2, ln_g.reshape(1, d), ln_b.reshape(1, d))


def _layer(x, mods, conv_buf, h0, attend, w, *, tm, tm_mix, tt, tc, tm_route, tm_peer, te, flatten):
    bsz, t, d = x.shape
    if flatten:
        mods = [jnp.broadcast_to(m, (bsz, t, d)).reshape(1, bsz * t, d) for m in mods]
        flat = lambda a: a.reshape(1, bsz * t, a.shape[-1])
    else:
        flat = lambda a: a
    shift1, scale1, gate1, shift2, scale2, gate2 = mods
    proj = _proj(flat(x), shift1, scale1, w["w_in"], tm).reshape(bsz, t, -1)
    pa, conv_state, rnn_state = _rnn(proj, conv_buf, h0, w["conv_w"], w["conv_b"], w["w_ra"], w["b_ra"],
                                     w["w_ri"], w["b_ri"], w["lam"], d, tt, tc)
    pb = attend(proj)
    x1, h2 = _mix(flat(pa), flat(pb), flat(x), w["w_out"], gate1, shift2, scale2, w["ln1_g"], w["ln1_b"], tm_mix)
    lcol, e1, r2, e2 = _route(h2, w["w_pq"], w["keys"], tm_route)
    y = _peer(h2, w["peer_u"], w["peer_vt"], lcol, e1, r2, e2, x1, gate2, w["ln2_g"], w["ln2_b"], tm_peer, te)
    n_heads = d // HEAD_DIM
    k = proj[:, :, 3 * d:4 * d].reshape(bsz, t, n_heads, HEAD_DIM)
    v = proj[:, :, 4 * d:5 * d].reshape(bsz, t, n_heads, HEAD_DIM)
    return y.reshape(bsz, t, d), conv_state, rnn_state, k, v


def kernel(x_prompt, x_sample, c_prompt, c_sample, state_conv, state_rnn, cache_k, cache_v, w_ada, b_ada, w_in, conv_w, conv_b, w_ra, b_ra, w_ri, b_ri, rg_lambda, rel_bias, w_out, ln1_g, ln1_b, w_pq, peer_keys, peer_u, peer_v, ln2_g, ln2_b):
    bsz_p, t_p, d = x_prompt.shape
    bsz_s, t_s, _ = x_sample.shape
    depth = w_ada.shape[0]
    keep = min(ATT_REACH, t_p)
    y_p, y_s = x_prompt, x_sample
    outs = [[] for _ in range(8)]
    rows = bsz_p + bsz_s
    rows_pad = -(-rows // SUBLANES) * SUBLANES
    c_all = jnp.concatenate([c_prompt, c_sample, jnp.zeros((rows_pad - rows, d), F32)], axis=0)
    tm = min(1024, t_p)
    for l in range(depth):
        w = dict(w_in=w_in[l].astype(BF16), conv_w=conv_w[l], conv_b=conv_b[l],
                 w_ra=w_ra[l].astype(BF16), b_ra=b_ra[l], w_ri=w_ri[l].astype(BF16), b_ri=b_ri[l],
                 lam=rg_lambda[l], w_out=w_out[l].astype(BF16), ln1_g=ln1_g[l], ln1_b=ln1_b[l],
                 w_pq=w_pq[l].astype(BF16),
                 keys=peer_keys[l].astype(BF16).reshape(2 * PEER_HEADS, N_KEYS, -1),
                 peer_u=peer_u[l].astype(BF16), peer_vt=peer_v[l].astype(BF16).T,
                 ln2_g=ln2_g[l], ln2_b=ln2_b[l])
        mod = _ada(c_all, w_ada[l], b_ada[l])
        mods_p = [m[:, None, :] for m in jnp.split(mod[:bsz_p], 6, axis=-1)]
        mods_s = [m[:, None, :] for m in jnp.split(mod[bsz_p:rows], 6, axis=-1)]

        y_p, cs, hs, k, v = _layer(
            y_p, mods_p, jnp.zeros((bsz_p, CONV_W - 1, d), F32), jnp.zeros((bsz_p, d), F32),
            lambda proj: _attn_prompt(proj, rel_bias[l], d), w,
            tm=tm, tm_mix=min(256, t_p), tt=min(256, t_p), tc=512, tm_route=256, tm_peer=256, te=1024, flatten=False)
        for dst, val in zip(outs[:4], (cs, hs, k[:, -keep:], v[:, -keep:])):
            dst.append(val)

        y_s, cs, hs, k, v = _layer(
            y_s, mods_s, state_conv[l], state_rnn[l],
            lambda proj: _attn_sample(proj, cache_k[l], cache_v[l], rel_bias[l], d), w,
            tm=bsz_s * t_s, tm_mix=bsz_s * t_s, tt=t_s, tc=512, tm_route=bsz_s * t_s, tm_peer=bsz_s * t_s, te=1024, flatten=True)
        for dst, val in zip(outs[4:], (cs, hs, k, v)):
            dst.append(val)
    return (y_p, y_s) + tuple(jnp.stack(o) for o in outs)
```

```python
import functools
import math

import jax
import jax.numpy as jnp
from jax import lax
from jax.experimental import pallas as pl
from jax.experimental.pallas import tpu as pltpu

F32 = jnp.float32
BF16 = jnp.bfloat16

CHUNK = 64
LEFT_CHUNKS = 8
ATT_REACH = LEFT_CHUNKS * CHUNK
MAX_REL = 128
REL_BUCKETS = MAX_REL + CHUNK
PAST_LEN = 1024
CONV_W = 4
RG_C = 8.0
RNN_BLOCK_DIM = 128
HEAD_DIM = 128
N_KEYS = 128
PEER_HEADS = 8
PEER_TOPK = 16
DEPTH = 1
ALPHA = (2.0 * DEPTH) ** 0.25
LN_EPS = 1e-5
NEG_INF = -1e30

LANES = 128
SUBLANES = 8
VMEM_LIMIT = 56 * 1024 * 1024

_CAND_COUNTS = tuple(PEER_TOPK // (r1 + 1) for r1 in range(PEER_TOPK))
_CAND_OFFSETS = tuple(sum(_CAND_COUNTS[:r1]) for r1 in range(PEER_TOPK))
_N_CAND = sum(_CAND_COUNTS)
_N_CAND_PAD = -(-_N_CAND // SUBLANES) * SUBLANES


def _params(n_grid):
    return pltpu.CompilerParams(dimension_semantics=("arbitrary",) * n_grid,
                                vmem_limit_bytes=VMEM_LIMIT)


def _normalize(x):
    mu = jnp.mean(x, axis=-1, keepdims=True)
    xc = x - mu
    var = jnp.mean(xc * xc, axis=-1, keepdims=True)
    return xc * lax.rsqrt(var + LN_EPS)


def _gelu(x):
    return 0.5 * x * (1.0 + lax.erf(x * (2.0 ** -0.5)))


def _ada_kernel(c_ref, w_ref, b_ref, o_ref):
    c = c_ref[...]
    s = c * jax.nn.sigmoid(c)
    o_ref[...] = jnp.dot(s, w_ref[...], preferred_element_type=F32) + b_ref[...]


def _ada(c, w_ada, b_ada):
    rows, d = c.shape
    n = w_ada.shape[1]
    tn = 1024
    return pl.pallas_call(
        _ada_kernel,
        grid=(n // tn,),
        in_specs=[pl.BlockSpec((rows, d), lambda j: (0, 0)),
                  pl.BlockSpec((d, tn), lambda j: (0, j)),
                  pl.BlockSpec((1, tn), lambda j: (0, j))],
        out_specs=pl.BlockSpec((rows, tn), lambda j: (0, j)),
        out_shape=jax.ShapeDtypeStruct((rows, n), F32),
        compiler_params=_params(1),
        name="ada",
    )(c, w_ada, b_ada.reshape(1, n))


def _proj_kernel(x_ref, sh_ref, sc_ref, w_ref, o_ref, h_ref):
    @pl.when(pl.program_id(2) == 0)
    def _():
        h = _normalize(x_ref[...]) * (1.0 + sc_ref[...]) + sh_ref[...]
        h_ref[...] = h.astype(BF16)

    o_ref[...] = jnp.dot(h_ref[...], w_ref[...], preferred_element_type=F32)


def _mod_spec(mod, tm, d):
    if mod.shape[1] == 1:
        return pl.BlockSpec((None, 1, d), lambda b, i, *_: (b, 0, 0))
    return pl.BlockSpec((None, tm, d), lambda b, i, *_: (b, i, 0))


def _proj(x, shift, scale, w_in_bf16, tm):
    bsz, t, d = x.shape
    n = w_in_bf16.shape[1]
    tn = 1024
    return pl.pallas_call(
        _proj_kernel,
        grid=(bsz, t // tm, n // tn),
        in_specs=[pl.BlockSpec((None, tm, d), lambda b, i, j: (b, i, 0)),
                  _mod_spec(shift, tm, d), _mod_spec(scale, tm, d),
                  pl.BlockSpec((d, tn), lambda b, i, j: (0, j))],
        out_specs=pl.BlockSpec((None, tm, tn), lambda b, i, j: (b, i, j)),
        out_shape=jax.ShapeDtypeStruct((bsz, t, n), F32),
        scratch_shapes=[pltpu.VMEM((tm, d), BF16)],
        compiler_params=_params(3),
        name="proj",
    )(x, shift, scale, w_in_bf16)


def _rnn_kernel(xr_ref, yr_ref, ga_ref, cbuf_ref, h0_ref, cw_ref, cb_ref, wra_ref, bra_ref,
                wri_ref, bri_ref, lam_ref, pa_ref, cst_ref, hst_ref,
                xbuf, a_s, u_s, hcar, *, tt, tc):
    t = pl.program_id(2)
    nt = pl.num_programs(2)
    pad = SUBLANES
    keep = CONV_W - 1

    @pl.when(t == 0)
    def _():
        xbuf[pad - keep:pad, :] = cbuf_ref[...]
        hcar[...] = h0_ref[...]

    xbuf[pad:pad + tt, :] = xr_ref[...]
    cw = cw_ref[...]
    xc = cb_ref[...] + cw[0:1, :] * xbuf[pad - keep:pad - keep + tt, :]
    for j in range(1, CONV_W):
        xc = xc + cw[j:j + 1, :] * xbuf[pad - keep + j:pad - keep + j + tt, :]
    tail = xbuf[pad + tt - keep:pad + tt, :]
    xbuf[pad - keep:pad, :] = tail

    lam = lam_ref[...]
    softplus_neg = jnp.maximum(-lam, 0.0) + jnp.log1p(jnp.exp(-jnp.abs(lam)))
    for blk in range(tc // RNN_BLOCK_DIM):
        cols = slice(blk * RNN_BLOCK_DIM, (blk + 1) * RNN_BLOCK_DIM)
        xb = xc[:, cols]
        xb16 = xb.astype(BF16)
        r = jax.nn.sigmoid(jnp.dot(xb16, wra_ref[blk], preferred_element_type=F32) + bra_ref[:, cols])
        i = jax.nn.sigmoid(jnp.dot(xb16, wri_ref[blk], preferred_element_type=F32) + bri_ref[:, cols])
        log_a = -RG_C * r * softplus_neg[:, cols]
        a = jnp.exp(log_a)
        u = jnp.sqrt(jnp.tanh(-log_a) * (1.0 + a * a)) * (i * xb)
        a_s[:, cols] = a
        u_s[:, cols] = u

    row = lax.broadcasted_iota(jnp.int32, (SUBLANES, tc), 0)

    def scan_group(g, h_prev):
        rows = pl.ds(pl.multiple_of(g * SUBLANES, SUBLANES), SUBLANES)
        a = a_s[rows, :]
        u = u_s[rows, :]
        for s in (1, 2, 4):
            a_sh = pltpu.roll(a, s, 0)
            u_sh = pltpu.roll(u, s, 0)
            live = row >= s
            u = jnp.where(live, a * u_sh + u, u)
            a = jnp.where(live, a * a_sh, a)
        h = a * h_prev + u
        u_s[rows, :] = h
        return h[SUBLANES - 1:SUBLANES, :]

    h_last = lax.fori_loop(0, tt // SUBLANES, scan_group, hcar[...])
    hcar[...] = h_last

    pa_ref[...] = u_s[...] * _gelu(yr_ref[...]) * jax.nn.sigmoid(ga_ref[...])

    @pl.when(t == nt - 1)
    def _():
        cst_ref[...] = tail
        hst_ref[...] = h_last


def _rnn(proj, conv_buf, h0, conv_w, conv_b, w_ra16, b_ra, w_ri16, b_ri, lam, d, tt, tc):
    bsz, t, _ = proj.shape
    nc = d // tc
    nb = tc // RNN_BLOCK_DIM
    keep = CONV_W - 1
    col = lambda off: (lambda b, c, i: (b, i, off * nc + c))
    vec = pl.BlockSpec((1, tc), lambda b, c, i: (0, c))
    blk = pl.BlockSpec((nb, RNN_BLOCK_DIM, RNN_BLOCK_DIM), lambda b, c, i: (c, 0, 0))
    pa, cst, hst = pl.pallas_call(
        functools.partial(_rnn_kernel, tt=tt, tc=tc),
        grid=(bsz, nc, t // tt),
        in_specs=[pl.BlockSpec((None, tt, tc), col(0)),
                  pl.BlockSpec((None, tt, tc), col(1)),
                  pl.BlockSpec((None, tt, tc), col(5)),
                  pl.BlockSpec((None, keep, tc), lambda b, c, i: (b, 0, c)),
                  pl.BlockSpec((None, 1, tc), lambda b, c, i: (b, 0, c)),
                  pl.BlockSpec((CONV_W, tc), lambda b, c, i: (0, c)),
                  vec, blk, vec, blk, vec, vec],
        out_specs=[pl.BlockSpec((None, tt, tc), lambda b, c, i: (b, i, c)),
                   pl.BlockSpec((None, keep, tc), lambda b, c, i: (b, 0, c)),
                   pl.BlockSpec((None, 1, tc), lambda b, c, i: (b, 0, c))],
        out_shape=[jax.ShapeDtypeStruct((bsz, t, d), F32),
                   jax.ShapeDtypeStruct((bsz, keep, d), F32),
                   jax.ShapeDtypeStruct((bsz, 1, d), F32)],
        scratch_shapes=[pltpu.VMEM((tt + SUBLANES, tc), F32),
                        pltpu.VMEM((tt, tc), F32),
                        pltpu.VMEM((tt, tc), F32),
                        pltpu.VMEM((1, tc), F32)],
        compiler_params=_params(3),
        name="rnn",
    )(proj, proj, proj, conv_buf, h0.reshape(bsz, 1, d), conv_w, conv_b.reshape(1, d),
      w_ra16, b_ra.reshape(1, d), w_ri16, b_ri.reshape(1, d), lam.reshape(1, d))
    return pa, cst, hst.reshape(bsz, d)


def _rel_bias_tile(rel_ref, head, dist, visible):
    idx = jnp.clip(dist, -(CHUNK - 1), MAX_REL) + (CHUNK - 1)

    def body(j, acc):
        return acc + jnp.where(idx == j, rel_ref[head, j], 0.0)

    bias = lax.fori_loop(0, REL_BUCKETS, body, jnp.zeros(dist.shape, F32))
    return jnp.where(visible, bias, NEG_INF)


def _softmax_pv(s, v16):
    m = jnp.max(s, axis=-1, keepdims=True)
    p = jnp.exp(s - m)
    l = jnp.sum(p, axis=-1, keepdims=True)
    return jnp.dot(p.astype(BF16), v16, preferred_element_type=F32) / l


_QSUB = 2 * CHUNK
_BAND = _QSUB + ATT_REACH


def _attn_prompt_kernel(rel_ref, q_ref, kp_ref, kc_ref, vp_ref, vc_ref, gb_ref, o_ref,
                        bias_s, kcat, vcat):
    head, b, i = pl.program_id(0), pl.program_id(1), pl.program_id(2)
    tq = ATT_REACH

    @pl.when((b == 0) & (i == 0))
    def _():
        r = lax.broadcasted_iota(jnp.int32, (_QSUB, _BAND), 0)
        c = lax.broadcasted_iota(jnp.int32, (_QSUB, _BAND), 1)
        qc = r // CHUNK + LEFT_CHUNKS
        kc = c // CHUNK
        visible = (kc <= qc) & (kc >= qc - LEFT_CHUNKS)
        bias_s[...] = _rel_bias_tile(rel_ref, head, r + ATT_REACH - c, visible)

    kcat[0:tq, :] = kp_ref[...].astype(BF16)
    kcat[tq:2 * tq, :] = kc_ref[...].astype(BF16)
    vcat[0:tq, :] = vp_ref[...].astype(BF16)
    vcat[tq:2 * tq, :] = vc_ref[...].astype(BF16)
    col = lax.broadcasted_iota(jnp.int32, (_QSUB, _BAND), 1)
    n_sub = tq // _QSUB
    scores = []
    for s in range(n_sub):
        q16 = q_ref[s * _QSUB:(s + 1) * _QSUB, :].astype(BF16)
        sc = lax.dot_general(q16, kcat[s * _QSUB:s * _QSUB + _BAND, :], (((1,), (1,)), ((), ())),
                             preferred_element_type=F32) * (HEAD_DIM ** -0.5)
        sc = sc + bias_s[...]
        k_pos = col + (i * tq + s * _QSUB - ATT_REACH)
        scores.append(jnp.where(k_pos >= 0, sc, NEG_INF))
    maxes = [jnp.max(sc, axis=-1, keepdims=True) for sc in scores]
    probs = [jnp.exp(sc - m) for sc, m in zip(scores, maxes)]
    sums = [jnp.sum(p, axis=-1, keepdims=True) for p in probs]
    outs = [jnp.dot(p.astype(BF16), vcat[s * _QSUB:s * _QSUB + _BAND, :], preferred_element_type=F32)
            for s, p in enumerate(probs)]
    for s in range(n_sub):
        rows = slice(s * _QSUB, (s + 1) * _QSUB)
        o_ref[rows, :] = outs[s] / sums[s] * jax.nn.sigmoid(gb_ref[rows, :])


def _attn_prompt(proj, rel_bias, d):
    bsz, t, _ = proj.shape
    nh = d // HEAD_DIM
    tq = ATT_REACH
    cur = lambda off: (lambda h, b, i: (b, i, off * nh + h))
    prev = lambda off: (lambda h, b, i: (b, jnp.maximum(i - 1, 0), off * nh + h))
    blk = lambda f: pl.BlockSpec((None, tq, HEAD_DIM), f)
    return pl.pallas_call(
        _attn_prompt_kernel,
        grid=(nh, bsz, t // tq),
        in_specs=[pl.BlockSpec(memory_space=pltpu.SMEM),
                  blk(cur(2)), blk(prev(3)), blk(cur(3)), blk(prev(4)), blk(cur(4)), blk(cur(6))],
        out_specs=pl.BlockSpec((None, tq, HEAD_DIM), lambda h, b, i: (b, i, h)),
        out_shape=jax.ShapeDtypeStruct((bsz, t, d), F32),
        scratch_shapes=[pltpu.VMEM((_QSUB, _BAND), F32),
                        pltpu.VMEM((2 * tq, HEAD_DIM), BF16),
                        pltpu.VMEM((2 * tq, HEAD_DIM), BF16)],
        compiler_params=_params(3),
        name="attn_prompt",
    )(rel_bias, proj, proj, proj, proj, proj, proj)


def _attn_sample_kernel(rel_ref, q_ref, kn_ref, vn_ref, gb_ref, ck_ref, cv_ref, o_ref,
                        bias_past_s, bias_new_s, *, t, n_past):
    head = pl.program_id(0)

    def bias(n_keys, first_pos):
        r = lax.broadcasted_iota(jnp.int32, (t, n_keys), 0)
        c = lax.broadcasted_iota(jnp.int32, (t, n_keys), 1)
        q_pos = PAST_LEN + r
        k_pos = first_pos + c
        qc = q_pos // CHUNK
        kc = k_pos // CHUNK
        visible = (k_pos >= 0) & (kc <= qc) & (kc >= qc - LEFT_CHUNKS)
        return _rel_bias_tile(rel_ref, head, q_pos - k_pos, visible)

    @pl.when(pl.program_id(1) == 0)
    def _():
        bias_past_s[...] = bias(n_past, PAST_LEN - n_past)
        bias_new_s[...] = bias(t, PAST_LEN)

    scale = HEAD_DIM ** -0.5
    q16 = q_ref[...].astype(BF16)
    nt = (((1,), (1,)), ((), ()))
    s_past = lax.dot_general(q16, ck_ref[...].astype(BF16), nt, preferred_element_type=F32) * scale
    s_new = lax.dot_general(q16, kn_ref[...].astype(BF16), nt, preferred_element_type=F32) * scale
    s_past = s_past + bias_past_s[...]
    s_new = s_new + bias_new_s[...]
    m = jnp.maximum(jnp.max(s_past, axis=-1, keepdims=True), jnp.max(s_new, axis=-1, keepdims=True))
    p_past = jnp.exp(s_past - m)
    p_new = jnp.exp(s_new - m)
    l = jnp.sum(p_past, axis=-1, keepdims=True) + jnp.sum(p_new, axis=-1, keepdims=True)
    o = (jnp.dot(p_past.astype(BF16), cv_ref[...].astype(BF16), preferred_element_type=F32)
         + jnp.dot(p_new.astype(BF16), vn_ref[...].astype(BF16), preferred_element_type=F32)) / l
    o_ref[...] = o * jax.nn.sigmoid(gb_ref[...])


def _attn_sample(proj, cache_k, cache_v, rel_bias, d):
    bsz, t, _ = proj.shape
    nh = d // HEAD_DIM
    n_past = cache_k.shape[1]
    new = lambda off: pl.BlockSpec((None, t, HEAD_DIM), lambda h, b: (b, 0, off * nh + h))
    past = pl.BlockSpec((None, n_past, HEAD_DIM), lambda h, b: (b, 0, h))
    return pl.pallas_call(
        functools.partial(_attn_sample_kernel, t=t, n_past=n_past),
        grid=(nh, bsz),
        in_specs=[pl.BlockSpec(memory_space=pltpu.SMEM), new(2), new(3), new(4), new(6), past, past],
        out_specs=pl.BlockSpec((None, t, HEAD_DIM), lambda h, b: (b, 0, h)),
        out_shape=jax.ShapeDtypeStruct((bsz, t, d), F32),
        scratch_shapes=[pltpu.VMEM((t, n_past), F32), pltpu.VMEM((t, t), F32)],
        compiler_params=_params(2),
        name="attn_sample",
    )(rel_bias, proj, proj, proj, proj, cache_k.reshape(bsz, n_past, d), cache_v.reshape(bsz, n_past, d))


def _mix_kernel(pa_ref, pb_ref, x_ref, w_ref, g1_ref, sh2_ref, sc2_ref, lg_ref, lb_ref,
                x1_ref, h2_ref):
    mixed = jnp.dot((pa_ref[...] + pb_ref[...]).astype(BF16), w_ref[...], preferred_element_type=F32)
    x1 = _normalize(ALPHA * x_ref[...] + g1_ref[...] * mixed) * lg_ref[...] + lb_ref[...]
    x1_ref[...] = x1
    h2_ref[...] = (_normalize(x1) * (1.0 + sc2_ref[...]) + sh2_ref[...]).astype(BF16)


def _mix(pa, pb, x, w_out16, gate1, shift2, scale2, ln_g, ln_b, tm):
    bsz, t, d = x.shape
    tok = pl.BlockSpec((None, tm, d), lambda b, i: (b, i, 0))
    vec = pl.BlockSpec((1, d), lambda b, i: (0, 0))
    return pl.pallas_call(
        _mix_kernel,
        grid=(bsz, t // tm),
        in_specs=[tok, tok, tok, pl.BlockSpec((d, d), lambda b, i: (0, 0)),
                  _mod_spec(gate1, tm, d), _mod_spec(shift2, tm, d), _mod_spec(scale2, tm, d), vec, vec],
        out_specs=[tok, tok],
        out_shape=[jax.ShapeDtypeStruct((bsz, t, d), F32), jax.ShapeDtypeStruct((bsz, t, d), BF16)],
        compiler_params=_params(2),
        name="mix",
    )(pa, pb, x, w_out16, gate1, shift2, scale2, ln_g.reshape(1, d), ln_b.reshape(1, d))


def _take_top(s, n_iter, rows_f, on_pick):
    def body(it, s_rem):
        m = jnp.max(s_rem, axis=0, keepdims=True)
        first = jnp.min(jnp.where(s_rem == m, rows_f, float(s.shape[0])), axis=0, keepdims=True)
        pick = rows_f == first
        on_pick(it, m, pick)
        return jnp.where(pick, -jnp.inf, s_rem)

    lax.fori_loop(0, n_iter, body, s)


def _route_kernel(h2_ref, wpq_ref, keys_ref, lcol_ref, e1_ref, r2_ref, e2_ref,
                  q_s, rank_s, vals_s, cand_s, sel_s, *, tm):
    q = jnp.dot(h2_ref[...], wpq_ref[...], preferred_element_type=F32)
    for hp in range(2 * PEER_HEADS):
        q_s[hp] = q[:, hp * N_KEYS:(hp + 1) * N_KEYS].astype(BF16)
    rows_f = lax.broadcasted_iota(jnp.int32, (N_KEYS, tm), 0).astype(F32)
    crow_f = lax.broadcasted_iota(jnp.int32, (_N_CAND_PAD, tm), 0).astype(F32)
    nt = (((1,), (1,)), ((), ()))

    def per_head(h, carry):
        scores = []
        for p in range(2):
            s = lax.dot_general(keys_ref[h * 2 + p], q_s[h * 2 + p], nt,
                                preferred_element_type=F32)
            scores.append(s)
            rank_s[p] = jnp.full((N_KEYS, tm), float(PEER_TOPK), F32)

            def on_pick(it, m, pick, p=p):
                vals_s[p, pl.ds(it, 1), :] = m
                rank_s[p] = jnp.where(pick, it.astype(F32), rank_s[p])

            _take_top(s, PEER_TOPK, rows_f, on_pick)

        cand_s[...] = jnp.full((_N_CAND_PAD, tm), -jnp.inf, F32)
        for r1 in range(PEER_TOPK):
            off, cnt = _CAND_OFFSETS[r1], _CAND_COUNTS[r1]
            cand_s[off:off + cnt, :] = vals_s[0, r1:r1 + 1, :] + vals_s[1, 0:cnt, :]
        sel_s[...] = jnp.zeros((_N_CAND_PAD, tm), F32)

        def on_pick_cand(it, m, pick):
            sel_s[...] = jnp.where(pick, 1.0, sel_s[...])

        _take_top(cand_s[...], PEER_TOPK, crow_f, on_pick_cand)

        sel = sel_s[...]
        top = vals_s[0, 0:1, :] + vals_s[1, 0:1, :]
        z = jnp.sum(jnp.where(sel > 0.0, jnp.exp(cand_s[...] - top), 0.0), axis=0, keepdims=True)
        rank1 = rank_s[0]
        lcol = jnp.zeros((N_KEYS, tm), F32)
        for r1 in range(PEER_TOPK):
            off, cnt = _CAND_OFFSETS[r1], _CAND_COUNTS[r1]
            n_sel = jnp.sum(sel[off:off + cnt, :], axis=0, keepdims=True)
            lcol = lcol + jnp.where(rank1 == float(r1), n_sel, 0.0)
        lcol_ref[h] = lcol
        e1_ref[h] = jnp.exp(scores[0] - vals_s[0, 0:1, :]) / z
        r2_ref[h] = rank_s[1].astype(BF16)
        e2_ref[h] = jnp.exp(scores[1] - vals_s[1, 0:1, :]).astype(BF16)
        return carry

    lax.fori_loop(0, PEER_HEADS, per_head, 0)


def _route(h2, w_pq16, keys16, tm):
    bsz, t, d = h2.shape
    nq = w_pq16.shape[1]
    out = pl.BlockSpec((None, PEER_HEADS, N_KEYS, tm), lambda b, i: (b, 0, 0, i))
    shape = jax.ShapeDtypeStruct((bsz, PEER_HEADS, N_KEYS, t), F32)
    shape16 = jax.ShapeDtypeStruct((bsz, PEER_HEADS, N_KEYS, t), BF16)
    return pl.pallas_call(
        functools.partial(_route_kernel, tm=tm),
        grid=(bsz, t // tm),
        in_specs=[pl.BlockSpec((None, tm, d), lambda b, i: (b, i, 0)),
                  pl.BlockSpec((d, nq), lambda b, i: (0, 0)),
                  pl.BlockSpec((2 * PEER_HEADS, N_KEYS, nq // (2 * PEER_HEADS)), lambda b, i: (0, 0, 0))],
        out_specs=[out, out, out, out],
        out_shape=[shape, shape, shape16, shape16],
        scratch_shapes=[pltpu.VMEM((2 * PEER_HEADS, tm, nq // (2 * PEER_HEADS)), BF16),
                        pltpu.VMEM((2, N_KEYS, tm), F32),
                        pltpu.VMEM((2, PEER_TOPK, tm), F32),
                        pltpu.VMEM((_N_CAND_PAD, tm), F32),
                        pltpu.VMEM((_N_CAND_PAD, tm), F32)],
        compiler_params=_params(2),
        name="route",
    )(h2, w_pq16, keys16)


_ACT_ROWS = 256


def _peer_kernel(h2_ref, u_ref, vt_ref, lcol_ref, e1_ref, r2_ref, e2_ref, x1_ref, g2_ref,
                 lg_ref, lb_ref, y_ref, acc, p_s, *, te):
    j = pl.program_id(2)

    @pl.when(j == 0)
    def _():
        acc[...] = jnp.zeros(acc.shape, F32)

    zero = jnp.zeros((), BF16)
    n_sub = _ACT_ROWS // N_KEYS
    for c in range(te // _ACT_ROWS):
        chunk = slice(c * _ACT_ROWS, (c + 1) * _ACT_ROWS)
        act = lax.dot_general(u_ref[chunk, :], h2_ref[...], (((1,), (1,)), ((), ())),
                              preferred_element_type=F32)
        for sub in range(n_sub):
            al = c * n_sub + sub
            w = None
            for h in range(PEER_HEADS):
                keep = r2_ref[h] < lcol_ref[h, al:al + 1, :].astype(BF16)
                term = jnp.where(keep, e2_ref[h] * e1_ref[h, al:al + 1, :].astype(BF16), zero)
                w = term if w is None else w + term
            g = _gelu(act[sub * N_KEYS:(sub + 1) * N_KEYS, :])
            p_s[al * N_KEYS:(al + 1) * N_KEYS, :] = (w.astype(F32) * g).astype(BF16)
    acc[...] += jnp.dot(vt_ref[...], p_s[...], preferred_element_type=F32)

    @pl.when(j == pl.num_programs(2) - 1)
    def _():
        ffn = acc[...].T
        y = _normalize(ALPHA * x1_ref[...] + g2_ref[...] * ffn)
        y_ref[...] = y * lg_ref[...] + lb_ref[...]


def _peer(h2, u16, vt16, lcol, e1, r2, e2, x1, gate2, ln_g, ln_b, tm, te):
    bsz, t, d = h2.shape
    n_exp = u16.shape[0]
    na = te // N_KEYS
    tok = pl.BlockSpec((None, tm, d), lambda b, i, j: (b, i, 0))
    tok1 = pl.BlockSpec((None, tm, d), lambda b, i, j: (b, i, 0), pipeline_mode=pl.Buffered(1))
    vec = pl.BlockSpec((1, d), lambda b, i, j: (0, 0))
    by_a =pl.BlockSpec((None, PEER_HEADS, na, tm), lambda b, i, j: (b, 0, j, i))
    by_b = pl.BlockSpec((None, PEER_HEADS, N_KEYS, tm), lambda b, i, j: (b, 0, 0, i))
    return pl.pallas_call(
        functools.partial(_peer_kernel, te=te),
        grid=(bsz, t // tm, n_exp // te),
        in_specs=[tok,
                  pl.BlockSpec((te, d), lambda b, i, j: (j, 0)),
                  pl.BlockSpec((d, te), lambda b, i, j: (0, j)),
                  by_a, by_a, by_b, by_b, tok1, _mod_spec(gate2, tm, d), vec, vec],
        out_specs=tok1,
        out_shape=jax.ShapeDtypeStruct((bsz, t, d), F32),
        scratch_shapes=[pltpu.VMEM((d, tm), F32), pltpu.VMEM((te, tm), BF16)],
        compiler_params=_params(3),
        name="peer",
    )(h2, u16, vt16, lcol, e1, r2, e2, x1, gate2, ln_g.reshape(1, d), ln_b.reshape(1, d))


def _layer(x, mods, conv_buf, h0, attend, w, *, tm, tm_mix, tt, tc, tm_route, tm_peer, te, flatten):
    bsz, t, d = x.shape
    if flatten:
        mods = [jnp.broadcast_to(m, (bsz, t, d)).reshape(1, bsz * t, d) for m in mods]
        flat = lambda a: a.reshape(1, bsz * t, a.shape[-1])
    else:
        flat = lambda a: a
    shift1, scale1, gate1, shift2, scale2, gate2 = mods
    proj = _proj(flat(x), shift1, scale1, w["w_in"], tm).reshape(bsz, t, -1)
    pa, conv_state, rnn_state = _rnn(proj, conv_buf, h0, w["conv_w"], w["conv_b"], w["w_ra"], w["b_ra"],
                                     w["w_ri"], w["b_ri"], w["lam"], d, tt, tc)
    pb = attend(proj)
    x1, h2 = _mix(flat(pa), flat(pb), flat(x), w["w_out"], gate1, shift2, scale2, w["ln1_g"], w["ln1_b"], tm_mix)
    lcol, e1, r2, e2 = _route(h2, w["w_pq"], w["keys"], tm_route)
    y = _peer(h2, w["peer_u"], w["peer_vt"], lcol, e1, r2, e2, x1, gate2, w["ln2_g"], w["ln2_b"], tm_peer, te)
    n_heads = d // HEAD_DIM
    k = proj[:, :, 3 * d:4 * d].reshape(bsz, t, n_heads, HEAD_DIM)
    v = proj[:, :, 4 * d:5 * d].reshape(bsz, t, n_heads, HEAD_DIM)
    return y.reshape(bsz, t, d), conv_state, rnn_state, k, v


def kernel(x_prompt, x_sample, c_prompt, c_sample, state_conv, state_rnn, cache_k, cache_v, w_ada, b_ada, w_in, conv_w, conv_b, w_ra, b_ra, w_ri, b_ri, rg_lambda, rel_bias, w_out, ln1_g, ln1_b, w_pq, peer_keys, peer_u, peer_v, ln2_g, ln2_b):
    bsz_p, t_p, d = x_prompt.shape
    bsz_s, t_s, _ = x_sample.shape
    depth = w_ada.shape[0]
    keep = min(ATT_REACH, t_p)
    y_p, y_s = x_prompt, x_sample
    outs = [[] for _ in range(8)]
    rows = bsz_p + bsz_s
    rows_pad = -(-rows // SUBLANES) * SUBLANES
    c_all = jnp.concatenate([c_prompt, c_sample, jnp.zeros((rows_pad - rows, d), F32)], axis=0)
    tm = min(1024, t_p)
    for l in range(depth):
        w = dict(w_in=w_in[l].astype(BF16), conv_w=conv_w[l], conv_b=conv_b[l],
                 w_ra=w_ra[l].astype(BF16), b_ra=b_ra[l], w_ri=w_ri[l].astype(BF16), b_ri=b_ri[l],
                 lam=rg_lambda[l], w_out=w_out[l].astype(BF16), ln1_g=ln1_g[l], ln1_b=ln1_b[l],
                 w_pq=w_pq[l].astype(BF16),
                 keys=peer_keys[l].astype(BF16).reshape(2 * PEER_HEADS, N_KEYS, -1),
                 peer_u=peer_u[l].astype(BF16), peer_vt=peer_v[l].astype(BF16).T,
                 ln2_g=ln2_g[l], ln2_b=ln2_b[l])
        mod = _ada(c_all, w_ada[l], b_ada[l])
        mods_p = [m[:, None, :] for m in jnp.split(mod[:bsz_p], 6, axis=-1)]
        mods_s = [m[:, None, :] for m in jnp.split(mod[bsz_p:rows], 6, axis=-1)]

        y_p, cs, hs, k, v = _layer(
            y_p, mods_p, jnp.zeros((bsz_p, CONV_W - 1, d), F32), jnp.zeros((bsz_p, d), F32),
            lambda proj: _attn_prompt(proj, rel_bias[l], d), w,
            tm=tm, tm_mix=min(256, t_p), tt=min(256, t_p), tc=512, tm_route=256, tm_peer=512, te=1024, flatten=False)
        for dst, val in zip(outs[:4], (cs, hs, k[:, -keep:], v[:, -keep:])):
            dst.append(val)

        y_s, cs, hs, k, v = _layer(
            y_s, mods_s, state_conv[l], state_rnn[l],
            lambda proj: _attn_sample(proj, cache_k[l], cache_v[l], rel_bias[l], d), w,
            tm=bsz_s * t_s, tm_mix=bsz_s * t_s, tt=t_s, tc=512, tm_route=bsz_s * t_s, tm_peer=bsz_s * t_s, te=1024, flatten=True)
        for dst, val in zip(outs[4:], (cs, hs, k, v)):
            dst.append(val)
    return (y_p, y_s) + tuple(jnp.stack(o) for o in outs)
```

```python
import functools
import math

import jax
import jax.numpy as jnp
from jax import lax
from jax.experimental import pallas as pl
from jax.experimental.pallas import tpu as pltpu

F32 = jnp.float32
BF16 = jnp.bfloat16

CHUNK = 64
LEFT_CHUNKS = 8
ATT_REACH = LEFT_CHUNKS * CHUNK
MAX_REL = 128
REL_BUCKETS = MAX_REL + CHUNK
PAST_LEN = 1024
CONV_W = 4
RG_C = 8.0
RNN_BLOCK_DIM = 128
HEAD_DIM = 128
N_KEYS = 128
PEER_HEADS = 8
PEER_TOPK = 16
DEPTH = 1
ALPHA = (2.0 * DEPTH) ** 0.25
LN_EPS = 1e-5
NEG_INF = -1e30

LANES = 128
SUBLANES = 8
VMEM_LIMIT = 56 * 1024 * 1024

_CAND_COUNTS = tuple(PEER_TOPK // (r1 + 1) for r1 in range(PEER_TOPK))
_CAND_OFFSETS = tuple(sum(_CAND_COUNTS[:r1]) for r1 in range(PEER_TOPK))
_N_CAND = sum(_CAND_COUNTS)
_N_CAND_PAD = -(-_N_CAND // SUBLANES) * SUBLANES


def _params(n_grid):
    return pltpu.CompilerParams(dimension_semantics=("arbitrary",) * n_grid,
                                vmem_limit_bytes=VMEM_LIMIT)


def _normalize(x):
    mu = jnp.mean(x, axis=-1, keepdims=True)
    xc = x - mu
    var = jnp.mean(xc * xc, axis=-1, keepdims=True)
    return xc * lax.rsqrt(var + LN_EPS)


def _gelu(x):
    return 0.5 * x * (1.0 + lax.erf(x * (2.0 ** -0.5)))


def _ada_kernel(c_ref, w_ref, b_ref, o_ref):
    c = c_ref[...]
    s = c * jax.nn.sigmoid(c)
    o_ref[...] = jnp.dot(s, w_ref[...], preferred_element_type=F32) + b_ref[...]


def _ada(c, w_ada, b_ada):
    rows, d = c.shape
    n = w_ada.shape[1]
    tn = 1024
    return pl.pallas_call(
        _ada_kernel,
        grid=(n // tn,),
        in_specs=[pl.BlockSpec((rows, d), lambda j: (0, 0)),
                  pl.BlockSpec((d, tn), lambda j: (0, j)),
                  pl.BlockSpec((1, tn), lambda j: (0, j))],
        out_specs=pl.BlockSpec((rows, tn), lambda j: (0, j)),
        out_shape=jax.ShapeDtypeStruct((rows, n), F32),
        compiler_params=_params(1),
        name="ada",
    )(c, w_ada, b_ada.reshape(1, n))


def _proj_kernel(x_ref, sh_ref, sc_ref, w_ref, o_ref, h_ref):
    @pl.when(pl.program_id(2) == 0)
    def _():
        h = _normalize(x_ref[...]) * (1.0 + sc_ref[...]) + sh_ref[...]
        h_ref[...] = h.astype(BF16)

    o_ref[...] = jnp.dot(h_ref[...], w_ref[...], preferred_element_type=F32)


def _mod_spec(mod, tm, d):
    if mod.shape[1] == 1:
        return pl.BlockSpec((None, 1, d), lambda b, i, *_: (b, 0, 0))
    return pl.BlockSpec((None, tm, d), lambda b, i, *_: (b, i, 0))


def _proj(x, shift, scale, w_in_bf16, tm):
    bsz, t, d = x.shape
    n = w_in_bf16.shape[1]
    tn = 1024
    return pl.pallas_call(
        _proj_kernel,
        grid=(bsz, t // tm, n // tn),
        in_specs=[pl.BlockSpec((None, tm, d), lambda b, i, j: (b, i, 0)),
                  _mod_spec(shift, tm, d), _mod_spec(scale, tm, d),
                  pl.BlockSpec((d, tn), lambda b, i, j: (0, j))],
        out_specs=pl.BlockSpec((None, tm, tn), lambda b, i, j: (b, i, j)),
        out_shape=jax.ShapeDtypeStruct((bsz, t, n), F32),
        scratch_shapes=[pltpu.VMEM((tm, d), BF16)],
        compiler_params=_params(3),
        name="proj",
    )(x, shift, scale, w_in_bf16)


def _rnn_kernel(xr_ref, yr_ref, ga_ref, cbuf_ref, h0_ref, cw_ref, cb_ref, wra_ref, bra_ref,
                wri_ref, bri_ref, lam_ref, pa_ref, cst_ref, hst_ref,
                xbuf, a_s, u_s, hcar, *, tt, tc):
    t = pl.program_id(2)
    nt = pl.num_programs(2)
    pad = SUBLANES
    keep = CONV_W - 1

    @pl.when(t == 0)
    def _():
        xbuf[pad - keep:pad, :] = cbuf_ref[...]
        hcar[...] = h0_ref[...]

    xbuf[pad:pad + tt, :] = xr_ref[...]
    cw = cw_ref[...]
    xc = cb_ref[...] + cw[0:1, :] * xbuf[pad - keep:pad - keep + tt, :]
    for j in range(1, CONV_W):
        xc = xc + cw[j:j + 1, :] * xbuf[pad - keep + j:pad - keep + j + tt, :]
    tail = xbuf[pad + tt - keep:pad + tt, :]
    xbuf[pad - keep:pad, :] = tail

    lam = lam_ref[...]
    softplus_neg = jnp.maximum(-lam, 0.0) + jnp.log1p(jnp.exp(-jnp.abs(lam)))
    for blk in range(tc // RNN_BLOCK_DIM):
        cols = slice(blk * RNN_BLOCK_DIM, (blk + 1) * RNN_BLOCK_DIM)
        xb = xc[:, cols]
        xb16 = xb.astype(BF16)
        r = jax.nn.sigmoid(jnp.dot(xb16, wra_ref[blk], preferred_element_type=F32) + bra_ref[:, cols])
        i = jax.nn.sigmoid(jnp.dot(xb16, wri_ref[blk], preferred_element_type=F32) + bri_ref[:, cols])
        log_a = -RG_C * r * softplus_neg[:, cols]
        a = jnp.exp(log_a)
        u = jnp.sqrt(jnp.tanh(-log_a) * (1.0 + a * a)) * (i * xb)
        a_s[:, cols] = a
        u_s[:, cols] = u

    row = lax.broadcasted_iota(jnp.int32, (SUBLANES, tc), 0)

    def scan_group(g, h_prev):
        rows = pl.ds(pl.multiple_of(g * SUBLANES, SUBLANES), SUBLANES)
        a = a_s[rows, :]
        u = u_s[rows, :]
        for s in (1, 2, 4):
            a_sh = pltpu.roll(a, s, 0)
            u_sh = pltpu.roll(u, s, 0)
            live = row >= s
            u = jnp.where(live, a * u_sh + u, u)
            a = jnp.where(live, a * a_sh, a)
        h = a * h_prev + u
        u_s[rows, :] = h
        return h[SUBLANES - 1:SUBLANES, :]

    h_last = lax.fori_loop(0, tt // SUBLANES, scan_group, hcar[...])
    hcar[...] = h_last

    pa_ref[...] = u_s[...] * _gelu(yr_ref[...]) * jax.nn.sigmoid(ga_ref[...])

    @pl.when(t == nt - 1)
    def _():
        cst_ref[...] = tail
        hst_ref[...] = h_last


def _rnn(proj, conv_buf, h0, conv_w, conv_b, w_ra16, b_ra, w_ri16, b_ri, lam, d, tt, tc):
    bsz, t, _ = proj.shape
    nc = d // tc
    nb = tc // RNN_BLOCK_DIM
    keep = CONV_W - 1
    col = lambda off: (lambda b, c, i: (b, i, off * nc + c))
    vec = pl.BlockSpec((1, tc), lambda b, c, i: (0, c))
    blk = pl.BlockSpec((nb, RNN_BLOCK_DIM, RNN_BLOCK_DIM), lambda b, c, i: (c, 0, 0))
    pa, cst, hst = pl.pallas_call(
        functools.partial(_rnn_kernel, tt=tt, tc=tc),
        grid=(bsz, nc, t // tt),
        in_specs=[pl.BlockSpec((None, tt, tc), col(0)),
                  pl.BlockSpec((None, tt, tc), col(1)),
                  pl.BlockSpec((None, tt, tc), col(5)),
                  pl.BlockSpec((None, keep, tc), lambda b, c, i: (b, 0, c)),
                  pl.BlockSpec((None, 1, tc), lambda b, c, i: (b, 0, c)),
                  pl.BlockSpec((CONV_W, tc), lambda b, c, i: (0, c)),
                  vec, blk, vec, blk, vec, vec],
        out_specs=[pl.BlockSpec((None, tt, tc), lambda b, c, i: (b, i, c)),
                   pl.BlockSpec((None, keep, tc), lambda b, c, i: (b, 0, c)),
                   pl.BlockSpec((None, 1, tc), lambda b, c, i: (b, 0, c))],
        out_shape=[jax.ShapeDtypeStruct((bsz, t, d), F32),
                   jax.ShapeDtypeStruct((bsz, keep, d), F32),
                   jax.ShapeDtypeStruct((bsz, 1, d), F32)],
        scratch_shapes=[pltpu.VMEM((tt + SUBLANES, tc), F32),
                        pltpu.VMEM((tt, tc), F32),
                        pltpu.VMEM((tt, tc), F32),
                        pltpu.VMEM((1, tc), F32)],
        compiler_params=_params(3),
        name="rnn",
    )(proj, proj, proj, conv_buf, h0.reshape(bsz, 1, d), conv_w, conv_b.reshape(1, d),
      w_ra16, b_ra.reshape(1, d), w_ri16, b_ri.reshape(1, d), lam.reshape(1, d))
    return pa, cst, hst.reshape(bsz, d)


def _rel_bias_tile(rel_ref, head, dist, visible):
    idx = jnp.clip(dist, -(CHUNK - 1), MAX_REL) + (CHUNK - 1)

    def body(j, acc):
        return acc + jnp.where(idx == j, rel_ref[head, j], 0.0)

    bias = lax.fori_loop(0, REL_BUCKETS, body, jnp.zeros(dist.shape, F32))
    return jnp.where(visible, bias, NEG_INF)


def _softmax_pv(s, v16):
    m = jnp.max(s, axis=-1, keepdims=True)
    p = jnp.exp(s - m)
    l = jnp.sum(p, axis=-1, keepdims=True)
    return jnp.dot(p.astype(BF16), v16, preferred_element_type=F32) / l


_QSUB = 2 * CHUNK
_BAND = _QSUB + ATT_REACH


def _attn_prompt_kernel(rel_ref, q_ref, kp_ref, kc_ref, vp_ref, vc_ref, gb_ref, o_ref,
                        bias_s, kcat, vcat):
    head, b, i = pl.program_id(0), pl.program_id(1), pl.program_id(2)
    tq = ATT_REACH

    @pl.when((b == 0) & (i == 0))
    def _():
        r = lax.broadcasted_iota(jnp.int32, (_QSUB, _BAND), 0)
        c = lax.broadcasted_iota(jnp.int32, (_QSUB, _BAND), 1)
        qc = r // CHUNK + LEFT_CHUNKS
        kc = c // CHUNK
        visible = (kc <= qc) & (kc >= qc - LEFT_CHUNKS)
        bias_s[...] = _rel_bias_tile(rel_ref, head, r + ATT_REACH - c, visible)

    kcat[0:tq, :] = kp_ref[...].astype(BF16)
    kcat[tq:2 * tq, :] = kc_ref[...].astype(BF16)
    vcat[0:tq, :] = vp_ref[...].astype(BF16)
    vcat[tq:2 * tq, :] = vc_ref[...].astype(BF16)
    col = lax.broadcasted_iota(jnp.int32, (_QSUB, _BAND), 1)
    n_sub = tq // _QSUB
    scores = []
    for s in range(n_sub):
        q16 = q_ref[s * _QSUB:(s + 1) * _QSUB, :].astype(BF16)
        sc = lax.dot_general(q16, kcat[s * _QSUB:s * _QSUB + _BAND, :], (((1,), (1,)), ((), ())),
                             preferred_element_type=F32) * (HEAD_DIM ** -0.5)
        sc = sc + bias_s[...]
        k_pos = col + (i * tq + s * _QSUB - ATT_REACH)
        scores.append(jnp.where(k_pos >= 0, sc, NEG_INF))
    maxes = [jnp.max(sc, axis=-1, keepdims=True) for sc in scores]
    probs = [jnp.exp(sc - m) for sc, m in zip(scores, maxes)]
    sums = [jnp.sum(p, axis=-1, keepdims=True) for p in probs]
    outs = [jnp.dot(p.astype(BF16), vcat[s * _QSUB:s * _QSUB + _BAND, :], preferred_element_type=F32)
            for s, p in enumerate(probs)]
    for s in range(n_sub):
        rows = slice(s * _QSUB, (s + 1) * _QSUB)
        o_ref[rows, :] = outs[s] / sums[s] * jax.nn.sigmoid(gb_ref[rows, :])


def _attn_prompt(proj, rel_bias, d):
    bsz, t, _ = proj.shape
    nh = d // HEAD_DIM
    tq = ATT_REACH
    cur = lambda off: (lambda h, b, i: (b, i, off * nh + h))
    prev = lambda off: (lambda h, b, i: (b, jnp.maximum(i - 1, 0), off * nh + h))
    blk = lambda f: pl.BlockSpec((None, tq, HEAD_DIM), f)
    return pl.pallas_call(
        _attn_prompt_kernel,
        grid=(nh, bsz, t // tq),
        in_specs=[pl.BlockSpec(memory_space=pltpu.SMEM),
                  blk(cur(2)), blk(prev(3)), blk(cur(3)), blk(prev(4)), blk(cur(4)), blk(cur(6))],
        out_specs=pl.BlockSpec((None, tq, HEAD_DIM), lambda h, b, i: (b, i, h)),
        out_shape=jax.ShapeDtypeStruct((bsz, t, d), F32),
        scratch_shapes=[pltpu.VMEM((_QSUB, _BAND), F32),
                        pltpu.VMEM((2 * tq, HEAD_DIM), BF16),
                        pltpu.VMEM((2 * tq, HEAD_DIM), BF16)],
        compiler_params=_params(3),
        name="attn_prompt",
    )(rel_bias, proj, proj, proj, proj, proj, proj)


def _attn_sample_kernel(rel_ref, q_ref, kn_ref, vn_ref, gb_ref, ck_ref, cv_ref, o_ref,
                        bias_past_s, bias_new_s, *, t, n_past):
    head = pl.program_id(0)

    def bias(n_keys, first_pos):
        r = lax.broadcasted_iota(jnp.int32, (t, n_keys), 0)
        c = lax.broadcasted_iota(jnp.int32, (t, n_keys), 1)
        q_pos = PAST_LEN + r
        k_pos = first_pos + c
        qc = q_pos // CHUNK
        kc = k_pos // CHUNK
        visible = (k_pos >= 0) & (kc <= qc) & (kc >= qc - LEFT_CHUNKS)
        return _rel_bias_tile(rel_ref, head, q_pos - k_pos, visible)

    @pl.when(pl.program_id(1) == 0)
    def _():
        bias_past_s[...] = bias(n_past, PAST_LEN - n_past)
        bias_new_s[...] = bias(t, PAST_LEN)

    scale = HEAD_DIM ** -0.5
    q16 = q_ref[...].astype(BF16)
    nt = (((1,), (1,)), ((), ()))
    s_past = lax.dot_general(q16, ck_ref[...].astype(BF16), nt, preferred_element_type=F32) * scale
    s_new = lax.dot_general(q16, kn_ref[...].astype(BF16), nt, preferred_element_type=F32) * scale
    s_past = s_past + bias_past_s[...]
    s_new = s_new + bias_new_s[...]
    m = jnp.maximum(jnp.max(s_past, axis=-1, keepdims=True), jnp.max(s_new, axis=-1, keepdims=True))
    p_past = jnp.exp(s_past - m)
    p_new = jnp.exp(s_new - m)
    l = jnp.sum(p_past, axis=-1, keepdims=True) + jnp.sum(p_new, axis=-1, keepdims=True)
    o = (jnp.dot(p_past.astype(BF16), cv_ref[...].astype(BF16), preferred_element_type=F32)
         + jnp.dot(p_new.astype(BF16), vn_ref[...].astype(BF16), preferred_element_type=F32)) / l
    o_ref[...] = o * jax.nn.sigmoid(gb_ref[...])


def _attn_sample(proj, cache_k, cache_v, rel_bias, d):
    bsz, t, _ = proj.shape
    nh = d // HEAD_DIM
    n_past = cache_k.shape[1]
    new = lambda off: pl.BlockSpec((None, t, HEAD_DIM), lambda h, b: (b, 0, off * nh + h))
    past = pl.BlockSpec((None, n_past, HEAD_DIM), lambda h, b: (b, 0, h))
    return pl.pallas_call(
        functools.partial(_attn_sample_kernel, t=t, n_past=n_past),
        grid=(nh, bsz),
        in_specs=[pl.BlockSpec(memory_space=pltpu.SMEM), new(2), new(3), new(4), new(6), past, past],
        out_specs=pl.BlockSpec((None, t, HEAD_DIM), lambda h, b: (b, 0, h)),
        out_shape=jax.ShapeDtypeStruct((bsz, t, d), F32),
        scratch_shapes=[pltpu.VMEM((t, n_past), F32), pltpu.VMEM((t, t), F32)],
        compiler_params=_params(2),
        name="attn_sample",
    )(rel_bias, proj, proj, proj, proj, cache_k.reshape(bsz, n_past, d), cache_v.reshape(bsz, n_past, d))


def _mix_kernel(pa_ref, pb_ref, x_ref, w_ref, g1_ref, sh2_ref, sc2_ref, lg_ref, lb_ref,
                x1_ref, h2_ref):
    mixed = jnp.dot((pa_ref[...] + pb_ref[...]).astype(BF16), w_ref[...], preferred_element_type=F32)
    x1 = _normalize(ALPHA * x_ref[...] + g1_ref[...] * mixed) * lg_ref[...] + lb_ref[...]
    x1_ref[...] = x1
    h2_ref[...] = (_normalize(x1) * (1.0 + sc2_ref[...]) + sh2_ref[...]).astype(BF16)


def _mix(pa, pb, x, w_out16, gate1, shift2, scale2, ln_g, ln_b, tm):
    bsz, t, d = x.shape
    tok = pl.BlockSpec((None, tm, d), lambda b, i: (b, i, 0))
    vec = pl.BlockSpec((1, d), lambda b, i: (0, 0))
    return pl.pallas_call(
        _mix_kernel,
        grid=(bsz, t // tm),
        in_specs=[tok, tok, tok, pl.BlockSpec((d, d), lambda b, i: (0, 0)),
                  _mod_spec(gate1, tm, d), _mod_spec(shift2, tm, d), _mod_spec(scale2, tm, d), vec, vec],
        out_specs=[tok, tok],
        out_shape=[jax.ShapeDtypeStruct((bsz, t, d), F32), jax.ShapeDtypeStruct((bsz, t, d), BF16)],
        compiler_params=_params(2),
        name="mix",
    )(pa, pb, x, w_out16, gate1, shift2, scale2, ln_g.reshape(1, d), ln_b.reshape(1, d))


def _take_top(s, n_iter, rows_f, on_pick):
    def body(it, s_rem):
        m = jnp.max(s_rem, axis=0, keepdims=True)
        first = jnp.min(jnp.where(s_rem == m, rows_f, float(s.shape[0])), axis=0, keepdims=True)
        pick = rows_f == first
        on_pick(it, m, pick)
        return jnp.where(pick, -jnp.inf, s_rem)

    lax.fori_loop(0, n_iter, body, s)


def _fill_candidates(vals_s, cand_s):
    cand_s[...] = jnp.full(cand_s.shape, -jnp.inf, F32)
    for r1 in range(PEER_TOPK):
        off, cnt = _CAND_OFFSETS[r1], _CAND_COUNTS[r1]
        cand_s[off:off + cnt, :] = vals_s[0, r1:r1 + 1, :] + vals_s[1, 0:cnt, :]


def _route_exact(h, score_s, rank_s, vals_s, cand_s, sel_s, lcol_ref, e1_ref, r2_ref, e2_ref):
    tm = score_s.shape[-1]
    rows_f = lax.broadcasted_iota(jnp.int32, (N_KEYS, tm), 0).astype(F32)
    crow_f = lax.broadcasted_iota(jnp.int32, (_N_CAND_PAD, tm), 0).astype(F32)
    for p in range(2):
        rank_s[p] = jnp.full((N_KEYS, tm), float(PEER_TOPK), F32)

        def on_pick(it, m, pick, p=p):
            vals_s[p, pl.ds(it, 1), :] = m
            rank_s[p] = jnp.where(pick, jnp.asarray(it, F32), rank_s[p])

        _take_top(score_s[p], PEER_TOPK, rows_f, on_pick)

    _fill_candidates(vals_s, cand_s)
    sel_s[...] = jnp.zeros((_N_CAND_PAD, tm), F32)

    def on_pick_cand(it, m, pick):
        sel_s[...] = jnp.where(pick, 1.0, sel_s[...])

    _take_top(cand_s[...], PEER_TOPK, crow_f, on_pick_cand)

    sel = sel_s[...]
    top = vals_s[0, 0:1, :] + vals_s[1, 0:1, :]
    z = jnp.sum(jnp.where(sel > 0.0, jnp.exp(cand_s[...] - top), 0.0), axis=0, keepdims=True)
    rank1 = rank_s[0]
    lcol = jnp.zeros((N_KEYS, tm), F32)
    for r1 in range(PEER_TOPK):
        off, cnt = _CAND_OFFSETS[r1], _CAND_COUNTS[r1]
        n_sel = jnp.sum(sel[off:off + cnt, :], axis=0, keepdims=True)
        lcol = lcol + jnp.where(rank1 == float(r1), n_sel, 0.0)
    lcol_ref[h] = lcol
    e1_ref[h] = jnp.exp(score_s[0] - vals_s[0, 0:1, :]) / z
    r2_ref[h] = rank_s[1].astype(BF16)
    e2_ref[h] = jnp.exp(score_s[1] - vals_s[1, 0:1, :]).astype(BF16)


def _compare_exchange(xs, i, l, descending):
    hi, lo = jnp.maximum(xs[i], xs[l]), jnp.minimum(xs[i], xs[l])
    xs[i], xs[l] = (hi, lo) if descending else (lo, hi)


def _bitonic_merge(xs):
    xs = list(xs)
    j = len(xs) // 2
    while j >= 1:
        for i in range(len(xs)):
            if i & j == 0:
                _compare_exchange(xs, i, i | j, True)
        j //= 2
    return xs


def _bitonic_sort(xs):
    xs = list(xs)
    k = 2
    while k <= len(xs):
        j = k // 2
        while j >= 1:
            for i in range(len(xs)):
                if i & j == 0:
                    _compare_exchange(xs, i, i | j, i & k == 0)
            j //= 2
        k *= 2
    return xs


def _sorted_top(pieces):
    xs = _bitonic_sort(pieces)
    for shift in (1, 2, 4):
        other = [pltpu.roll(x, shift, 0) for x in xs][::-1]
        if len(xs) < PEER_TOPK:
            xs = _bitonic_merge(xs + other)
        else:
            xs = _bitonic_merge([jnp.maximum(a, b) for a, b in zip(xs, other)])
    return xs


def _sublane_sum(x):
    for shift in (4, 2, 1):
        x = x + pltpu.roll(x, shift, 0)
    return x


def _route_fast(h, score_s, vals_s, cand_s, lcol_ref, e1_ref, r2_ref, e2_ref):
    tm = score_s.shape[-1]
    n_slab = N_KEYS // SUBLANES
    slabs = [[score_s[p, v * SUBLANES:(v + 1) * SUBLANES, :] for v in range(n_slab)] for p in range(2)]
    tops = [_sorted_top(slabs[p]) for p in range(2)]
    for p in range(2):
        for j in range(PEER_TOPK):
            vals_s[p, j:j + 1, :] = tops[p][j][0:1, :]
    _fill_candidates(vals_s, cand_s)
    cands = [cand_s[k * SUBLANES:(k + 1) * SUBLANES, :] for k in range(_N_CAND_PAD // SUBLANES)]
    pad = [jnp.full((SUBLANES, tm), -jnp.inf, F32)] * (SUBLANES - len(cands))
    tau = _sorted_top(cands + pad)[PEER_TOPK - 1]
    top = tops[0][0] + tops[1][0]
    n_picked = jnp.zeros((SUBLANES, tm), F32)
    z = jnp.zeros((SUBLANES, tm), F32)
    for c in cands:
        n_picked = n_picked + jnp.where(c >= tau, 1.0, 0.0)
        z = z + jnp.where(c >= tau, jnp.exp(c - top), 0.0)
    inv_z = 1.0 / _sublane_sum(z)
    tie = jnp.where(_sublane_sum(n_picked) != float(PEER_TOPK), 1.0, 0.0)
    for p in range(2):
        n_ge = jnp.zeros((SUBLANES, tm), F32)
        for s in slabs[p]:
            n_ge = n_ge + jnp.where(s >= tops[p][PEER_TOPK - 1], 1.0, 0.0)
        tie = jnp.maximum(tie, jnp.where(_sublane_sum(n_ge) != float(PEER_TOPK), 1.0, 0.0))
        for j in range(PEER_TOPK - 1):
            tie = jnp.maximum(tie, jnp.where(tops[p][j] == tops[p][j + 1], 1.0, 0.0))
    n_pick = []
    for r1 in range(PEER_TOPK):
        n = jnp.zeros((SUBLANES, tm), F32)
        for r2 in range(_CAND_COUNTS[r1]):
            n = n + jnp.where(tops[0][r1] + tops[1][r2] >= tau, 1.0, 0.0)
        n_pick.append(n)
    rank2, e2 = [], []
    for v in range(n_slab):
        rows = slice(v * SUBLANES, (v + 1) * SUBLANES)
        lc = jnp.zeros((SUBLANES, tm), F32)
        rk = jnp.full((SUBLANES, tm), float(PEER_TOPK), F32)
        for j in range(PEER_TOPK):
            lc = jnp.where(slabs[0][v] == tops[0][j], n_pick[j], lc)
            rk = jnp.where(slabs[1][v] == tops[1][j], float(j), rk)
        lcol_ref[h, rows, :] = lc
        e1_ref[h, rows, :] = jnp.exp(slabs[0][v] - tops[0][0]) * inv_z
        rank2.append(rk)
        e2.append(jnp.exp(slabs[1][v] - tops[1][0]))
    for u in range(n_slab // 2):
        rows = slice(2 * u * SUBLANES, (2 * u + 2) * SUBLANES)
        r2_ref[h, rows, :] = jnp.concatenate(rank2[2 * u:2 * u + 2], axis=0).astype(BF16)
        e2_ref[h, rows, :] = jnp.concatenate(e2[2 * u:2 * u + 2], axis=0).astype(BF16)
    return jnp.max(tie)


def _route_kernel(h2_ref, wpq_ref, keys_ref, lcol_ref, e1_ref, r2_ref, e2_ref,
                  q_s, score_s, rank_s, vals_s, cand_s, sel_s):
    q = jnp.dot(h2_ref[...], wpq_ref[...], preferred_element_type=F32)
    for hp in range(2 * PEER_HEADS):
        q_s[hp] = q[:, hp * N_KEYS:(hp + 1) * N_KEYS].astype(BF16)
    nt = (((1,), (1,)), ((), ()))

    def per_head(h, carry):
        for p in range(2):
            score_s[p] = lax.dot_general(keys_ref[h * 2 + p], q_s[h * 2 + p], nt,
                                         preferred_element_type=F32)
        tie = _route_fast(h, score_s, vals_s, cand_s, lcol_ref, e1_ref, r2_ref, e2_ref)

        @pl.when(tie > 0.0)
        def _():
            _route_exact(h, score_s, rank_s, vals_s, cand_s, sel_s, lcol_ref, e1_ref, r2_ref, e2_ref)

        return carry

    lax.fori_loop(0, PEER_HEADS, per_head, 0)


def _route(h2, w_pq16, keys16, tm):
    bsz, t, d = h2.shape
    nq = w_pq16.shape[1]
    out = pl.BlockSpec((None, PEER_HEADS, N_KEYS, tm), lambda b, i: (b, 0, 0, i))
    shape = jax.ShapeDtypeStruct((bsz, PEER_HEADS, N_KEYS, t), F32)
    shape16 = jax.ShapeDtypeStruct((bsz, PEER_HEADS, N_KEYS, t), BF16)
    return pl.pallas_call(
        _route_kernel,
        grid=(bsz, t // tm),
        in_specs=[pl.BlockSpec((None, tm, d), lambda b, i: (b, i, 0)),
                  pl.BlockSpec((d, nq), lambda b, i: (0, 0)),
                  pl.BlockSpec((2 * PEER_HEADS, N_KEYS, nq // (2 * PEER_HEADS)), lambda b, i: (0, 0, 0))],
        out_specs=[out, out, out, out],
        out_shape=[shape, shape, shape16, shape16],
        scratch_shapes=[pltpu.VMEM((2 * PEER_HEADS, tm, nq // (2 * PEER_HEADS)), BF16),
                        pltpu.VMEM((2, N_KEYS, tm), F32),
                        pltpu.VMEM((2, N_KEYS, tm), F32),
                        pltpu.VMEM((2, PEER_TOPK, tm), F32),
                        pltpu.VMEM((_N_CAND_PAD, tm), F32),
                        pltpu.VMEM((_N_CAND_PAD, tm), F32)],
        compiler_params=_params(2),
        name="route",
    )(h2, w_pq16, keys16)


_ACT_ROWS = 256


def _peer_kernel(h2_ref, u_ref, vt_ref, lcol_ref, e1_ref, r2_ref, e2_ref, x1_ref, g2_ref,
                 lg_ref, lb_ref, y_ref, acc, p_s, *, te):
    j = pl.program_id(2)

    @pl.when(j == 0)
    def _():
        acc[...] = jnp.zeros(acc.shape, F32)

    zero = jnp.zeros((), BF16)
    n_sub = _ACT_ROWS // N_KEYS
    for c in range(te // _ACT_ROWS):
        chunk = slice(c * _ACT_ROWS, (c + 1) * _ACT_ROWS)
        act = lax.dot_general(u_ref[chunk, :], h2_ref[...], (((1,), (1,)), ((), ())),
                              preferred_element_type=F32)
        for sub in range(n_sub):
            al = c * n_sub + sub
            w = None
            for h in range(PEER_HEADS):
                keep = r2_ref[h] < lcol_ref[h, al:al + 1, :].astype(BF16)
                term = jnp.where(keep, e2_ref[h] * e1_ref[h, al:al + 1, :].astype(BF16), zero)
                w = term if w is None else w + term
            g = _gelu(act[sub * N_KEYS:(sub + 1) * N_KEYS, :])
            p_s[al * N_KEYS:(al + 1) * N_KEYS, :] = (w.astype(F32) * g).astype(BF16)
    acc[...] += jnp.dot(vt_ref[...], p_s[...], preferred_element_type=F32)

    @pl.when(j == pl.num_programs(2) - 1)
    def _():
        ffn = acc[...].T
        y = _normalize(ALPHA * x1_ref[...] + g2_ref[...] * ffn)
        y_ref[...] = y * lg_ref[...] + lb_ref[...]


def _peer(h2, u16, vt16, lcol, e1, r2, e2, x1, gate2, ln_g, ln_b, tm, te):
    bsz, t, d = h2.shape
    n_exp = u16.shape[0]
    na = te // N_KEYS
    tok = pl.BlockSpec((None, tm, d), lambda b, i, j: (b, i, 0))
    tok1 = pl.BlockSpec((None, tm, d), lambda b, i, j: (b, i, 0), pipeline_mode=pl.Buffered(1))
    vec = pl.BlockSpec((1, d), lambda b, i, j: (0, 0))
    by_a = pl.BlockSpec((None, PEER_HEADS, na, tm), lambda b, i, j: (b, 0, j, i))
    by_b = pl.BlockSpec((None, PEER_HEADS, N_KEYS, tm), lambda b, i, j: (b, 0, 0, i))
    return pl.pallas_call(
        functools.partial(_peer_kernel, te=te),
        grid=(bsz, t // tm, n_exp // te),
        in_specs=[tok,
                  pl.BlockSpec((te, d), lambda b, i, j: (j, 0)),
                  pl.BlockSpec((d, te), lambda b, i, j: (0, j)),
                  by_a, by_a, by_b, by_b, tok1, _mod_spec(gate2, tm, d), vec, vec],
        out_specs=tok1,
        out_shape=jax.ShapeDtypeStruct((bsz, t, d), F32),
        scratch_shapes=[pltpu.VMEM((d, tm), F32), pltpu.VMEM((te, tm), BF16)],
        compiler_params=_params(3),
        name="peer",
    )(h2, u16, vt16, lcol, e1, r2, e2, x1, gate2, ln_g.reshape(1, d), ln_b.reshape(1, d))


def _layer(x, mods, conv_buf, h0, attend, w, *, tm, tm_mix, tt, tc, tm_route, tm_peer, te, flatten):
    bsz, t, d = x.shape
    if flatten:
        mods = [jnp.broadcast_to(m, (bsz, t, d)).reshape(1, bsz * t, d) for m in mods]
        flat = lambda a: a.reshape(1, bsz * t, a.shape[-1])
    else:
        flat = lambda a: a
    shift1, scale1, gate1, shift2, scale2, gate2 = mods
    proj = _proj(flat(x), shift1, scale1, w["w_in"], tm).reshape(bsz, t, -1)
    pa, conv_state, rnn_state = _rnn(proj, conv_buf, h0, w["conv_w"], w["conv_b"], w["w_ra"], w["b_ra"],
                                     w["w_ri"], w["b_ri"], w["lam"], d, tt, tc)
    pb = attend(proj)
    x1, h2 = _mix(flat(pa), flat(pb), flat(x), w["w_out"], gate1, shift2, scale2, w["ln1_g"], w["ln1_b"], tm_mix)
    lcol, e1, r2, e2 = _route(h2, w["w_pq"], w["keys"], tm_route)
    y = _peer(h2, w["peer_u"], w["peer_vt"], lcol, e1, r2, e2, x1, gate2, w["ln2_g"], w["ln2_b"], tm_peer, te)
    n_heads = d // HEAD_DIM
    k = proj[:, :, 3 * d:4 * d].reshape(bsz, t, n_heads, HEAD_DIM)
    v = proj[:, :, 4 * d:5 * d].reshape(bsz, t, n_heads, HEAD_DIM)
    return y.reshape(bsz, t, d), conv_state, rnn_state, k, v


def kernel(x_prompt, x_sample, c_prompt, c_sample, state_conv, state_rnn, cache_k, cache_v, w_ada, b_ada, w_in, conv_w, conv_b, w_ra, b_ra, w_ri, b_ri, rg_lambda, rel_bias, w_out, ln1_g, ln1_b, w_pq, peer_keys, peer_u, peer_v, ln2_g, ln2_b):
    bsz_p, t_p, d = x_prompt.shape
    bsz_s, t_s, _ = x_sample.shape
    depth = w_ada.shape[0]
    keep = min(ATT_REACH, t_p)
    y_p, y_s = x_prompt, x_sample
    outs = [[] for _ in range(8)]
    rows = bsz_p + bsz_s
    rows_pad = -(-rows // SUBLANES) * SUBLANES
    c_all = jnp.concatenate([c_prompt, c_sample, jnp.zeros((rows_pad - rows, d), F32)], axis=0)
    tm = min(1024, t_p)
    for l in range(depth):
        w = dict(w_in=w_in[l].astype(BF16), conv_w=conv_w[l], conv_b=conv_b[l],
                 w_ra=w_ra[l].astype(BF16), b_ra=b_ra[l], w_ri=w_ri[l].astype(BF16), b_ri=b_ri[l],
                 lam=rg_lambda[l], w_out=w_out[l].astype(BF16), ln1_g=ln1_g[l], ln1_b=ln1_b[l],
                 w_pq=w_pq[l].astype(BF16),
                 keys=peer_keys[l].astype(BF16).reshape(2 * PEER_HEADS, N_KEYS, -1),
                 peer_u=peer_u[l].astype(BF16), peer_vt=peer_v[l].astype(BF16).T,
                 ln2_g=ln2_g[l], ln2_b=ln2_b[l])
        mod = _ada(c_all, w_ada[l], b_ada[l])
        mods_p = [m[:, None, :] for m in jnp.split(mod[:bsz_p], 6, axis=-1)]
        mods_s = [m[:, None, :] for m in jnp.split(mod[bsz_p:rows], 6, axis=-1)]

        y_p, cs, hs, k, v = _layer(
            y_p, mods_p, jnp.zeros((bsz_p, CONV_W - 1, d), F32), jnp.zeros((bsz_p, d), F32),
            lambda proj: _attn_prompt(proj, rel_bias[l], d), w,
            tm=tm, tm_mix=min(256, t_p), tt=min(256, t_p), tc=512, tm_route=256, tm_peer=512, te=1024, flatten=False)
        for dst, val in zip(outs[:4], (cs, hs, k[:, -keep:], v[:, -keep:])):
            dst.append(val)

        y_s, cs, hs, k, v = _layer(
            y_s, mods_s, state_conv[l], state_rnn[l],
            lambda proj: _attn_sample(proj, cache_k[l], cache_v[l], rel_bias[l], d), w,
            tm=bsz_s * t_s, tm_mix=bsz_s * t_s, tt=t_s, tc=512, tm_route=bsz_s * t_s, tm_peer=bsz_s * t_s, te=1024, flatten=True)
        for dst, val in zip(outs[4:], (cs, hs, k, v)):
            dst.append(val)
    return (y_p, y_s) + tuple(jnp.stack(o) for o in outs)
```

```python
import functools
import math

import jax
import jax.numpy as jnp
from jax import lax
from jax.experimental import pallas as pl
from jax.experimental.pallas import tpu as pltpu

F32 = jnp.float32
BF16 = jnp.bfloat16

CHUNK = 64
LEFT_CHUNKS = 8
ATT_REACH = LEFT_CHUNKS * CHUNK
MAX_REL = 128
REL_BUCKETS = MAX_REL + CHUNK
PAST_LEN = 1024
CONV_W = 4
RG_C = 8.0
RNN_BLOCK_DIM = 128
HEAD_DIM = 128
N_KEYS = 128
PEER_HEADS = 8
PEER_TOPK = 16
DEPTH = 1
ALPHA = (2.0 * DEPTH) ** 0.25
LN_EPS = 1e-5
NEG_INF = -1e30

LANES = 128
SUBLANES = 8
VMEM_LIMIT = 56 * 1024 * 1024

_CAND_COUNTS = tuple(PEER_TOPK // (r1 + 1) for r1 in range(PEER_TOPK))
_CAND_OFFSETS = tuple(sum(_CAND_COUNTS[:r1]) for r1 in range(PEER_TOPK))
_N_CAND = sum(_CAND_COUNTS)
_N_CAND_PAD = -(-_N_CAND // SUBLANES) * SUBLANES


def _params(n_grid):
    return pltpu.CompilerParams(dimension_semantics=("arbitrary",) * n_grid,
                                vmem_limit_bytes=VMEM_LIMIT)


def _normalize(x):
    mu = jnp.mean(x, axis=-1, keepdims=True)
    xc = x - mu
    var = jnp.mean(xc * xc, axis=-1, keepdims=True)
    return xc * lax.rsqrt(var + LN_EPS)


def _gelu(x):
    return 0.5 * x * (1.0 + lax.erf(x * (2.0 ** -0.5)))


def _ada_kernel(c_ref, w_ref, b_ref, o_ref):
    c = c_ref[...]
    s = c * jax.nn.sigmoid(c)
    o_ref[...] = jnp.dot(s, w_ref[...], preferred_element_type=F32) + b_ref[...]


def _ada(c, w_ada, b_ada):
    rows, d = c.shape
    n = w_ada.shape[1]
    tn = 1024
    return pl.pallas_call(
        _ada_kernel,
        grid=(n // tn,),
        in_specs=[pl.BlockSpec((rows, d), lambda j: (0, 0)),
                  pl.BlockSpec((d, tn), lambda j: (0, j)),
                  pl.BlockSpec((1, tn), lambda j: (0, j))],
        out_specs=pl.BlockSpec((rows, tn), lambda j: (0, j)),
        out_shape=jax.ShapeDtypeStruct((rows, n), F32),
        compiler_params=_params(1),
        name="ada",
    )(c, w_ada, b_ada.reshape(1, n))


def _proj_kernel(x_ref, sh_ref, sc_ref, w_ref, o_ref, h_ref):
    @pl.when(pl.program_id(2) == 0)
    def _():
        h = _normalize(x_ref[...]) * (1.0 + sc_ref[...]) + sh_ref[...]
        h_ref[...] = h.astype(BF16)

    o_ref[...] = jnp.dot(h_ref[...], w_ref[...], preferred_element_type=F32)


def _mod_spec(mod, tm, d):
    if mod.shape[1] == 1:
        return pl.BlockSpec((None, 1, d), lambda b, i, *_: (b, 0, 0))
    return pl.BlockSpec((None, tm, d), lambda b, i, *_: (b, i, 0))


def _proj(x, shift, scale, w_in_bf16, tm):
    bsz, t, d = x.shape
    n = w_in_bf16.shape[1]
    tn = 1024
    return pl.pallas_call(
        _proj_kernel,
        grid=(bsz, t // tm, n // tn),
        in_specs=[pl.BlockSpec((None, tm, d), lambda b, i, j: (b, i, 0)),
                  _mod_spec(shift, tm, d), _mod_spec(scale, tm, d),
                  pl.BlockSpec((d, tn), lambda b, i, j: (0, j))],
        out_specs=pl.BlockSpec((None, tm, tn), lambda b, i, j: (b, i, j)),
        out_shape=jax.ShapeDtypeStruct((bsz, t, n), F32),
        scratch_shapes=[pltpu.VMEM((tm, d), BF16)],
        compiler_params=_params(3),
        name="proj",
    )(x, shift, scale, w_in_bf16)


def _rnn_kernel(xr_ref, yr_ref, ga_ref, cbuf_ref, h0_ref, cw_ref, cb_ref, wra_ref, bra_ref,
                wri_ref, bri_ref, lam_ref, pa_ref, cst_ref, hst_ref,
                xbuf, a_s, u_s, hcar, *, tt, tc):
    t = pl.program_id(2)
    nt = pl.num_programs(2)
    pad = SUBLANES
    keep = CONV_W - 1

    @pl.when(t == 0)
    def _():
        xbuf[pad - keep:pad, :] = cbuf_ref[...]
        hcar[...] = h0_ref[...]

    xbuf[pad:pad + tt, :] = xr_ref[...]
    cw = cw_ref[...]
    xc = cb_ref[...] + cw[0:1, :] * xbuf[pad - keep:pad - keep + tt, :]
    for j in range(1, CONV_W):
        xc = xc + cw[j:j + 1, :] * xbuf[pad - keep + j:pad - keep + j + tt, :]
    tail = xbuf[pad + tt - keep:pad + tt, :]
    xbuf[pad - keep:pad, :] = tail

    lam = lam_ref[...]
    softplus_neg = jnp.maximum(-lam, 0.0) + jnp.log1p(jnp.exp(-jnp.abs(lam)))
    for blk in range(tc // RNN_BLOCK_DIM):
        cols = slice(blk * RNN_BLOCK_DIM, (blk + 1) * RNN_BLOCK_DIM)
        xb = xc[:, cols]
        xb16 = xb.astype(BF16)
        r = jax.nn.sigmoid(jnp.dot(xb16, wra_ref[blk], preferred_element_type=F32) + bra_ref[:, cols])
        i = jax.nn.sigmoid(jnp.dot(xb16, wri_ref[blk], preferred_element_type=F32) + bri_ref[:, cols])
        log_a = -RG_C * r * softplus_neg[:, cols]
        a = jnp.exp(log_a)
        u = jnp.sqrt(jnp.tanh(-log_a) * (1.0 + a * a)) * (i * xb)
        a_s[:, cols] = a
        u_s[:, cols] = u

    row = lax.broadcasted_iota(jnp.int32, (SUBLANES, tc), 0)

    def scan_group(g, h_prev):
        rows = pl.ds(pl.multiple_of(g * SUBLANES, SUBLANES), SUBLANES)
        a = a_s[rows, :]
        u = u_s[rows, :]
        for s in (1, 2, 4):
            a_sh = pltpu.roll(a, s, 0)
            u_sh = pltpu.roll(u, s, 0)
            live = row >= s
            u = jnp.where(live, a * u_sh + u, u)
            a = jnp.where(live, a * a_sh, a)
        h = a * h_prev + u
        u_s[rows, :] = h
        return h[SUBLANES - 1:SUBLANES, :]

    h_last = lax.fori_loop(0, tt // SUBLANES, scan_group, hcar[...])
    hcar[...] = h_last

    pa_ref[...] = u_s[...] * _gelu(yr_ref[...]) * jax.nn.sigmoid(ga_ref[...])

    @pl.when(t == nt - 1)
    def _():
        cst_ref[...] = tail
        hst_ref[...] = h_last


def _rnn(proj, conv_buf, h0, conv_w, conv_b, w_ra16, b_ra, w_ri16, b_ri, lam, d, tt, tc):
    bsz, t, _ = proj.shape
    nc = d // tc
    nb = tc // RNN_BLOCK_DIM
    keep = CONV_W - 1
    col = lambda off: (lambda b, c, i: (b, i, off * nc + c))
    vec = pl.BlockSpec((1, tc), lambda b, c, i: (0, c))
    blk = pl.BlockSpec((nb, RNN_BLOCK_DIM, RNN_BLOCK_DIM), lambda b, c, i: (c, 0, 0))
    pa, cst, hst = pl.pallas_call(
        functools.partial(_rnn_kernel, tt=tt, tc=tc),
        grid=(bsz, nc, t // tt),
        in_specs=[pl.BlockSpec((None, tt, tc), col(0)),
                  pl.BlockSpec((None, tt, tc), col(1)),
                  pl.BlockSpec((None, tt, tc), col(5)),
                  pl.BlockSpec((None, keep, tc), lambda b, c, i: (b, 0, c)),
                  pl.BlockSpec((None, 1, tc), lambda b, c, i: (b, 0, c)),
                  pl.BlockSpec((CONV_W, tc), lambda b, c, i: (0, c)),
                  vec, blk, vec, blk, vec, vec],
        out_specs=[pl.BlockSpec((None, tt, tc), lambda b, c, i: (b, i, c)),
                   pl.BlockSpec((None, keep, tc), lambda b, c, i: (b, 0, c)),
                   pl.BlockSpec((None, 1, tc), lambda b, c, i: (b, 0, c))],
        out_shape=[jax.ShapeDtypeStruct((bsz, t, d), F32),
                   jax.ShapeDtypeStruct((bsz, keep, d), F32),
                   jax.ShapeDtypeStruct((bsz, 1, d), F32)],
        scratch_shapes=[pltpu.VMEM((tt + SUBLANES, tc), F32),
                        pltpu.VMEM((tt, tc), F32),
                        pltpu.VMEM((tt, tc), F32),
                        pltpu.VMEM((1, tc), F32)],
        compiler_params=_params(3),
        name="rnn",
    )(proj, proj, proj, conv_buf, h0.reshape(bsz, 1, d), conv_w, conv_b.reshape(1, d),
      w_ra16, b_ra.reshape(1, d), w_ri16, b_ri.reshape(1, d), lam.reshape(1, d))
    return pa, cst, hst.reshape(bsz, d)


def _rel_bias_tile(rel_ref, head, dist, visible):
    idx = jnp.clip(dist, -(CHUNK - 1), MAX_REL) + (CHUNK - 1)

    def body(j, acc):
        return acc + jnp.where(idx == j, rel_ref[head, j], 0.0)

    bias = lax.fori_loop(0, REL_BUCKETS, body, jnp.zeros(dist.shape, F32))
    return jnp.where(visible, bias, NEG_INF)


def _softmax_pv(s, v16):
    m = jnp.max(s, axis=-1, keepdims=True)
    p = jnp.exp(s - m)
    l = jnp.sum(p, axis=-1, keepdims=True)
    return jnp.dot(p.astype(BF16), v16, preferred_element_type=F32) / l


_QSUB = 2 * CHUNK
_BAND = _QSUB + ATT_REACH


_ATTN_HEADS = 4


def _attn_prompt_kernel(rel_ref, q_ref, kp_ref, kc_ref, vp_ref, vc_ref, gb_ref, o_ref,
                        bias_s, kcat, vcat):
    group, b, i = pl.program_id(0), pl.program_id(1), pl.program_id(2)
    tq = ATT_REACH

    @pl.when((b == 0) & (i == 0))
    def _():
        r = lax.broadcasted_iota(jnp.int32, (_QSUB, _BAND), 0)
        c = lax.broadcasted_iota(jnp.int32, (_QSUB, _BAND), 1)
        qc = r // CHUNK + LEFT_CHUNKS
        kc = c // CHUNK
        visible = (kc <= qc) & (kc >= qc - LEFT_CHUNKS)
        for hh in range(_ATTN_HEADS):
            bias_s[hh] = _rel_bias_tile(rel_ref, group * _ATTN_HEADS + hh, r + ATT_REACH - c, visible)

    kcat[0:tq, :] = kp_ref[...].astype(BF16)
    kcat[tq:2 * tq, :] = kc_ref[...].astype(BF16)
    vcat[0:tq, :] = vp_ref[...].astype(BF16)
    vcat[tq:2 * tq, :] = vc_ref[...].astype(BF16)
    col = lax.broadcasted_iota(jnp.int32, (_QSUB, _BAND), 1)
    n_sub = tq // _QSUB
    for hh in range(_ATTN_HEADS):
        cols = slice(hh * HEAD_DIM, (hh + 1) * HEAD_DIM)
        scores = []
        for s in range(n_sub):
            q16 = q_ref[s * _QSUB:(s + 1) * _QSUB, cols].astype(BF16)
            sc = lax.dot_general(q16, kcat[s * _QSUB:s * _QSUB + _BAND, cols], (((1,), (1,)), ((), ())),
                                 preferred_element_type=F32) * (HEAD_DIM ** -0.5)
            sc = sc + bias_s[hh]
            k_pos = col + (i * tq + s * _QSUB - ATT_REACH)
            scores.append(jnp.where(k_pos >= 0, sc, NEG_INF))
        maxes = [jnp.max(sc, axis=-1, keepdims=True) for sc in scores]
        probs = [jnp.exp(sc - m) for sc, m in zip(scores, maxes)]
        sums = [jnp.sum(p, axis=-1, keepdims=True) for p in probs]
        outs = [jnp.dot(p.astype(BF16), vcat[s * _QSUB:s * _QSUB + _BAND, cols], preferred_element_type=F32)
                for s, p in enumerate(probs)]
        for s in range(n_sub):
            rows = slice(s * _QSUB, (s + 1) * _QSUB)
            o_ref[rows, cols] = outs[s] / sums[s] * jax.nn.sigmoid(gb_ref[rows, cols])


def _attn_prompt(proj, rel_bias, d):
    bsz, t, _ = proj.shape
    ng = d // (HEAD_DIM * _ATTN_HEADS)
    width = HEAD_DIM * _ATTN_HEADS
    tq = ATT_REACH
    cur = lambda off: (lambda g, b, i: (b, i, off * ng + g))
    prev = lambda off: (lambda g, b, i: (b, jnp.maximum(i - 1, 0), off * ng + g))
    blk = lambda f: pl.BlockSpec((None, tq, width), f)
    return pl.pallas_call(
        _attn_prompt_kernel,
        grid=(ng, bsz, t // tq),
        in_specs=[pl.BlockSpec(memory_space=pltpu.SMEM),
                  blk(cur(2)), blk(prev(3)), blk(cur(3)), blk(prev(4)), blk(cur(4)), blk(cur(6))],
        out_specs=pl.BlockSpec((None, tq, width), lambda g, b, i: (b, i, g)),
        out_shape=jax.ShapeDtypeStruct((bsz, t, d), F32),
        scratch_shapes=[pltpu.VMEM((_ATTN_HEADS, _QSUB, _BAND), F32),
                        pltpu.VMEM((2 * tq, width), BF16),
                        pltpu.VMEM((2 * tq, width), BF16)],
        compiler_params=_params(3),
        name="attn_prompt",
    )(rel_bias, proj, proj, proj, proj, proj, proj)


def _attn_sample_kernel(rel_ref, q_ref, kn_ref, vn_ref, gb_ref, ck_ref, cv_ref, o_ref,
                        bias_past_s, bias_new_s, *, t, n_past):
    head = pl.program_id(0)

    def bias(n_keys, first_pos):
        r = lax.broadcasted_iota(jnp.int32, (t, n_keys), 0)
        c = lax.broadcasted_iota(jnp.int32, (t, n_keys), 1)
        q_pos = PAST_LEN + r
        k_pos = first_pos + c
        qc = q_pos // CHUNK
        kc = k_pos // CHUNK
        visible = (k_pos >= 0) & (kc <= qc) & (kc >= qc - LEFT_CHUNKS)
        return _rel_bias_tile(rel_ref, head, q_pos - k_pos, visible)

    @pl.when(pl.program_id(1) == 0)
    def _():
        bias_past_s[...] = bias(n_past, PAST_LEN - n_past)
        bias_new_s[...] = bias(t, PAST_LEN)

    scale = HEAD_DIM ** -0.5
    q16 = q_ref[...].astype(BF16)
    nt = (((1,), (1,)), ((), ()))
    s_past = lax.dot_general(q16, ck_ref[...].astype(BF16), nt, preferred_element_type=F32) * scale
    s_new = lax.dot_general(q16, kn_ref[...].astype(BF16), nt, preferred_element_type=F32) * scale
    s_past = s_past + bias_past_s[...]
    s_new = s_new + bias_new_s[...]
    m = jnp.maximum(jnp.max(s_past, axis=-1, keepdims=True), jnp.max(s_new, axis=-1, keepdims=True))
    p_past = jnp.exp(s_past - m)
    p_new = jnp.exp(s_new - m)
    l = jnp.sum(p_past, axis=-1, keepdims=True) + jnp.sum(p_new, axis=-1, keepdims=True)
    o = (jnp.dot(p_past.astype(BF16), cv_ref[...].astype(BF16), preferred_element_type=F32)
         + jnp.dot(p_new.astype(BF16), vn_ref[...].astype(BF16), preferred_element_type=F32)) / l
    o_ref[...] = o * jax.nn.sigmoid(gb_ref[...])


def _attn_sample(proj, cache_k, cache_v, rel_bias, d):
    bsz, t, _ = proj.shape
    nh = d // HEAD_DIM
    n_past = cache_k.shape[1]
    new = lambda off: pl.BlockSpec((None, t, HEAD_DIM), lambda h, b: (b, 0, off * nh + h))
    past = pl.BlockSpec((None, n_past, HEAD_DIM), lambda h, b: (b, 0, h))
    return pl.pallas_call(
        functools.partial(_attn_sample_kernel, t=t, n_past=n_past),
        grid=(nh, bsz),
        in_specs=[pl.BlockSpec(memory_space=pltpu.SMEM), new(2), new(3), new(4), new(6), past, past],
        out_specs=pl.BlockSpec((None, t, HEAD_DIM), lambda h, b: (b, 0, h)),
        out_shape=jax.ShapeDtypeStruct((bsz, t, d), F32),
        scratch_shapes=[pltpu.VMEM((t, n_past), F32), pltpu.VMEM((t, t), F32)],
        compiler_params=_params(2),
        name="attn_sample",
    )(rel_bias, proj, proj, proj, proj, cache_k.reshape(bsz, n_past, d), cache_v.reshape(bsz, n_past, d))


def _mix_kernel(pa_ref, pb_ref, x_ref, w_ref, g1_ref, sh2_ref, sc2_ref, lg_ref, lb_ref,
                x1_ref, h2_ref):
    mixed = jnp.dot((pa_ref[...] + pb_ref[...]).astype(BF16), w_ref[...], preferred_element_type=F32)
    x1 = _normalize(ALPHA * x_ref[...] + g1_ref[...] * mixed) * lg_ref[...] + lb_ref[...]
    x1_ref[...] = x1
    h2_ref[...] = (_normalize(x1) * (1.0 + sc2_ref[...]) + sh2_ref[...]).astype(BF16)


def _mix(pa, pb, x, w_out16, gate1, shift2, scale2, ln_g, ln_b, tm):
    bsz, t, d = x.shape
    tok = pl.BlockSpec((None, tm, d), lambda b, i: (b, i, 0))
    vec = pl.BlockSpec((1, d), lambda b, i: (0, 0))
    return pl.pallas_call(
        _mix_kernel,
        grid=(bsz, t // tm),
        in_specs=[tok, tok, tok, pl.BlockSpec((d, d), lambda b, i: (0, 0)),
                  _mod_spec(gate1, tm, d), _mod_spec(shift2, tm, d), _mod_spec(scale2, tm, d), vec, vec],
        out_specs=[tok, tok],
        out_shape=[jax.ShapeDtypeStruct((bsz, t, d), F32), jax.ShapeDtypeStruct((bsz, t, d), BF16)],
        compiler_params=_params(2),
        name="mix",
    )(pa, pb, x, w_out16, gate1, shift2, scale2, ln_g.reshape(1, d), ln_b.reshape(1, d))


def _take_top(s, n_iter, rows_f, on_pick):
    def body(it, s_rem):
        m = jnp.max(s_rem, axis=0, keepdims=True)
        first = jnp.min(jnp.where(s_rem == m, rows_f, float(s.shape[0])), axis=0, keepdims=True)
        pick = rows_f == first
        on_pick(it, m, pick)
        return jnp.where(pick, -jnp.inf, s_rem)

    lax.fori_loop(0, n_iter, body, s)


def _fill_candidates(vals_s, cand_s):
    cand_s[...] = jnp.full(cand_s.shape, -jnp.inf, F32)
    for r1 in range(PEER_TOPK):
        off, cnt = _CAND_OFFSETS[r1], _CAND_COUNTS[r1]
        cand_s[off:off + cnt, :] = vals_s[0, r1:r1 + 1, :] + vals_s[1, 0:cnt, :]


def _route_exact(h, score_s, rank_s, vals_s, cand_s, sel_s, lcol_ref, e1_ref, r2_ref, e2_ref):
    tm = score_s.shape[-1]
    rows_f = lax.broadcasted_iota(jnp.int32, (N_KEYS, tm), 0).astype(F32)
    crow_f = lax.broadcasted_iota(jnp.int32, (_N_CAND_PAD, tm), 0).astype(F32)
    for p in range(2):
        rank_s[p] = jnp.full((N_KEYS, tm), float(PEER_TOPK), F32)

        def on_pick(it, m, pick, p=p):
            vals_s[p, pl.ds(it, 1), :] = m
            rank_s[p] = jnp.where(pick, jnp.asarray(it, F32), rank_s[p])

        _take_top(score_s[p], PEER_TOPK, rows_f, on_pick)

    _fill_candidates(vals_s, cand_s)
    sel_s[...] = jnp.zeros((_N_CAND_PAD, tm), F32)

    def on_pick_cand(it, m, pick):
        sel_s[...] = jnp.where(pick, 1.0, sel_s[...])

    _take_top(cand_s[...], PEER_TOPK, crow_f, on_pick_cand)

    sel = sel_s[...]
    top = vals_s[0, 0:1, :] + vals_s[1, 0:1, :]
    z = jnp.sum(jnp.where(sel > 0.0, jnp.exp(cand_s[...] - top), 0.0), axis=0, keepdims=True)
    rank1 = rank_s[0]
    lcol = jnp.zeros((N_KEYS, tm), F32)
    for r1 in range(PEER_TOPK):
        off, cnt = _CAND_OFFSETS[r1], _CAND_COUNTS[r1]
        n_sel = jnp.sum(sel[off:off + cnt, :], axis=0, keepdims=True)
        lcol = lcol + jnp.where(rank1 == float(r1), n_sel, 0.0)
    lcol_ref[h] = lcol
    e1_ref[h] = jnp.exp(score_s[0] - vals_s[0, 0:1, :]) / z
    r2_ref[h] = rank_s[1].astype(BF16)
    e2_ref[h] = jnp.exp(score_s[1] - vals_s[1, 0:1, :]).astype(BF16)


def _compare_exchange(xs, i, l, descending):
    hi, lo = jnp.maximum(xs[i], xs[l]), jnp.minimum(xs[i], xs[l])
    xs[i], xs[l] = (hi, lo) if descending else (lo, hi)


def _bitonic_merge(xs):
    xs = list(xs)
    j = len(xs) // 2
    while j >= 1:
        for i in range(len(xs)):
            if i & j == 0:
                _compare_exchange(xs, i, i | j, True)
        j //= 2
    return xs


def _bitonic_sort(xs):
    xs = list(xs)
    k = 2
    while k <= len(xs):
        j = k // 2
        while j >= 1:
            for i in range(len(xs)):
                if i & j == 0:
                    _compare_exchange(xs, i, i | j, i & k == 0)
            j //= 2
        k *= 2
    return xs


def _sorted_top(pieces):
    xs = _bitonic_sort(pieces)
    for shift in (1, 2, 4):
        other = [pltpu.roll(x, shift, 0) for x in xs][::-1]
        if len(xs) < PEER_TOPK:
            xs = _bitonic_merge(xs + other)
        else:
            xs = _bitonic_merge([jnp.maximum(a, b) for a, b in zip(xs, other)])
    return xs


def _sublane_sum(x):
    for shift in (4, 2, 1):
        x = x + pltpu.roll(x, shift, 0)
    return x


def _route_fast(h, score_s, vals_s, cand_s, lcol_ref, e1_ref, r2_ref, e2_ref):
    tm = score_s.shape[-1]
    n_slab = N_KEYS // SUBLANES
    slabs = [[score_s[p, v * SUBLANES:(v + 1) * SUBLANES, :] for v in range(n_slab)] for p in range(2)]
    tops = [_sorted_top(slabs[p]) for p in range(2)]
    for p in range(2):
        for j in range(PEER_TOPK):
            vals_s[p, j:j + 1, :] = tops[p][j][0:1, :]
    _fill_candidates(vals_s, cand_s)
    cands = [cand_s[k * SUBLANES:(k + 1) * SUBLANES, :] for k in range(_N_CAND_PAD // SUBLANES)]
    pad = [jnp.full((SUBLANES, tm), -jnp.inf, F32)] * (SUBLANES - len(cands))
    tau = _sorted_top(cands + pad)[PEER_TOPK - 1]
    top = tops[0][0] + tops[1][0]
    n_picked = jnp.zeros((SUBLANES, tm), F32)
    z = jnp.zeros((SUBLANES, tm), F32)
    for c in cands:
        n_picked = n_picked + jnp.where(c >= tau, 1.0, 0.0)
        z = z + jnp.where(c >= tau, jnp.exp(c - top), 0.0)
    inv_z = 1.0 / _sublane_sum(z)
    tie = jnp.where(_sublane_sum(n_picked) != float(PEER_TOPK), 1.0, 0.0)
    for p in range(2):
        n_ge = jnp.zeros((SUBLANES, tm), F32)
        for s in slabs[p]:
            n_ge = n_ge + jnp.where(s >= tops[p][PEER_TOPK - 1], 1.0, 0.0)
        tie = jnp.maximum(tie, jnp.where(_sublane_sum(n_ge) != float(PEER_TOPK), 1.0, 0.0))
        for j in range(PEER_TOPK - 1):
            tie = jnp.maximum(tie, jnp.where(tops[p][j] == tops[p][j + 1], 1.0, 0.0))
    n_pick = []
    for r1 in range(PEER_TOPK):
        n = jnp.zeros((SUBLANES, tm), F32)
        for r2 in range(_CAND_COUNTS[r1]):
            n = n + jnp.where(tops[0][r1] + tops[1][r2] >= tau, 1.0, 0.0)
        n_pick.append(n)
    rank2, e2 = [], []
    for v in range(n_slab):
        rows = slice(v * SUBLANES, (v + 1) * SUBLANES)
        lc = jnp.zeros((SUBLANES, tm), F32)
        rk = jnp.full((SUBLANES, tm), float(PEER_TOPK), F32)
        for j in range(PEER_TOPK):
            lc = jnp.where(slabs[0][v] == tops[0][j], n_pick[j], lc)
            rk = jnp.where(slabs[1][v] == tops[1][j], float(j), rk)
        lcol_ref[h, rows, :] = lc
        e1_ref[h, rows, :] = jnp.exp(slabs[0][v] - tops[0][0]) * inv_z
        rank2.append(rk)
        e2.append(jnp.exp(slabs[1][v] - tops[1][0]))
    for u in range(n_slab // 2):
        rows = slice(2 * u * SUBLANES, (2 * u + 2) * SUBLANES)
        r2_ref[h, rows, :] = jnp.concatenate(rank2[2 * u:2 * u + 2], axis=0).astype(BF16)
        e2_ref[h, rows, :] = jnp.concatenate(e2[2 * u:2 * u + 2], axis=0).astype(BF16)
    return jnp.max(tie)


def _route_kernel(h2_ref, wpq_ref, keys_ref, lcol_ref, e1_ref, r2_ref, e2_ref,
                  q_s, score_s, rank_s, vals_s, cand_s, sel_s):
    q = jnp.dot(h2_ref[...], wpq_ref[...], preferred_element_type=F32)
    for hp in range(2 * PEER_HEADS):
        q_s[hp] = q[:, hp * N_KEYS:(hp + 1) * N_KEYS].astype(BF16)
    nt = (((1,), (1,)), ((), ()))

    def per_head(h, carry):
        for p in range(2):
            score_s[p] = lax.dot_general(keys_ref[h * 2 + p], q_s[h * 2 + p], nt,
                                         preferred_element_type=F32)
        tie = _route_fast(h, score_s, vals_s, cand_s, lcol_ref, e1_ref, r2_ref, e2_ref)

        @pl.when(tie > 0.0)
        def _():
            _route_exact(h, score_s, rank_s, vals_s, cand_s, sel_s, lcol_ref, e1_ref, r2_ref, e2_ref)

        return carry

    lax.fori_loop(0, PEER_HEADS, per_head, 0)


def _route(h2, w_pq16, keys16, tm):
    bsz, t, d = h2.shape
    nq = w_pq16.shape[1]
    out = pl.BlockSpec((None, PEER_HEADS, N_KEYS, tm), lambda b, i: (b, 0, 0, i))
    shape = jax.ShapeDtypeStruct((bsz, PEER_HEADS, N_KEYS, t), F32)
    shape16 = jax.ShapeDtypeStruct((bsz, PEER_HEADS, N_KEYS, t), BF16)
    return pl.pallas_call(
        _route_kernel,
        grid=(bsz, t // tm),
        in_specs=[pl.BlockSpec((None, tm, d), lambda b, i: (b, i, 0)),
                  pl.BlockSpec((d, nq), lambda b, i: (0, 0)),
                  pl.BlockSpec((2 * PEER_HEADS, N_KEYS, nq // (2 * PEER_HEADS)), lambda b, i: (0, 0, 0))],
        out_specs=[out, out, out, out],
        out_shape=[shape, shape, shape16, shape16],
        scratch_shapes=[pltpu.VMEM((2 * PEER_HEADS, tm, nq // (2 * PEER_HEADS)), BF16),
                        pltpu.VMEM((2, N_KEYS, tm), F32),
                        pltpu.VMEM((2, N_KEYS, tm), F32),
                        pltpu.VMEM((2, PEER_TOPK, tm), F32),
                        pltpu.VMEM((_N_CAND_PAD, tm), F32),
                        pltpu.VMEM((_N_CAND_PAD, tm), F32)],
        compiler_params=_params(2),
        name="route",
    )(h2, w_pq16, keys16)


_ACT_ROWS = 256


def _peer_kernel(h2_ref, u_ref, vt_ref, lcol_ref, e1_ref, r2_ref, e2_ref, x1_ref, g2_ref,
                 lg_ref, lb_ref, y_ref, acc, p_s, *, te):
    j = pl.program_id(2)

    @pl.when(j == 0)
    def _():
        acc[...] = jnp.zeros(acc.shape, F32)

    zero = jnp.zeros((), BF16)
    n_sub = _ACT_ROWS // N_KEYS
    for c in range(te // _ACT_ROWS):
        chunk = slice(c * _ACT_ROWS, (c + 1) * _ACT_ROWS)
        act = lax.dot_general(u_ref[chunk, :], h2_ref[...], (((1,), (1,)), ((), ())),
                              preferred_element_type=F32)
        for sub in range(n_sub):
            al = c * n_sub + sub
            w = None
            for h in range(PEER_HEADS):
                keep = r2_ref[h] < lcol_ref[h, al:al + 1, :].astype(BF16)
                term = jnp.where(keep, e2_ref[h] * e1_ref[h, al:al + 1, :].astype(BF16), zero)
                w = term if w is None else w + term
            g = _gelu(act[sub * N_KEYS:(sub + 1) * N_KEYS, :])
            p_s[al * N_KEYS:(al + 1) * N_KEYS, :] = (w.astype(F32) * g).astype(BF16)
    acc[...] += jnp.dot(vt_ref[...], p_s[...], preferred_element_type=F32)

    @pl.when(j == pl.num_programs(2) - 1)
    def _():
        ffn = acc[...].T
        y = _normalize(ALPHA * x1_ref[...] + g2_ref[...] * ffn)
        y_ref[...] = y * lg_ref[...] + lb_ref[...]


def _peer(h2, u16, vt16, lcol, e1, r2, e2, x1, gate2, ln_g, ln_b, tm, te):
    bsz, t, d = h2.shape
    n_exp = u16.shape[0]
    na = te // N_KEYS
    tok = pl.BlockSpec((None, tm, d), lambda b, i, j: (b, i, 0))
    tok1 = pl.BlockSpec((None, tm, d), lambda b, i, j: (b, i, 0), pipeline_mode=pl.Buffered(1))
    vec = pl.BlockSpec((1, d), lambda b, i, j: (0, 0))
    by_a = pl.BlockSpec((None, PEER_HEADS, na, tm), lambda b, i, j: (b, 0, j, i))
    by_b = pl.BlockSpec((None, PEER_HEADS, N_KEYS, tm), lambda b, i, j: (b, 0, 0, i))
    return pl.pallas_call(
        functools.partial(_peer_kernel, te=te),
        grid=(bsz, t // tm, n_exp // te),
        in_specs=[tok,
                  pl.BlockSpec((te, d), lambda b, i, j: (j, 0)),
                  pl.BlockSpec((d, te), lambda b, i, j: (0, j)),
                  by_a, by_a, by_b, by_b, tok1, _mod_spec(gate2, tm, d), vec, vec],
        out_specs=tok1,
        out_shape=jax.ShapeDtypeStruct((bsz, t, d), F32),
        scratch_shapes=[pltpu.VMEM((d, tm), F32), pltpu.VMEM((te, tm), BF16)],
        compiler_params=_params(3),
        name="peer",
    )(h2, u16, vt16, lcol, e1, r2, e2, x1, gate2, ln_g.reshape(1, d), ln_b.reshape(1, d))


def _layer(x, mods, conv_buf, h0, attend, w, *, keep_rows, tm, tm_mix, tt, tc, tm_route, tm_peer, te, flatten):
    bsz, t, d = x.shape
    if flatten:
        mods = [jnp.broadcast_to(m, (bsz, t, d)).reshape(1, bsz * t, d) for m in mods]
        flat = lambda a: a.reshape(1, bsz * t, a.shape[-1])
    else:
        flat = lambda a: a
    shift1, scale1, gate1, shift2, scale2, gate2 = mods
    proj = _proj(flat(x), shift1, scale1, w["w_in"], tm).reshape(bsz, t, -1)
    pa, conv_state, rnn_state = _rnn(proj, conv_buf, h0, w["conv_w"], w["conv_b"], w["w_ra"], w["b_ra"],
                                     w["w_ri"], w["b_ri"], w["lam"], d, tt, tc)
    pb = attend(proj)
    x1, h2 = _mix(flat(pa), flat(pb), flat(x), w["w_out"], gate1, shift2, scale2, w["ln1_g"], w["ln1_b"], tm_mix)
    lcol, e1, r2, e2 = _route(h2, w["w_pq"], w["keys"], tm_route)
    y = _peer(h2, w["peer_u"], w["peer_vt"], lcol, e1, r2, e2, x1, gate2, w["ln2_g"], w["ln2_b"], tm_peer, te)
    n_heads = d // HEAD_DIM
    k = proj[:, t - keep_rows:, 3 * d:4 * d].reshape(bsz, keep_rows, n_heads, HEAD_DIM)
    v = proj[:, t - keep_rows:, 4 * d:5 * d].reshape(bsz, keep_rows, n_heads, HEAD_DIM)
    return y.reshape(bsz, t, d), conv_state, rnn_state, k, v


def kernel(x_prompt, x_sample, c_prompt, c_sample, state_conv, state_rnn, cache_k, cache_v, w_ada, b_ada, w_in, conv_w, conv_b, w_ra, b_ra, w_ri, b_ri, rg_lambda, rel_bias, w_out, ln1_g, ln1_b, w_pq, peer_keys, peer_u, peer_v, ln2_g, ln2_b):
    bsz_p, t_p, d = x_prompt.shape
    bsz_s, t_s, _ = x_sample.shape
    depth = w_ada.shape[0]
    keep = min(ATT_REACH, t_p)
    y_p, y_s = x_prompt, x_sample
    outs = [[] for _ in range(8)]
    rows = bsz_p + bsz_s
    rows_pad = -(-rows // SUBLANES) * SUBLANES
    c_all = jnp.concatenate([c_prompt, c_sample, jnp.zeros((rows_pad - rows, d), F32)], axis=0)
    tm = min(1024, t_p)
    for l in range(depth):
        w = dict(w_in=w_in[l].astype(BF16), conv_w=conv_w[l], conv_b=conv_b[l],
                 w_ra=w_ra[l].astype(BF16), b_ra=b_ra[l], w_ri=w_ri[l].astype(BF16), b_ri=b_ri[l],
                 lam=rg_lambda[l], w_out=w_out[l].astype(BF16), ln1_g=ln1_g[l], ln1_b=ln1_b[l],
                 w_pq=w_pq[l].astype(BF16),
                 keys=peer_keys[l].astype(BF16).reshape(2 * PEER_HEADS, N_KEYS, -1),
                 peer_u=peer_u[l].astype(BF16), peer_vt=peer_v[l].astype(BF16).T,
                 ln2_g=ln2_g[l], ln2_b=ln2_b[l])
        mod = _ada(c_all, w_ada[l], b_ada[l])
        mods_p = [m[:, None, :] for m in jnp.split(mod[:bsz_p], 6, axis=-1)]
        mods_s = [m[:, None, :] for m in jnp.split(mod[bsz_p:rows], 6, axis=-1)]

        y_p, cs, hs, k, v = _layer(
            y_p, mods_p, jnp.zeros((bsz_p, CONV_W - 1, d), F32), jnp.zeros((bsz_p, d), F32),
            lambda proj: _attn_prompt(proj, rel_bias[l], d), w,
            keep_rows=keep, tm=tm, tm_mix=min(256, t_p), tt=min(256, t_p), tc=512, tm_route=256, tm_peer=512,
            te=1024, flatten=False)
        for dst, val in zip(outs[:4], (cs, hs, k, v)):
            dst.append(val)

        y_s, cs, hs, k, v = _layer(
            y_s, mods_s, state_conv[l], state_rnn[l],
            lambda proj: _attn_sample(proj, cache_k[l], cache_v[l], rel_bias[l], d), w,
            keep_rows=t_s, tm=bsz_s * t_s, tm_mix=bsz_s * t_s, tt=t_s, tc=512, tm_route=bsz_s * t_s, tm_peer=bsz_s * t_s, te=1024, flatten=True)
        for dst, val in zip(outs[4:], (cs, hs, k, v)):
            dst.append(val)
    return (y_p, y_s) + tuple(jnp.stack(o) for o in outs)
```

```python
import functools

import jax
import jax.numpy as jnp
from jax import lax
from jax.experimental import pallas as pl
from jax.experimental.pallas import tpu as pltpu

F32 = jnp.float32
BF16 = jnp.bfloat16

CHUNK = 64
LEFT_CHUNKS = 8
ATT_REACH = LEFT_CHUNKS * CHUNK
MAX_REL = 128
REL_BUCKETS = MAX_REL + CHUNK
PAST_LEN = 1024
CONV_W = 4
RG_C = 8.0
RNN_BLOCK_DIM = 128
HEAD_DIM = 128
N_KEYS = 128
PEER_HEADS = 8
PEER_TOPK = 16
DEPTH = 1
ALPHA = (2.0 * DEPTH) ** 0.25
LN_EPS = 1e-5
NEG_INF = -1e30

LANES = 128
SUBLANES = 8
VMEM_LIMIT = 56 * 1024 * 1024

_CAND_COUNTS = tuple(PEER_TOPK // (r1 + 1) for r1 in range(PEER_TOPK))
_CAND_OFFSETS = tuple(sum(_CAND_COUNTS[:r1]) for r1 in range(PEER_TOPK))
_N_CAND = sum(_CAND_COUNTS)
_N_CAND_PAD = -(-_N_CAND // SUBLANES) * SUBLANES


def _params(n_grid):
    return pltpu.CompilerParams(dimension_semantics=("arbitrary",) * n_grid,
                                vmem_limit_bytes=VMEM_LIMIT)


def _normalize(x):
    mu = jnp.mean(x, axis=-1, keepdims=True)
    xc = x - mu
    var = jnp.mean(xc * xc, axis=-1, keepdims=True)
    return xc * lax.rsqrt(var + LN_EPS)


def _gelu(x):
    return 0.5 * x * (1.0 + lax.erf(x * (2.0 ** -0.5)))


def _ada_kernel(c_ref, w_ref, b_ref, o_ref):
    c = c_ref[...]
    s = c * jax.nn.sigmoid(c)
    o_ref[...] = jnp.dot(s, w_ref[...], preferred_element_type=F32) + b_ref[...]


def _ada(c, w_ada, b_ada):
    rows, d = c.shape
    n = w_ada.shape[1]
    tn = 1024
    return pl.pallas_call(
        _ada_kernel,
        grid=(n // tn,),
        in_specs=[pl.BlockSpec((rows, d), lambda j: (0, 0)),
                  pl.BlockSpec((d, tn), lambda j: (0, j)),
                  pl.BlockSpec((1, tn), lambda j: (0, j))],
        out_specs=pl.BlockSpec((rows, tn), lambda j: (0, j)),
        out_shape=jax.ShapeDtypeStruct((rows, n), F32),
        compiler_params=_params(1),
        name="ada",
    )(c, w_ada, b_ada.reshape(1, n))


def _proj_kernel(x_ref, sh_ref, sc_ref, w_ref, o_ref, h_ref):
    @pl.when(pl.program_id(2) == 0)
    def _():
        h = _normalize(x_ref[...]) * (1.0 + sc_ref[...]) + sh_ref[...]
        h_ref[...] = h.astype(BF16)

    o_ref[...] = jnp.dot(h_ref[...], w_ref[...], preferred_element_type=F32)


def _mod_spec(mod, tm, d):
    if mod.shape[1] == 1:
        return pl.BlockSpec((None, 1, d), lambda b, i, *_: (b, 0, 0))
    return pl.BlockSpec((None, tm, d), lambda b, i, *_: (b, i, 0))


def _proj(x, shift, scale, w_in_bf16, tm):
    bsz, t, d = x.shape
    n = w_in_bf16.shape[1]
    tn = 1024
    return pl.pallas_call(
        _proj_kernel,
        grid=(bsz, t // tm, n // tn),
        in_specs=[pl.BlockSpec((None, tm, d), lambda b, i, j: (b, i, 0)),
                  _mod_spec(shift, tm, d), _mod_spec(scale, tm, d),
                  pl.BlockSpec((d, tn), lambda b, i, j: (0, j))],
        out_specs=pl.BlockSpec((None, tm, tn), lambda b, i, j: (b, i, j)),
        out_shape=jax.ShapeDtypeStruct((bsz, t, n), F32),
        scratch_shapes=[pltpu.VMEM((tm, d), BF16)],
        compiler_params=_params(3),
        name="proj",
    )(x, shift, scale, w_in_bf16)


def _rnn_kernel(xr_ref, yr_ref, ga_ref, cbuf_ref, h0_ref, cw_ref, cb_ref, wra_ref, bra_ref,
                wri_ref, bri_ref, lam_ref, pa_ref, cst_ref, hst_ref,
                xbuf, a_s, u_s, hcar, *, tt, tc):
    t = pl.program_id(2)
    nt = pl.num_programs(2)
    pad = SUBLANES
    keep = CONV_W - 1

    @pl.when(t == 0)
    def _():
        xbuf[pad - keep:pad, :] = cbuf_ref[...]
        hcar[...] = h0_ref[...]

    xbuf[pad:pad + tt, :] = xr_ref[...]
    cw = cw_ref[...]
    xc = cb_ref[...] + cw[0:1, :] * xbuf[pad - keep:pad - keep + tt, :]
    for j in range(1, CONV_W):
        xc = xc + cw[j:j + 1, :] * xbuf[pad - keep + j:pad - keep + j + tt, :]
    tail = xbuf[pad + tt - keep:pad + tt, :]
    xbuf[pad - keep:pad, :] = tail

    lam = lam_ref[...]
    softplus_neg = jnp.maximum(-lam, 0.0) + jnp.log1p(jnp.exp(-jnp.abs(lam)))
    for blk in range(tc // RNN_BLOCK_DIM):
        cols = slice(blk * RNN_BLOCK_DIM, (blk + 1) * RNN_BLOCK_DIM)
        xb = xc[:, cols]
        xb16 = xb.astype(BF16)
        r = jax.nn.sigmoid(jnp.dot(xb16, wra_ref[blk], preferred_element_type=F32) + bra_ref[:, cols])
        i = jax.nn.sigmoid(jnp.dot(xb16, wri_ref[blk], preferred_element_type=F32) + bri_ref[:, cols])
        log_a = -RG_C * r * softplus_neg[:, cols]
        a = jnp.exp(log_a)
        u = jnp.sqrt(jnp.tanh(-log_a) * (1.0 + a * a)) * (i * xb)
        a_s[:, cols] = a
        u_s[:, cols] = u

    row = lax.broadcasted_iota(jnp.int32, (SUBLANES, tc), 0)

    def scan_group(g, h_prev):
        rows = pl.ds(pl.multiple_of(g * SUBLANES, SUBLANES), SUBLANES)
        a = a_s[rows, :]
        u = u_s[rows, :]
        for s in (1, 2, 4):
            a_sh = pltpu.roll(a, s, 0)
            u_sh = pltpu.roll(u, s, 0)
            live = row >= s
            u = jnp.where(live, a * u_sh + u, u)
            a = jnp.where(live, a * a_sh, a)
        h = a * h_prev + u
        u_s[rows, :] = h
        return h[SUBLANES - 1:SUBLANES, :]

    h_last = lax.fori_loop(0, tt // SUBLANES, scan_group, hcar[...])
    hcar[...] = h_last

    pa_ref[...] = u_s[...] * _gelu(yr_ref[...]) * jax.nn.sigmoid(ga_ref[...])

    @pl.when(t == nt - 1)
    def _():
        cst_ref[...] = tail
        hst_ref[...] = h_last


def _rnn(proj, conv_buf, h0, conv_w, conv_b, w_ra16, b_ra, w_ri16, b_ri, lam, d, tt, tc):
    bsz, t, _ = proj.shape
    nc = d // tc
    nb = tc // RNN_BLOCK_DIM
    keep = CONV_W - 1
    col = lambda off: (lambda b, c, i: (b, i, off * nc + c))
    vec = pl.BlockSpec((1, tc), lambda b, c, i: (0, c))
    blk = pl.BlockSpec((nb, RNN_BLOCK_DIM, RNN_BLOCK_DIM), lambda b, c, i: (c, 0, 0))
    pa, cst, hst = pl.pallas_call(
        functools.partial(_rnn_kernel, tt=tt, tc=tc),
        grid=(bsz, nc, t // tt),
        in_specs=[pl.BlockSpec((None, tt, tc), col(0)),
                  pl.BlockSpec((None, tt, tc), col(1)),
                  pl.BlockSpec((None, tt, tc), col(5)),
                  pl.BlockSpec((None, keep, tc), lambda b, c, i: (b, 0, c)),
                  pl.BlockSpec((None, 1, tc), lambda b, c, i: (b, 0, c)),
                  pl.BlockSpec((CONV_W, tc), lambda b, c, i: (0, c)),
                  vec, blk, vec, blk, vec, vec],
        out_specs=[pl.BlockSpec((None, tt, tc), lambda b, c, i: (b, i, c)),
                   pl.BlockSpec((None, keep, tc), lambda b, c, i: (b, 0, c)),
                   pl.BlockSpec((None, 1, tc), lambda b, c, i: (b, 0, c))],
        out_shape=[jax.ShapeDtypeStruct((bsz, t, d), F32),
                   jax.ShapeDtypeStruct((bsz, keep, d), F32),
                   jax.ShapeDtypeStruct((bsz, 1, d), F32)],
        scratch_shapes=[pltpu.VMEM((tt + SUBLANES, tc), F32),
                        pltpu.VMEM((tt, tc), F32),
                        pltpu.VMEM((tt, tc), F32),
                        pltpu.VMEM((1, tc), F32)],
        compiler_params=_params(3),
        name="rnn",
    )(proj, proj, proj, conv_buf, h0.reshape(bsz, 1, d), conv_w, conv_b.reshape(1, d),
      w_ra16, b_ra.reshape(1, d), w_ri16, b_ri.reshape(1, d), lam.reshape(1, d))
    return pa, cst, hst.reshape(bsz, d)


def _rel_bias_tile(rel_ref, head, dist, visible):
    idx = jnp.clip(dist, -(CHUNK - 1), MAX_REL) + (CHUNK - 1)

    def body(j, acc):
        return acc + jnp.where(idx == j, rel_ref[head, j], 0.0)

    bias = lax.fori_loop(0, REL_BUCKETS, body, jnp.zeros(dist.shape, F32))
    return jnp.where(visible, bias, NEG_INF)


def _softmax_pv(s, v16):
    m = jnp.max(s, axis=-1, keepdims=True)
    p = jnp.exp(s - m)
    l = jnp.sum(p, axis=-1, keepdims=True)
    return jnp.dot(p.astype(BF16), v16, preferred_element_type=F32) / l


_QSUB = 2 * CHUNK
_BAND = _QSUB + ATT_REACH


_ATTN_HEADS = 4


def _attn_prompt_kernel(rel_ref, q_ref, kp_ref, kc_ref, vp_ref, vc_ref, gb_ref, o_ref,
                        bias_s, kcat, vcat):
    group, b, i = pl.program_id(0), pl.program_id(1), pl.program_id(2)
    tq = ATT_REACH

    @pl.when((b == 0) & (i == 0))
    def _():
        r = lax.broadcasted_iota(jnp.int32, (_QSUB, _BAND), 0)
        c = lax.broadcasted_iota(jnp.int32, (_QSUB, _BAND), 1)
        qc = r // CHUNK + LEFT_CHUNKS
        kc = c // CHUNK
        visible = (kc <= qc) & (kc >= qc - LEFT_CHUNKS)
        for hh in range(_ATTN_HEADS):
            bias_s[hh] = _rel_bias_tile(rel_ref, group * _ATTN_HEADS + hh, r + ATT_REACH - c, visible)

    kcat[0:tq, :] = kp_ref[...].astype(BF16)
    kcat[tq:2 * tq, :] = kc_ref[...].astype(BF16)
    vcat[0:tq, :] = vp_ref[...].astype(BF16)
    vcat[tq:2 * tq, :] = vc_ref[...].astype(BF16)
    col = lax.broadcasted_iota(jnp.int32, (_QSUB, _BAND), 1)
    n_sub = tq // _QSUB
    for hh in range(_ATTN_HEADS):
        cols = slice(hh * HEAD_DIM, (hh + 1) * HEAD_DIM)
        scores = []
        for s in range(n_sub):
            q16 = q_ref[s * _QSUB:(s + 1) * _QSUB, cols].astype(BF16)
            sc = lax.dot_general(q16, kcat[s * _QSUB:s * _QSUB + _BAND, cols], (((1,), (1,)), ((), ())),
                                 preferred_element_type=F32) * (HEAD_DIM ** -0.5)
            sc = sc + bias_s[hh]
            k_pos = col + (i * tq + s * _QSUB - ATT_REACH)
            scores.append(jnp.where(k_pos >= 0, sc, NEG_INF))
        maxes = [jnp.max(sc, axis=-1, keepdims=True) for sc in scores]
        probs = [jnp.exp(sc - m) for sc, m in zip(scores, maxes)]
        sums = [jnp.sum(p, axis=-1, keepdims=True) for p in probs]
        outs = [jnp.dot(p.astype(BF16), vcat[s * _QSUB:s * _QSUB + _BAND, cols], preferred_element_type=F32)
                for s, p in enumerate(probs)]
        for s in range(n_sub):
            rows = slice(s * _QSUB, (s + 1) * _QSUB)
            o_ref[rows, cols] = outs[s] / sums[s] * jax.nn.sigmoid(gb_ref[rows, cols])


def _attn_prompt(proj, rel_bias, d):
    bsz, t, _ = proj.shape
    ng = d // (HEAD_DIM * _ATTN_HEADS)
    width = HEAD_DIM * _ATTN_HEADS
    tq = ATT_REACH
    cur = lambda off: (lambda g, b, i: (b, i, off * ng + g))
    prev = lambda off: (lambda g, b, i: (b, jnp.maximum(i - 1, 0), off * ng + g))
    blk = lambda f: pl.BlockSpec((None, tq, width), f)
    return pl.pallas_call(
        _attn_prompt_kernel,
        grid=(ng, bsz, t // tq),
        in_specs=[pl.BlockSpec(memory_space=pltpu.SMEM),
                  blk(cur(2)), blk(prev(3)), blk(cur(3)), blk(prev(4)), blk(cur(4)), blk(cur(6))],
        out_specs=pl.BlockSpec((None, tq, width), lambda g, b, i: (b, i, g)),
        out_shape=jax.ShapeDtypeStruct((bsz, t, d), F32),
        scratch_shapes=[pltpu.VMEM((_ATTN_HEADS, _QSUB, _BAND), F32),
                        pltpu.VMEM((2 * tq, width), BF16),
                        pltpu.VMEM((2 * tq, width), BF16)],
        compiler_params=_params(3),
        name="attn_prompt",
    )(rel_bias, proj, proj, proj, proj, proj, proj)


def _attn_sample_kernel(rel_ref, q_ref, kn_ref, vn_ref, gb_ref, ck_ref, cv_ref, o_ref,
                        bias_past_s, bias_new_s, *, t, n_past):
    group = pl.program_id(0)

    def bias(head, n_keys, first_pos):
        r = lax.broadcasted_iota(jnp.int32, (t, n_keys), 0)
        c = lax.broadcasted_iota(jnp.int32, (t, n_keys), 1)
        q_pos = PAST_LEN + r
        k_pos = first_pos + c
        qc = q_pos // CHUNK
        kc = k_pos // CHUNK
        visible = (k_pos >= 0) & (kc <= qc) & (kc >= qc - LEFT_CHUNKS)
        return _rel_bias_tile(rel_ref, head, q_pos - k_pos, visible)

    @pl.when(pl.program_id(1) == 0)
    def _():
        for hh in range(_ATTN_HEADS):
            bias_past_s[hh] = bias(group * _ATTN_HEADS + hh, n_past, PAST_LEN - n_past)
            bias_new_s[hh] = bias(group * _ATTN_HEADS + hh, t, PAST_LEN)

    scale = HEAD_DIM ** -0.5
    nt = (((1,), (1,)), ((), ()))
    for hh in range(_ATTN_HEADS):
        cols = slice(hh * HEAD_DIM, (hh + 1) * HEAD_DIM)
        q16 = q_ref[:, cols].astype(BF16)
        s_past = lax.dot_general(q16, ck_ref[:, cols].astype(BF16), nt, preferred_element_type=F32) * scale
        s_new = lax.dot_general(q16, kn_ref[:, cols].astype(BF16), nt, preferred_element_type=F32) * scale
        s_past = s_past + bias_past_s[hh]
        s_new = s_new + bias_new_s[hh]
        m = jnp.maximum(jnp.max(s_past, axis=-1, keepdims=True), jnp.max(s_new, axis=-1, keepdims=True))
        p_past = jnp.exp(s_past - m)
        p_new = jnp.exp(s_new - m)
        l = jnp.sum(p_past, axis=-1, keepdims=True) + jnp.sum(p_new, axis=-1, keepdims=True)
        o = (jnp.dot(p_past.astype(BF16), cv_ref[:, cols].astype(BF16), preferred_element_type=F32)
             + jnp.dot(p_new.astype(BF16), vn_ref[:, cols].astype(BF16), preferred_element_type=F32)) / l
        o_ref[:, cols] = o * jax.nn.sigmoid(gb_ref[:, cols])


def _attn_sample(proj, cache_k, cache_v, rel_bias, d):
    bsz, t, _ = proj.shape
    width = HEAD_DIM * _ATTN_HEADS
    ng = d // width
    n_past = cache_k.shape[1]
    new = lambda off: pl.BlockSpec((None, t, width), lambda g, b: (b, 0, off * ng + g))
    past = pl.BlockSpec((None, n_past, width), lambda g, b: (b, 0, g))
    return pl.pallas_call(
        functools.partial(_attn_sample_kernel, t=t, n_past=n_past),
        grid=(ng, bsz),
        in_specs=[pl.BlockSpec(memory_space=pltpu.SMEM), new(2), new(3), new(4), new(6), past, past],
        out_specs=pl.BlockSpec((None, t, width), lambda g, b: (b, 0, g)),
        out_shape=jax.ShapeDtypeStruct((bsz, t, d), F32),
        scratch_shapes=[pltpu.VMEM((_ATTN_HEADS, t, n_past), F32), pltpu.VMEM((_ATTN_HEADS, t, t), F32)],
        compiler_params=_params(2),
        name="attn_sample",
    )(rel_bias, proj, proj, proj, proj, cache_k.reshape(bsz, n_past, d), cache_v.reshape(bsz, n_past, d))


def _mix_kernel(pa_ref, pb_ref, x_ref, w_ref, g1_ref, sh2_ref, sc2_ref, lg_ref, lb_ref,
                x1_ref, h2_ref):
    mixed = jnp.dot((pa_ref[...] + pb_ref[...]).astype(BF16), w_ref[...], preferred_element_type=F32)
    x1 = _normalize(ALPHA * x_ref[...] + g1_ref[...] * mixed) * lg_ref[...] + lb_ref[...]
    x1_ref[...] = x1
    h2_ref[...] = (_normalize(x1) * (1.0 + sc2_ref[...]) + sh2_ref[...]).astype(BF16)


def _mix(pa, pb, x, w_out16, gate1, shift2, scale2, ln_g, ln_b, tm):
    bsz, t, d = x.shape
    tok = pl.BlockSpec((None, tm, d), lambda b, i: (b, i, 0))
    vec = pl.BlockSpec((1, d), lambda b, i: (0, 0))
    return pl.pallas_call(
        _mix_kernel,
        grid=(bsz, t // tm),
        in_specs=[tok, tok, tok, pl.BlockSpec((d, d), lambda b, i: (0, 0)),
                  _mod_spec(gate1, tm, d), _mod_spec(shift2, tm, d), _mod_spec(scale2, tm, d), vec, vec],
        out_specs=[tok, tok],
        out_shape=[jax.ShapeDtypeStruct((bsz, t, d), F32), jax.ShapeDtypeStruct((bsz, t, d), BF16)],
        compiler_params=_params(2),
        name="mix",
    )(pa, pb, x, w_out16, gate1, shift2, scale2, ln_g.reshape(1, d), ln_b.reshape(1, d))


def _take_top(s, n_iter, rows_f, on_pick):
    def body(it, s_rem):
        m = jnp.max(s_rem, axis=0, keepdims=True)
        first = jnp.min(jnp.where(s_rem == m, rows_f, float(s.shape[0])), axis=0, keepdims=True)
        pick = rows_f == first
        on_pick(it, m, pick)
        return jnp.where(pick, -jnp.inf, s_rem)

    lax.fori_loop(0, n_iter, body, s)


def _fill_candidates(vals_s, cand_s):
    cand_s[...] = jnp.full(cand_s.shape, -jnp.inf, F32)
    for r1 in range(PEER_TOPK):
        off, cnt = _CAND_OFFSETS[r1], _CAND_COUNTS[r1]
        cand_s[off:off + cnt, :] = vals_s[0, r1:r1 + 1, :] + vals_s[1, 0:cnt, :]


def _route_exact(h, score_s, rank_s, vals_s, cand_s, sel_s, lcol_ref, e1_ref, r2_ref, e2_ref):
    tm = score_s.shape[-1]
    rows_f = lax.broadcasted_iota(jnp.int32, (N_KEYS, tm), 0).astype(F32)
    crow_f = lax.broadcasted_iota(jnp.int32, (_N_CAND_PAD, tm), 0).astype(F32)
    for p in range(2):
        rank_s[p] = jnp.full((N_KEYS, tm), float(PEER_TOPK), F32)

        def on_pick(it, m, pick, p=p):
            vals_s[p, pl.ds(it, 1), :] = m
            rank_s[p] = jnp.where(pick, jnp.asarray(it, F32), rank_s[p])

        _take_top(score_s[p], PEER_TOPK, rows_f, on_pick)

    _fill_candidates(vals_s, cand_s)
    sel_s[...] = jnp.zeros((_N_CAND_PAD, tm), F32)

    def on_pick_cand(it, m, pick):
        sel_s[...] = jnp.where(pick, 1.0, sel_s[...])

    _take_top(cand_s[...], PEER_TOPK, crow_f, on_pick_cand)

    sel = sel_s[...]
    top = vals_s[0, 0:1, :] + vals_s[1, 0:1, :]
    z = jnp.sum(jnp.where(sel > 0.0, jnp.exp(cand_s[...] - top), 0.0), axis=0, keepdims=True)
    rank1 = rank_s[0]
    lcol = jnp.zeros((N_KEYS, tm), F32)
    for r1 in range(PEER_TOPK):
        off, cnt = _CAND_OFFSETS[r1], _CAND_COUNTS[r1]
        n_sel = jnp.sum(sel[off:off + cnt, :], axis=0, keepdims=True)
        lcol = lcol + jnp.where(rank1 == float(r1), n_sel, 0.0)
    lcol_ref[h] = lcol
    e1_ref[h] = jnp.exp(score_s[0] - vals_s[0, 0:1, :]) / z
    r2_ref[h] = rank_s[1].astype(BF16)
    e2_ref[h] = jnp.exp(score_s[1] - vals_s[1, 0:1, :]).astype(BF16)


def _compare_exchange(xs, i, l, descending):
    hi, lo = jnp.maximum(xs[i], xs[l]), jnp.minimum(xs[i], xs[l])
    xs[i], xs[l] = (hi, lo) if descending else (lo, hi)


def _bitonic_merge(xs):
    xs = list(xs)
    j = len(xs) // 2
    while j >= 1:
        for i in range(len(xs)):
            if i & j == 0:
                _compare_exchange(xs, i, i | j, True)
        j //= 2
    return xs


def _bitonic_sort(xs):
    xs = list(xs)
    k = 2
    while k <= len(xs):
        j = k // 2
        while j >= 1:
            for i in range(len(xs)):
                if i & j == 0:
                    _compare_exchange(xs, i, i | j, i & k == 0)
            j //= 2
        k *= 2
    return xs


def _sorted_top(pieces):
    xs = _bitonic_sort(pieces)
    for shift in (1, 2, 4):
        other = [pltpu.roll(x, shift, 0) for x in xs][::-1]
        if len(xs) < PEER_TOPK:
            xs = _bitonic_merge(xs + other)
        else:
            xs = _bitonic_merge([jnp.maximum(a, b) for a, b in zip(xs, other)])
    return xs


def _sublane_sum(x):
    for shift in (4, 2, 1):
        x = x + pltpu.roll(x, shift, 0)
    return x


def _route_fast(h, score_s, vals_s, cand_s, lcol_ref, e1_ref, r2_ref, e2_ref):
    tm = score_s.shape[-1]
    n_slab = N_KEYS // SUBLANES
    slabs = [[score_s[p, v * SUBLANES:(v + 1) * SUBLANES, :] for v in range(n_slab)] for p in range(2)]
    tops = [_sorted_top(slabs[p]) for p in range(2)]
    for p in range(2):
        for j in range(PEER_TOPK):
            vals_s[p, j:j + 1, :] = tops[p][j][0:1, :]
    _fill_candidates(vals_s, cand_s)
    cands = [cand_s[k * SUBLANES:(k + 1) * SUBLANES, :] for k in range(_N_CAND_PAD // SUBLANES)]
    pad = [jnp.full((SUBLANES, tm), -jnp.inf, F32)] * (SUBLANES - len(cands))
    tau = _sorted_top(cands + pad)[PEER_TOPK - 1]
    top = tops[0][0] + tops[1][0]
    n_picked = jnp.zeros((SUBLANES, tm), F32)
    z = jnp.zeros((SUBLANES, tm), F32)
    for c in cands:
        n_picked = n_picked + jnp.where(c >= tau, 1.0, 0.0)
        z = z + jnp.where(c >= tau, jnp.exp(c - top), 0.0)
    inv_z = 1.0 / _sublane_sum(z)
    tie = jnp.where(_sublane_sum(n_picked) != float(PEER_TOPK), 1.0, 0.0)
    for p in range(2):
        n_ge = jnp.zeros((SUBLANES, tm), F32)
        for s in slabs[p]:
            n_ge = n_ge + jnp.where(s >= tops[p][PEER_TOPK - 1], 1.0, 0.0)
        tie = jnp.maximum(tie, jnp.where(_sublane_sum(n_ge) != float(PEER_TOPK), 1.0, 0.0))
        for j in range(PEER_TOPK - 1):
            tie = jnp.maximum(tie, jnp.where(tops[p][j] == tops[p][j + 1], 1.0, 0.0))
    n_pick = []
    for r1 in range(PEER_TOPK):
        n = jnp.zeros((SUBLANES, tm), F32)
        for r2 in range(_CAND_COUNTS[r1]):
            n = n + jnp.where(tops[0][r1] + tops[1][r2] >= tau, 1.0, 0.0)
        n_pick.append(n)
    rank2, e2 = [], []
    for v in range(n_slab):
        rows = slice(v * SUBLANES, (v + 1) * SUBLANES)
        lc = jnp.zeros((SUBLANES, tm), F32)
        rk = jnp.full((SUBLANES, tm), float(PEER_TOPK), F32)
        for j in range(PEER_TOPK):
            lc = jnp.where(slabs[0][v] == tops[0][j], n_pick[j], lc)
            rk = jnp.where(slabs[1][v] == tops[1][j], float(j), rk)
        lcol_ref[h, rows, :] = lc
        e1_ref[h, rows, :] = jnp.exp(slabs[0][v] - tops[0][0]) * inv_z
        rank2.append(rk)
        e2.append(jnp.exp(slabs[1][v] - tops[1][0]))
    for u in range(n_slab // 2):
        rows = slice(2 * u * SUBLANES, (2 * u + 2) * SUBLANES)
        r2_ref[h, rows, :] = jnp.concatenate(rank2[2 * u:2 * u + 2], axis=0).astype(BF16)
        e2_ref[h, rows, :] = jnp.concatenate(e2[2 * u:2 * u + 2], axis=0).astype(BF16)
    return jnp.max(tie)


def _route_kernel(h2_ref, wpq_ref, keys_ref, lcol_ref, e1_ref, r2_ref, e2_ref,
                  q_s, score_s, rank_s, vals_s, cand_s, sel_s):
    q = jnp.dot(h2_ref[...], wpq_ref[...], preferred_element_type=F32)
    for hp in range(2 * PEER_HEADS):
        q_s[hp] = q[:, hp * N_KEYS:(hp + 1) * N_KEYS].astype(BF16)
    nt = (((1,), (1,)), ((), ()))

    def per_head(h, carry):
        for p in range(2):
            score_s[p] = lax.dot_general(keys_ref[h * 2 + p], q_s[h * 2 + p], nt,
                                         preferred_element_type=F32)
        tie = _route_fast(h, score_s, vals_s, cand_s, lcol_ref, e1_ref, r2_ref, e2_ref)

        @pl.when(tie > 0.0)
        def _():
            _route_exact(h, score_s, rank_s, vals_s, cand_s, sel_s, lcol_ref, e1_ref, r2_ref, e2_ref)

        return carry

    lax.fori_loop(0, PEER_HEADS, per_head, 0)


def _route(h2, w_pq16, keys16, tm):
    bsz, t, d = h2.shape
    nq = w_pq16.shape[1]
    out = pl.BlockSpec((None, PEER_HEADS, N_KEYS, tm), lambda b, i: (b, 0, 0, i))
    shape = jax.ShapeDtypeStruct((bsz, PEER_HEADS, N_KEYS, t), F32)
    shape16 = jax.ShapeDtypeStruct((bsz, PEER_HEADS, N_KEYS, t), BF16)
    return pl.pallas_call(
        _route_kernel,
        grid=(bsz, t // tm),
        in_specs=[pl.BlockSpec((None, tm, d), lambda b, i: (b, i, 0)),
                  pl.BlockSpec((d, nq), lambda b, i: (0, 0)),
                  pl.BlockSpec((2 * PEER_HEADS, N_KEYS, nq // (2 * PEER_HEADS)), lambda b, i: (0, 0, 0))],
        out_specs=[out, out, out, out],
        out_shape=[shape, shape, shape16, shape16],
        scratch_shapes=[pltpu.VMEM((2 * PEER_HEADS, tm, nq // (2 * PEER_HEADS)), BF16),
                        pltpu.VMEM((2, N_KEYS, tm), F32),
                        pltpu.VMEM((2, N_KEYS, tm), F32),
                        pltpu.VMEM((2, PEER_TOPK, tm), F32),
                        pltpu.VMEM((_N_CAND_PAD, tm), F32),
                        pltpu.VMEM((_N_CAND_PAD, tm), F32)],
        compiler_params=_params(2),
        name="route",
    )(h2, w_pq16, keys16)


_ACT_ROWS = 256


def _peer_kernel(h2_ref, u_ref, vt_ref, lcol_ref, e1_ref, r2_ref, e2_ref, x1_ref, g2_ref,
                 lg_ref, lb_ref, y_ref, acc, p_s, *, te):
    j = pl.program_id(2)

    @pl.when(j == 0)
    def _():
        acc[...] = jnp.zeros(acc.shape, F32)

    zero = jnp.zeros((), BF16)
    n_sub = _ACT_ROWS // N_KEYS
    for c in range(te // _ACT_ROWS):
        chunk = slice(c * _ACT_ROWS, (c + 1) * _ACT_ROWS)
        act = lax.dot_general(u_ref[chunk, :], h2_ref[...], (((1,), (1,)), ((), ())),
                              preferred_element_type=F32)
        for sub in range(n_sub):
            al = c * n_sub + sub
            w = None
            for h in range(PEER_HEADS):
                keep = r2_ref[h] < lcol_ref[h, al:al + 1, :].astype(BF16)
                term = jnp.where(keep, e2_ref[h] * e1_ref[h, al:al + 1, :].astype(BF16), zero)
                w = term if w is None else w + term
            g = _gelu(act[sub * N_KEYS:(sub + 1) * N_KEYS, :])
            p_s[al * N_KEYS:(al + 1) * N_KEYS, :] = (w.astype(F32) * g).astype(BF16)
    acc[...] += jnp.dot(vt_ref[...], p_s[...], preferred_element_type=F32)

    @pl.when(j == pl.num_programs(2) - 1)
    def _():
        ffn = acc[...].T
        y = _normalize(ALPHA * x1_ref[...] + g2_ref[...] * ffn)
        y_ref[...] = y * lg_ref[...] + lb_ref[...]


def _peer(h2, u16, vt16, lcol, e1, r2, e2, x1, gate2, ln_g, ln_b, tm, te):
    bsz, t, d = h2.shape
    n_exp = u16.shape[0]
    na = te // N_KEYS
    tok = pl.BlockSpec((None, tm, d), lambda b, i, j: (b, i, 0))
    tok1 = pl.BlockSpec((None, tm, d), lambda b, i, j: (b, i, 0), pipeline_mode=pl.Buffered(1))
    vec = pl.BlockSpec((1, d), lambda b, i, j: (0, 0))
    by_a = pl.BlockSpec((None, PEER_HEADS, na, tm), lambda b, i, j: (b, 0, j, i))
    by_b = pl.BlockSpec((None, PEER_HEADS, N_KEYS, tm), lambda b, i, j: (b, 0, 0, i))
    return pl.pallas_call(
        functools.partial(_peer_kernel, te=te),
        grid=(bsz, t // tm, n_exp // te),
        in_specs=[tok,
                  pl.BlockSpec((te, d), lambda b, i, j: (j, 0)),
                  pl.BlockSpec((d, te), lambda b, i, j: (0, j)),
                  by_a, by_a, by_b, by_b, tok1, _mod_spec(gate2, tm, d), vec, vec],
        out_specs=tok1,
        out_shape=jax.ShapeDtypeStruct((bsz, t, d), F32),
        scratch_shapes=[pltpu.VMEM((d, tm), F32), pltpu.VMEM((te, tm), BF16)],
        compiler_params=_params(3),
        name="peer",
    )(h2, u16, vt16, lcol, e1, r2, e2, x1, gate2, ln_g.reshape(1, d), ln_b.reshape(1, d))


def _tiles(n_tok, t):
    return dict(tm_proj=min(1024, n_tok),
                tm_mix=min(256, n_tok),
                tt=min(256, t), tc=512,
                tm_route=min(512, n_tok),
                tm_peer=min(512, n_tok), te=1024)


def _layer(x, mods, conv_buf, h0, attend, w, *, keep_rows, flatten):
    bsz, t, d = x.shape
    tl = _tiles(bsz * t if flatten else t, t)
    if flatten:
        mods = [jnp.broadcast_to(m, (bsz, t, d)).reshape(1, bsz * t, d) for m in mods]
        flat = lambda a: a.reshape(1, bsz * t, a.shape[-1])
    else:
        flat = lambda a: a
    shift1, scale1, gate1, shift2, scale2, gate2 = mods
    proj = _proj(flat(x), shift1, scale1, w["w_in"], tl["tm_proj"]).reshape(bsz, t, -1)
    pa, conv_state, rnn_state = _rnn(proj, conv_buf, h0, w["conv_w"], w["conv_b"], w["w_ra"], w["b_ra"],
                                     w["w_ri"], w["b_ri"], w["lam"], d, tl["tt"], tl["tc"])
    pb = attend(proj)
    x1, h2 = _mix(flat(pa), flat(pb), flat(x), w["w_out"], gate1, shift2, scale2, w["ln1_g"], w["ln1_b"],
                  tl["tm_mix"])
    lcol, e1, r2, e2 = _route(h2, w["w_pq"], w["keys"], tl["tm_route"])
    y = _peer(h2, w["peer_u"], w["peer_vt"], lcol, e1, r2, e2, x1, gate2, w["ln2_g"], w["ln2_b"],
              tl["tm_peer"], tl["te"])
    n_heads = d // HEAD_DIM
    k = proj[:, t - keep_rows:, 3 * d:4 * d].reshape(bsz, keep_rows, n_heads, HEAD_DIM)
    v = proj[:, t - keep_rows:, 4 * d:5 * d].reshape(bsz, keep_rows, n_heads, HEAD_DIM)
    return y.reshape(bsz, t, d), conv_state, rnn_state, k, v


def kernel(x_prompt, x_sample, c_prompt, c_sample, state_conv, state_rnn, cache_k, cache_v, w_ada, b_ada, w_in, conv_w, conv_b, w_ra, b_ra, w_ri, b_ri, rg_lambda, rel_bias, w_out, ln1_g, ln1_b, w_pq, peer_keys, peer_u, peer_v, ln2_g, ln2_b):
    bsz_p, t_p, d = x_prompt.shape
    bsz_s, t_s, _ = x_sample.shape
    depth = w_ada.shape[0]
    keep = min(ATT_REACH, t_p)
    y_p, y_s = x_prompt, x_sample
    outs = [[] for _ in range(8)]
    rows = bsz_p + bsz_s
    rows_pad = -(-rows // SUBLANES) * SUBLANES
    c_all = jnp.concatenate([c_prompt, c_sample, jnp.zeros((rows_pad - rows, d), F32)], axis=0)
    for l in range(depth):
        w = dict(w_in=w_in[l].astype(BF16), conv_w=conv_w[l], conv_b=conv_b[l],
                 w_ra=w_ra[l].astype(BF16), b_ra=b_ra[l], w_ri=w_ri[l].astype(BF16), b_ri=b_ri[l],
                 lam=rg_lambda[l], w_out=w_out[l].astype(BF16), ln1_g=ln1_g[l], ln1_b=ln1_b[l],
                 w_pq=w_pq[l].astype(BF16),
                 keys=peer_keys[l].astype(BF16).reshape(2 * PEER_HEADS, N_KEYS, -1),
                 peer_u=peer_u[l].astype(BF16), peer_vt=peer_v[l].astype(BF16).T,
                 ln2_g=ln2_g[l], ln2_b=ln2_b[l])
        mod = _ada(c_all, w_ada[l], b_ada[l])
        mods_p = [m[:, None, :] for m in jnp.split(mod[:bsz_p], 6, axis=-1)]
        mods_s = [m[:, None, :] for m in jnp.split(mod[bsz_p:rows], 6, axis=-1)]

        y_p, cs, hs, k, v = _layer(
            y_p, mods_p, jnp.zeros((bsz_p, CONV_W - 1, d), F32), jnp.zeros((bsz_p, d), F32),
            lambda proj: _attn_prompt(proj, rel_bias[l], d), w,
            keep_rows=keep, flatten=False)
        for dst, val in zip(outs[:4], (cs, hs, k, v)):
            dst.append(val)

        y_s, cs, hs, k, v = _layer(
            y_s, mods_s, state_conv[l], state_rnn[l],
            lambda proj: _attn_sample(proj, cache_k[l], cache_v[l], rel_bias[l], d), w,
            keep_rows=t_s, flatten=True)
        for dst, val in zip(outs[4:], (cs, hs, k, v)):
            dst.append(val)
    return (y_p, y_s) + tuple(jnp.stack(o) for o in outs)
```

```python
import functools

import jax
import jax.numpy as jnp
from jax import lax
from jax.experimental import pallas as pl
from jax.experimental.pallas import tpu as pltpu

F32 = jnp.float32
BF16 = jnp.bfloat16

CHUNK = 64
LEFT_CHUNKS = 8
ATT_REACH = LEFT_CHUNKS * CHUNK
MAX_REL = 128
REL_BUCKETS = MAX_REL + CHUNK
PAST_LEN = 1024
CONV_W = 4
RG_C = 8.0
RNN_BLOCK_DIM = 128
HEAD_DIM = 128
N_KEYS = 128
PEER_HEADS = 8
PEER_TOPK = 16
DEPTH = 1
ALPHA = (2.0 * DEPTH) ** 0.25
LN_EPS = 1e-5
NEG_INF = -1e30

LANES = 128
SUBLANES = 8
VMEM_LIMIT = 56 * 1024 * 1024

_CAND_COUNTS = tuple(PEER_TOPK // (r1 + 1) for r1 in range(PEER_TOPK))
_CAND_OFFSETS = tuple(sum(_CAND_COUNTS[:r1]) for r1 in range(PEER_TOPK))
_N_CAND = sum(_CAND_COUNTS)
_N_CAND_PAD = -(-_N_CAND // SUBLANES) * SUBLANES


def _params(n_grid):
    return pltpu.CompilerParams(dimension_semantics=("arbitrary",) * n_grid,
                                vmem_limit_bytes=VMEM_LIMIT)


def _normalize(x):
    mu = jnp.mean(x, axis=-1, keepdims=True)
    xc = x - mu
    var = jnp.mean(xc * xc, axis=-1, keepdims=True)
    return xc * lax.rsqrt(var + LN_EPS)


def _gelu(x):
    return 0.5 * x * (1.0 + lax.erf(x * (2.0 ** -0.5)))


def _ada_kernel(c_ref, w_ref, b_ref, o_ref):
    c = c_ref[...]
    s = c * jax.nn.sigmoid(c)
    o_ref[...] = jnp.dot(s, w_ref[...], preferred_element_type=F32) + b_ref[...]


def _ada(c, w_ada, b_ada):
    rows, d = c.shape
    n = w_ada.shape[1]
    tn = 1024
    return pl.pallas_call(
        _ada_kernel,
        grid=(n // tn,),
        in_specs=[pl.BlockSpec((rows, d), lambda j: (0, 0)),
                  pl.BlockSpec((d, tn), lambda j: (0, j)),
                  pl.BlockSpec((1, tn), lambda j: (0, j))],
        out_specs=pl.BlockSpec((rows, tn), lambda j: (0, j)),
        out_shape=jax.ShapeDtypeStruct((rows, n), F32),
        compiler_params=_params(1),
        name="ada",
    )(c, w_ada, b_ada.reshape(1, n))


def _proj_kernel(x_ref, sh_ref, sc_ref, w_ref, o_ref, h_ref):
    @pl.when(pl.program_id(2) == 0)
    def _():
        h = _normalize(x_ref[...]) * (1.0 + sc_ref[...]) + sh_ref[...]
        h_ref[...] = h.astype(BF16)

    o_ref[...] = jnp.dot(h_ref[...], w_ref[...], preferred_element_type=F32)


def _mod_spec(mod, tm, d):
    if mod.shape[1] == 1:
        return pl.BlockSpec((None, 1, d), lambda b, i, *_: (b, 0, 0))
    return pl.BlockSpec((None, tm, d), lambda b, i, *_: (b, i, 0))


def _proj(x, shift, scale, w_in_bf16, tm):
    bsz, t, d = x.shape
    n = w_in_bf16.shape[1]
    tn = 1024
    return pl.pallas_call(
        _proj_kernel,
        grid=(bsz, t // tm, n // tn),
        in_specs=[pl.BlockSpec((None, tm, d), lambda b, i, j: (b, i, 0)),
                  _mod_spec(shift, tm, d), _mod_spec(scale, tm, d),
                  pl.BlockSpec((d, tn), lambda b, i, j: (0, j))],
        out_specs=pl.BlockSpec((None, tm, tn), lambda b, i, j: (b, i, j)),
        out_shape=jax.ShapeDtypeStruct((bsz, t, n), F32),
        scratch_shapes=[pltpu.VMEM((tm, d), BF16)],
        compiler_params=_params(3),
        name="proj",
    )(x, shift, scale, w_in_bf16)


def _rnn_kernel(xr_ref, yr_ref, ga_ref, cbuf_ref, h0_ref, cw_ref, cb_ref, wra_ref, bra_ref,
                wri_ref, bri_ref, lam_ref, pa_ref, cst_ref, hst_ref,
                xbuf, a_s, u_s, hcar, *, tt, tc):
    t = pl.program_id(2)
    nt = pl.num_programs(2)
    pad = SUBLANES
    keep = CONV_W - 1

    @pl.when(t == 0)
    def _():
        xbuf[pad - keep:pad, :] = cbuf_ref[...]
        hcar[...] = h0_ref[...]

    xbuf[pad:pad + tt, :] = xr_ref[...]
    cw = cw_ref[...]
    xc = cb_ref[...] + cw[0:1, :] * xbuf[pad - keep:pad - keep + tt, :]
    for j in range(1, CONV_W):
        xc = xc + cw[j:j + 1, :] * xbuf[pad - keep + j:pad - keep + j + tt, :]
    tail = xbuf[pad + tt - keep:pad + tt, :]
    xbuf[pad - keep:pad, :] = tail

    lam = lam_ref[...]
    softplus_neg = jnp.maximum(-lam, 0.0) + jnp.log1p(jnp.exp(-jnp.abs(lam)))
    for blk in range(tc // RNN_BLOCK_DIM):
        cols = slice(blk * RNN_BLOCK_DIM, (blk + 1) * RNN_BLOCK_DIM)
        xb = xc[:, cols]
        xb16 = xb.astype(BF16)
        r = jax.nn.sigmoid(jnp.dot(xb16, wra_ref[blk], preferred_element_type=F32) + bra_ref[:, cols])
        i = jax.nn.sigmoid(jnp.dot(xb16, wri_ref[blk], preferred_element_type=F32) + bri_ref[:, cols])
        log_a = -RG_C * r * softplus_neg[:, cols]
        a = jnp.exp(log_a)
        u = jnp.sqrt(jnp.tanh(-log_a) * (1.0 + a * a)) * (i * xb)
        a_s[:, cols] = a
        u_s[:, cols] = u

    row = lax.broadcasted_iota(jnp.int32, (SUBLANES, tc), 0)

    def scan_group(g, h_prev):
        rows = pl.ds(pl.multiple_of(g * SUBLANES, SUBLANES), SUBLANES)
        a = a_s[rows, :]
        u = u_s[rows, :]
        for s in (1, 2, 4):
            a_sh = pltpu.roll(a, s, 0)
            u_sh = pltpu.roll(u, s, 0)
            live = row >= s
            u = jnp.where(live, a * u_sh + u, u)
            a = jnp.where(live, a * a_sh, a)
        h = a * h_prev + u
        u_s[rows, :] = h
        return h[SUBLANES - 1:SUBLANES, :]

    h_last = lax.fori_loop(0, tt // SUBLANES, scan_group, hcar[...])
    hcar[...] = h_last

    pa_ref[...] = u_s[...] * _gelu(yr_ref[...]) * jax.nn.sigmoid(ga_ref[...])

    @pl.when(t == nt - 1)
    def _():
        cst_ref[...] = tail
        hst_ref[...] = h_last


def _rnn(proj, conv_buf, h0, conv_w, conv_b, w_ra16, b_ra, w_ri16, b_ri, lam, d, tt, tc):
    bsz, t, _ = proj.shape
    nc = d // tc
    nb = tc // RNN_BLOCK_DIM
    keep = CONV_W - 1
    col = lambda off: (lambda b, c, i: (b, i, off * nc + c))
    vec = pl.BlockSpec((1, tc), lambda b, c, i: (0, c))
    blk = pl.BlockSpec((nb, RNN_BLOCK_DIM, RNN_BLOCK_DIM), lambda b, c, i: (c, 0, 0))
    pa, cst, hst = pl.pallas_call(
        functools.partial(_rnn_kernel, tt=tt, tc=tc),
        grid=(bsz, nc, t // tt),
        in_specs=[pl.BlockSpec((None, tt, tc), col(0)),
                  pl.BlockSpec((None, tt, tc), col(1)),
                  pl.BlockSpec((None, tt, tc), col(5)),
                  pl.BlockSpec((None, keep, tc), lambda b, c, i: (b, 0, c)),
                  pl.BlockSpec((None, 1, tc), lambda b, c, i: (b, 0, c)),
                  pl.BlockSpec((CONV_W, tc), lambda b, c, i: (0, c)),
                  vec, blk, vec, blk, vec, vec],
        out_specs=[pl.BlockSpec((None, tt, tc), lambda b, c, i: (b, i, c)),
                   pl.BlockSpec((None, keep, tc), lambda b, c, i: (b, 0, c)),
                   pl.BlockSpec((None, 1, tc), lambda b, c, i: (b, 0, c))],
        out_shape=[jax.ShapeDtypeStruct((bsz, t, d), F32),
                   jax.ShapeDtypeStruct((bsz, keep, d), F32),
                   jax.ShapeDtypeStruct((bsz, 1, d), F32)],
        scratch_shapes=[pltpu.VMEM((tt + SUBLANES, tc), F32),
                        pltpu.VMEM((tt, tc), F32),
                        pltpu.VMEM((tt, tc), F32),
                        pltpu.VMEM((1, tc), F32)],
        compiler_params=_params(3),
        name="rnn",
    )(proj, proj, proj, conv_buf, h0.reshape(bsz, 1, d), conv_w, conv_b.reshape(1, d),
      w_ra16, b_ra.reshape(1, d), w_ri16, b_ri.reshape(1, d), lam.reshape(1, d))
    return pa, cst, hst.reshape(bsz, d)


def _rel_bias_tile(rel_ref, head, dist, visible):
    idx = jnp.clip(dist, -(CHUNK - 1), MAX_REL) + (CHUNK - 1)

    def body(j, acc):
        return acc + jnp.where(idx == j, rel_ref[head, j], 0.0)

    bias = lax.fori_loop(0, REL_BUCKETS, body, jnp.zeros(dist.shape, F32))
    return jnp.where(visible, bias, NEG_INF)


def _softmax_pv(s, v16):
    m = jnp.max(s, axis=-1, keepdims=True)
    p = jnp.exp(s - m)
    l = jnp.sum(p, axis=-1, keepdims=True)
    return jnp.dot(p.astype(BF16), v16, preferred_element_type=F32) / l


_QSUB = 2 * CHUNK
_BAND = _QSUB + ATT_REACH


_ATTN_HEADS = 4


def _attn_prompt_kernel(rel_ref, q_ref, kp_ref, kc_ref, vp_ref, vc_ref, gb_ref, o_ref,
                        bias_s, kcat, vcat):
    group, b, i = pl.program_id(0), pl.program_id(1), pl.program_id(2)
    tq = ATT_REACH

    @pl.when((b == 0) & (i == 0))
    def _():
        r = lax.broadcasted_iota(jnp.int32, (_QSUB, _BAND), 0)
        c = lax.broadcasted_iota(jnp.int32, (_QSUB, _BAND), 1)
        qc = r // CHUNK + LEFT_CHUNKS
        kc = c // CHUNK
        visible = (kc <= qc) & (kc >= qc - LEFT_CHUNKS)
        for hh in range(_ATTN_HEADS):
            bias_s[hh] = _rel_bias_tile(rel_ref, group * _ATTN_HEADS + hh, r + ATT_REACH - c, visible)

    kcat[0:tq, :] = kp_ref[...].astype(BF16)
    kcat[tq:2 * tq, :] = kc_ref[...].astype(BF16)
    vcat[0:tq, :] = vp_ref[...].astype(BF16)
    vcat[tq:2 * tq, :] = vc_ref[...].astype(BF16)
    col = lax.broadcasted_iota(jnp.int32, (_QSUB, _BAND), 1)
    n_sub = tq // _QSUB
    for hh in range(_ATTN_HEADS):
        cols = slice(hh * HEAD_DIM, (hh + 1) * HEAD_DIM)
        scores = []
        for s in range(n_sub):
            q16 = q_ref[s * _QSUB:(s + 1) * _QSUB, cols].astype(BF16)
            sc = lax.dot_general(q16, kcat[s * _QSUB:s * _QSUB + _BAND, cols], (((1,), (1,)), ((), ())),
                                 preferred_element_type=F32) * (HEAD_DIM ** -0.5)
            sc = sc + bias_s[hh]
            k_pos = col + (i * tq + s * _QSUB - ATT_REACH)
            scores.append(jnp.where(k_pos >= 0, sc, NEG_INF))
        maxes = [jnp.max(sc, axis=-1, keepdims=True) for sc in scores]
        probs = [jnp.exp(sc - m) for sc, m in zip(scores, maxes)]
        sums = [jnp.sum(p, axis=-1, keepdims=True) for p in probs]
        outs = [jnp.dot(p.astype(BF16), vcat[s * _QSUB:s * _QSUB + _BAND, cols], preferred_element_type=F32)
                for s, p in enumerate(probs)]
        for s in range(n_sub):
            rows = slice(s * _QSUB, (s + 1) * _QSUB)
            o_ref[rows, cols] = outs[s] / sums[s] * jax.nn.sigmoid(gb_ref[rows, cols])


def _attn_prompt(proj, rel_bias, d):
    bsz, t, _ = proj.shape
    ng = d // (HEAD_DIM * _ATTN_HEADS)
    width = HEAD_DIM * _ATTN_HEADS
    tq = ATT_REACH
    cur = lambda off: (lambda g, b, i: (b, i, off * ng + g))
    prev = lambda off: (lambda g, b, i: (b, jnp.maximum(i - 1, 0), off * ng + g))
    blk = lambda f: pl.BlockSpec((None, tq, width), f)
    return pl.pallas_call(
        _attn_prompt_kernel,
        grid=(ng, bsz, t // tq),
        in_specs=[pl.BlockSpec(memory_space=pltpu.SMEM),
                  blk(cur(2)), blk(prev(3)), blk(cur(3)), blk(prev(4)), blk(cur(4)), blk(cur(6))],
        out_specs=pl.BlockSpec((None, tq, width), lambda g, b, i: (b, i, g)),
        out_shape=jax.ShapeDtypeStruct((bsz, t, d), F32),
        scratch_shapes=[pltpu.VMEM((_ATTN_HEADS, _QSUB, _BAND), F32),
                        pltpu.VMEM((2 * tq, width), BF16),
                        pltpu.VMEM((2 * tq, width), BF16)],
        compiler_params=_params(3),
        name="attn_prompt",
    )(rel_bias, proj, proj, proj, proj, proj, proj)


def _attn_sample_kernel(rel_ref, q_ref, kn_ref, vn_ref, gb_ref, ck_ref, cv_ref, o_ref,
                        bias_past_s, bias_new_s, *, t, n_past):
    group = pl.program_id(0)

    def bias(head, n_keys, first_pos):
        r = lax.broadcasted_iota(jnp.int32, (t, n_keys), 0)
        c = lax.broadcasted_iota(jnp.int32, (t, n_keys), 1)
        q_pos = PAST_LEN + r
        k_pos = first_pos + c
        qc = q_pos // CHUNK
        kc = k_pos // CHUNK
        visible = (k_pos >= 0) & (kc <= qc) & (kc >= qc - LEFT_CHUNKS)
        return _rel_bias_tile(rel_ref, head, q_pos - k_pos, visible)

    @pl.when(pl.program_id(1) == 0)
    def _():
        for hh in range(_ATTN_HEADS):
            bias_past_s[hh] = bias(group * _ATTN_HEADS + hh, n_past, PAST_LEN - n_past)
            bias_new_s[hh] = bias(group * _ATTN_HEADS + hh, t, PAST_LEN)

    scale = HEAD_DIM ** -0.5
    nt = (((1,), (1,)), ((), ()))
    for hh in range(_ATTN_HEADS):
        cols = slice(hh * HEAD_DIM, (hh + 1) * HEAD_DIM)
        q16 = q_ref[:, cols].astype(BF16)
        s_past = lax.dot_general(q16, ck_ref[:, cols].astype(BF16), nt, preferred_element_type=F32) * scale
        s_new = lax.dot_general(q16, kn_ref[:, cols].astype(BF16), nt, preferred_element_type=F32) * scale
        s_past = s_past + bias_past_s[hh]
        s_new = s_new + bias_new_s[hh]
        m = jnp.maximum(jnp.max(s_past, axis=-1, keepdims=True), jnp.max(s_new, axis=-1, keepdims=True))
        p_past = jnp.exp(s_past - m)
        p_new = jnp.exp(s_new - m)
        l = jnp.sum(p_past, axis=-1, keepdims=True) + jnp.sum(p_new, axis=-1, keepdims=True)
        o = (jnp.dot(p_past.astype(BF16), cv_ref[:, cols].astype(BF16), preferred_element_type=F32)
             + jnp.dot(p_new.astype(BF16), vn_ref[:, cols].astype(BF16), preferred_element_type=F32)) / l
        o_ref[:, cols] = o * jax.nn.sigmoid(gb_ref[:, cols])


def _attn_sample(proj, cache_k, cache_v, rel_bias, d):
    bsz, t, _ = proj.shape
    width = HEAD_DIM * _ATTN_HEADS
    ng = d // width
    n_past = cache_k.shape[1]
    new = lambda off: pl.BlockSpec((None, t, width), lambda g, b: (b, 0, off * ng + g))
    past = pl.BlockSpec((None, n_past, width), lambda g, b: (b, 0, g))
    return pl.pallas_call(
        functools.partial(_attn_sample_kernel, t=t, n_past=n_past),
        grid=(ng, bsz),
        in_specs=[pl.BlockSpec(memory_space=pltpu.SMEM), new(2), new(3), new(4), new(6), past, past],
        out_specs=pl.BlockSpec((None, t, width), lambda g, b: (b, 0, g)),
        out_shape=jax.ShapeDtypeStruct((bsz, t, d), F32),
        scratch_shapes=[pltpu.VMEM((_ATTN_HEADS, t, n_past), F32), pltpu.VMEM((_ATTN_HEADS, t, t), F32)],
        compiler_params=_params(2),
        name="attn_sample",
    )(rel_bias, proj, proj, proj, proj, cache_k.reshape(bsz, n_past, d), cache_v.reshape(bsz, n_past, d))


def _mix_kernel(pa_ref, pb_ref, x_ref, w_ref, g1_ref, sh2_ref, sc2_ref, lg_ref, lb_ref,
                x1_ref, h2_ref):
    mixed = jnp.dot((pa_ref[...] + pb_ref[...]).astype(BF16), w_ref[...], preferred_element_type=F32)
    x1 = _normalize(ALPHA * x_ref[...] + g1_ref[...] * mixed) * lg_ref[...] + lb_ref[...]
    x1_ref[...] = x1
    h2_ref[...] = (_normalize(x1) * (1.0 + sc2_ref[...]) + sh2_ref[...]).astype(BF16)


def _mix(pa, pb, x, w_out16, gate1, shift2, scale2, ln_g, ln_b, tm):
    bsz, t, d = x.shape
    tok = pl.BlockSpec((None, tm, d), lambda b, i: (b, i, 0))
    vec = pl.BlockSpec((1, d), lambda b, i: (0, 0))
    return pl.pallas_call(
        _mix_kernel,
        grid=(bsz, t // tm),
        in_specs=[tok, tok, tok, pl.BlockSpec((d, d), lambda b, i: (0, 0)),
                  _mod_spec(gate1, tm, d), _mod_spec(shift2, tm, d), _mod_spec(scale2, tm, d), vec, vec],
        out_specs=[tok, tok],
        out_shape=[jax.ShapeDtypeStruct((bsz, t, d), F32), jax.ShapeDtypeStruct((bsz, t, d), BF16)],
        compiler_params=_params(2),
        name="mix",
    )(pa, pb, x, w_out16, gate1, shift2, scale2, ln_g.reshape(1, d), ln_b.reshape(1, d))


def _take_top(s, n_iter, rows_f, on_pick):
    def body(it, s_rem):
        m = jnp.max(s_rem, axis=0, keepdims=True)
        first = jnp.min(jnp.where(s_rem == m, rows_f, float(s.shape[0])), axis=0, keepdims=True)
        pick = rows_f == first
        on_pick(it, m, pick)
        return jnp.where(pick, -jnp.inf, s_rem)

    lax.fori_loop(0, n_iter, body, s)


def _fill_candidates(vals_s, cand_s):
    cand_s[...] = jnp.full(cand_s.shape, -jnp.inf, F32)
    for r1 in range(PEER_TOPK):
        off, cnt = _CAND_OFFSETS[r1], _CAND_COUNTS[r1]
        cand_s[off:off + cnt, :] = vals_s[0, r1:r1 + 1, :] + vals_s[1, 0:cnt, :]


def _route_exact(h, score_s, rank_s, vals_s, cand_s, sel_s, lcol_ref, e1_ref, r2_ref, e2_ref):
    tm = score_s.shape[-1]
    rows_f = lax.broadcasted_iota(jnp.int32, (N_KEYS, tm), 0).astype(F32)
    crow_f = lax.broadcasted_iota(jnp.int32, (_N_CAND_PAD, tm), 0).astype(F32)
    for p in range(2):
        rank_s[p] = jnp.full((N_KEYS, tm), float(PEER_TOPK), F32)

        def on_pick(it, m, pick, p=p):
            vals_s[p, pl.ds(it, 1), :] = m
            rank_s[p] = jnp.where(pick, jnp.asarray(it, F32), rank_s[p])

        _take_top(score_s[p], PEER_TOPK, rows_f, on_pick)

    _fill_candidates(vals_s, cand_s)
    sel_s[...] = jnp.zeros((_N_CAND_PAD, tm), F32)

    def on_pick_cand(it, m, pick):
        sel_s[...] = jnp.where(pick, 1.0, sel_s[...])

    _take_top(cand_s[...], PEER_TOPK, crow_f, on_pick_cand)

    sel = sel_s[...]
    top = vals_s[0, 0:1, :] + vals_s[1, 0:1, :]
    z = jnp.sum(jnp.where(sel > 0.0, jnp.exp(cand_s[...] - top), 0.0), axis=0, keepdims=True)
    rank1 = rank_s[0]
    lcol = jnp.zeros((N_KEYS, tm), F32)
    for r1 in range(PEER_TOPK):
        off, cnt = _CAND_OFFSETS[r1], _CAND_COUNTS[r1]
        n_sel = jnp.sum(sel[off:off + cnt, :], axis=0, keepdims=True)
        lcol = lcol + jnp.where(rank1 == float(r1), n_sel, 0.0)
    lcol_ref[h] = lcol
    e1_ref[h] = jnp.exp(score_s[0] - vals_s[0, 0:1, :]) / z
    r2_ref[h] = rank_s[1].astype(BF16)
    e2_ref[h] = jnp.exp(score_s[1] - vals_s[1, 0:1, :]).astype(BF16)


def _compare_exchange(xs, i, l, descending):
    hi, lo = jnp.maximum(xs[i], xs[l]), jnp.minimum(xs[i], xs[l])
    xs[i], xs[l] = (hi, lo) if descending else (lo, hi)


def _bitonic_merge(xs):
    xs = list(xs)
    j = len(xs) // 2
    while j >= 1:
        for i in range(len(xs)):
            if i & j == 0:
                _compare_exchange(xs, i, i | j, True)
        j //= 2
    return xs


def _bitonic_sort(xs):
    xs = list(xs)
    k = 2
    while k <= len(xs):
        j = k // 2
        while j >= 1:
            for i in range(len(xs)):
                if i & j == 0:
                    _compare_exchange(xs, i, i | j, i & k == 0)
            j //= 2
        k *= 2
    return xs


def _sorted_top(pieces):
    xs = _bitonic_sort(pieces)
    for shift in (1, 2, 4):
        other = [pltpu.roll(x, shift, 0) for x in xs][::-1]
        if len(xs) < PEER_TOPK:
            xs = _bitonic_merge(xs + other)
        else:
            xs = _bitonic_merge([jnp.maximum(a, b) for a, b in zip(xs, other)])
    return xs


def _sublane_sum(x):
    for shift in (4, 2, 1):
        x = x + pltpu.roll(x, shift, 0)
    return x


def _route_fast(h, score_s, vals_s, cand_s, lcol_ref, e1_ref, r2_ref, e2_ref):
    tm = score_s.shape[-1]
    n_slab = N_KEYS // SUBLANES
    slabs = [[score_s[p, v * SUBLANES:(v + 1) * SUBLANES, :] for v in range(n_slab)] for p in range(2)]
    tops = [_sorted_top(slabs[p]) for p in range(2)]
    for p in range(2):
        for j in range(PEER_TOPK):
            vals_s[p, j:j + 1, :] = tops[p][j][0:1, :]
    _fill_candidates(vals_s, cand_s)
    cands = [cand_s[k * SUBLANES:(k + 1) * SUBLANES, :] for k in range(_N_CAND_PAD // SUBLANES)]
    pad = [jnp.full((SUBLANES, tm), -jnp.inf, F32)] * (SUBLANES - len(cands))
    tau = _sorted_top(cands + pad)[PEER_TOPK - 1]
    top = tops[0][0] + tops[1][0]
    n_picked = jnp.zeros((SUBLANES, tm), F32)
    z = jnp.zeros((SUBLANES, tm), F32)
    for c in cands:
        n_picked = n_picked + jnp.where(c >= tau, 1.0, 0.0)
        z = z + jnp.where(c >= tau, jnp.exp(c - top), 0.0)
    inv_z = 1.0 / _sublane_sum(z)
    tie = jnp.where(_sublane_sum(n_picked) != float(PEER_TOPK), 1.0, 0.0)
    for p in range(2):
        n_ge = jnp.zeros((SUBLANES, tm), F32)
        for s in slabs[p]:
            n_ge = n_ge + jnp.where(s >= tops[p][PEER_TOPK - 1], 1.0, 0.0)
        tie = jnp.maximum(tie, jnp.where(_sublane_sum(n_ge) != float(PEER_TOPK), 1.0, 0.0))
        for j in range(PEER_TOPK - 1):
            tie = jnp.maximum(tie, jnp.where(tops[p][j] == tops[p][j + 1], 1.0, 0.0))
    n_pick = []
    for r1 in range(PEER_TOPK):
        n = jnp.zeros((SUBLANES, tm), F32)
        for r2 in range(_CAND_COUNTS[r1]):
            n = n + jnp.where(tops[0][r1] + tops[1][r2] >= tau, 1.0, 0.0)
        n_pick.append(n)
    rank2, e2 = [], []
    for v in range(n_slab):
        rows = slice(v * SUBLANES, (v + 1) * SUBLANES)
        lc = jnp.zeros((SUBLANES, tm), F32)
        rk = jnp.full((SUBLANES, tm), float(PEER_TOPK), F32)
        for j in range(PEER_TOPK):
            lc = jnp.where(slabs[0][v] == tops[0][j], n_pick[j], lc)
            rk = jnp.where(slabs[1][v] == tops[1][j], float(j), rk)
        lcol_ref[h, rows, :] = lc
        e1_ref[h, rows, :] = jnp.exp(slabs[0][v] - tops[0][0]) * inv_z
        rank2.append(rk)
        e2.append(jnp.exp(slabs[1][v] - tops[1][0]))
    for u in range(n_slab // 2):
        rows = slice(2 * u * SUBLANES, (2 * u + 2) * SUBLANES)
        r2_ref[h, rows, :] = jnp.concatenate(rank2[2 * u:2 * u + 2], axis=0).astype(BF16)
        e2_ref[h, rows, :] = jnp.concatenate(e2[2 * u:2 * u + 2], axis=0).astype(BF16)
    return jnp.max(tie)


def _route_kernel(h2_ref, wpq_ref, keys_ref, lcol_ref, e1_ref, r2_ref, e2_ref,
                  q_s, score_s, rank_s, vals_s, cand_s, sel_s):
    q = jnp.dot(h2_ref[...], wpq_ref[...], preferred_element_type=F32)
    for hp in range(2 * PEER_HEADS):
        q_s[hp] = q[:, hp * N_KEYS:(hp + 1) * N_KEYS].astype(BF16)
    nt = (((1,), (1,)), ((), ()))

    def per_head(h, carry):
        for p in range(2):
            score_s[p] = lax.dot_general(keys_ref[h * 2 + p], q_s[h * 2 + p], nt,
                                         preferred_element_type=F32)
        tie = _route_fast(h, score_s, vals_s, cand_s, lcol_ref, e1_ref, r2_ref, e2_ref)

        @pl.when(tie > 0.0)
        def _():
            _route_exact(h, score_s, rank_s, vals_s, cand_s, sel_s, lcol_ref, e1_ref, r2_ref, e2_ref)

        return carry

    lax.fori_loop(0, PEER_HEADS, per_head, 0)


def _route(h2, w_pq16, keys16, tm):
    bsz, t, d = h2.shape
    nq = w_pq16.shape[1]
    out = pl.BlockSpec((None, PEER_HEADS, N_KEYS, tm), lambda b, i: (b, 0, 0, i))
    shape = jax.ShapeDtypeStruct((bsz, PEER_HEADS, N_KEYS, t), F32)
    shape16 = jax.ShapeDtypeStruct((bsz, PEER_HEADS, N_KEYS, t), BF16)
    return pl.pallas_call(
        _route_kernel,
        grid=(bsz, t // tm),
        in_specs=[pl.BlockSpec((None, tm, d), lambda b, i: (b, i, 0)),
                  pl.BlockSpec((d, nq), lambda b, i: (0, 0)),
                  pl.BlockSpec((2 * PEER_HEADS, N_KEYS, nq // (2 * PEER_HEADS)), lambda b, i: (0, 0, 0))],
        out_specs=[out, out, out, out],
        out_shape=[shape, shape, shape16, shape16],
        scratch_shapes=[pltpu.VMEM((2 * PEER_HEADS, tm, nq // (2 * PEER_HEADS)), BF16),
                        pltpu.VMEM((2, N_KEYS, tm), F32),
                        pltpu.VMEM((2, N_KEYS, tm), F32),
                        pltpu.VMEM((2, PEER_TOPK, tm), F32),
                        pltpu.VMEM((_N_CAND_PAD, tm), F32),
                        pltpu.VMEM((_N_CAND_PAD, tm), F32)],
        compiler_params=_params(2),
        name="route",
    )(h2, w_pq16, keys16)


_ACT_ROWS = 256


def _peer_kernel(h2_ref, u_lo_ref, u_hi_ref, vt_lo_ref, vt_hi_ref, lcol_ref, e1_ref, r2_ref, e2_ref,
                 x1_ref, g2_ref, lg_ref, lb_ref, y_ref, acc, p_s, *, te):
    j = pl.program_id(2)

    @pl.when(j == 0)
    def _():
        acc[...] = jnp.zeros(acc.shape, F32)

    zero = jnp.zeros((), BF16)
    n_sub = _ACT_ROWS // N_KEYS
    half_chunks = te // _ACT_ROWS // 2
    half_d = acc.shape[0] // 2
    for c in range(te // _ACT_ROWS):
        u_ref = u_lo_ref if c < half_chunks else u_hi_ref
        chunk = slice((c % half_chunks) * _ACT_ROWS, (c % half_chunks + 1) * _ACT_ROWS)
        act = lax.dot_general(u_ref[chunk, :], h2_ref[...], (((1,), (1,)), ((), ())),
                              preferred_element_type=F32)
        for sub in range(n_sub):
            al = c * n_sub + sub
            w = None
            for h in range(PEER_HEADS):
                keep = r2_ref[h] < lcol_ref[h, al:al + 1, :].astype(BF16)
                term = jnp.where(keep, e2_ref[h] * e1_ref[h, al:al + 1, :].astype(BF16), zero)
                w = term if w is None else w + term
            g = _gelu(act[sub * N_KEYS:(sub + 1) * N_KEYS, :])
            p_s[al * N_KEYS:(al + 1) * N_KEYS, :] = (w.astype(F32) * g).astype(BF16)
    acc[0:half_d, :] += jnp.dot(vt_lo_ref[...], p_s[...], preferred_element_type=F32)
    acc[half_d:, :] += jnp.dot(vt_hi_ref[...], p_s[...], preferred_element_type=F32)

    @pl.when(j == pl.num_programs(2) - 1)
    def _():
        ffn = acc[...].T
        y = _normalize(ALPHA * x1_ref[...] + g2_ref[...] * ffn)
        y_ref[...] = y * lg_ref[...] + lb_ref[...]


def _peer(h2, u16, vt16, lcol, e1, r2, e2, x1, gate2, ln_g, ln_b, tm, te):
    bsz, t, d = h2.shape
    n_exp = u16.shape[0]
    na = te // N_KEYS
    tok = pl.BlockSpec((None, tm, d), lambda b, i, j: (b, i, 0))
    tok1 = pl.BlockSpec((None, tm, d), lambda b, i, j: (b, i, 0), pipeline_mode=pl.Buffered(1))
    vec = pl.BlockSpec((1, d), lambda b, i, j: (0, 0))
    by_a = pl.BlockSpec((None, PEER_HEADS, na, tm), lambda b, i, j: (b, 0, j, i))
    by_b = pl.BlockSpec((None, PEER_HEADS, N_KEYS, tm), lambda b, i, j: (b, 0, 0, i))
    return pl.pallas_call(
        functools.partial(_peer_kernel, te=te),
        grid=(bsz, t // tm, n_exp // te),
        in_specs=[tok,
                  pl.BlockSpec((te // 2, d), lambda b, i, j: (2 * j, 0)),
                  pl.BlockSpec((te // 2, d), lambda b, i, j: (2 * j + 1, 0)),
                  pl.BlockSpec((d // 2, te), lambda b, i, j: (0, j)),
                  pl.BlockSpec((d // 2, te), lambda b, i, j: (1, j)),
                  by_a, by_a, by_b, by_b, tok1, _mod_spec(gate2, tm, d), vec, vec],
        out_specs=tok1,
        out_shape=jax.ShapeDtypeStruct((bsz, t, d), F32),
        scratch_shapes=[pltpu.VMEM((d, tm), F32), pltpu.VMEM((te, tm), BF16)],
        compiler_params=_params(3),
        name="peer",
    )(h2, u16, u16, vt16, vt16, lcol, e1, r2, e2, x1, gate2, ln_g.reshape(1, d), ln_b.reshape(1, d))


def _tiles(n_tok, t):
    return dict(tm_proj=min(1024, n_tok),
                tm_mix=min(256, n_tok),
                tt=min(256, t), tc=512,
                tm_route=min(512, n_tok),
                tm_peer=min(512, n_tok), te=1024)


def _layer(x, mods, conv_buf, h0, attend, w, *, keep_rows, flatten):
    bsz, t, d = x.shape
    tl = _tiles(bsz * t if flatten else t, t)
    if flatten:
        mods = [jnp.broadcast_to(m, (bsz, t, d)).reshape(1, bsz * t, d) for m in mods]
        flat = lambda a: a.reshape(1, bsz * t, a.shape[-1])
    else:
        flat = lambda a: a
    shift1, scale1, gate1, shift2, scale2, gate2 = mods
    proj = _proj(flat(x), shift1, scale1, w["w_in"], tl["tm_proj"]).reshape(bsz, t, -1)
    pa, conv_state, rnn_state = _rnn(proj, conv_buf, h0, w["conv_w"], w["conv_b"], w["w_ra"], w["b_ra"],
                                     w["w_ri"], w["b_ri"], w["lam"], d, tl["tt"], tl["tc"])
    pb = attend(proj)
    x1, h2 = _mix(flat(pa), flat(pb), flat(x), w["w_out"], gate1, shift2, scale2, w["ln1_g"], w["ln1_b"],
                  tl["tm_mix"])
    lcol, e1, r2, e2 = _route(h2, w["w_pq"], w["keys"], tl["tm_route"])
    y = _peer(h2, w["peer_u"], w["peer_vt"], lcol, e1, r2, e2, x1, gate2, w["ln2_g"], w["ln2_b"],
              tl["tm_peer"], tl["te"])
    n_heads = d // HEAD_DIM
    k = proj[:, t - keep_rows:, 3 * d:4 * d].reshape(bsz, keep_rows, n_heads, HEAD_DIM)
    v = proj[:, t - keep_rows:, 4 * d:5 * d].reshape(bsz, keep_rows, n_heads, HEAD_DIM)
    return y.reshape(bsz, t, d), conv_state, rnn_state, k, v


def kernel(x_prompt, x_sample, c_prompt, c_sample, state_conv, state_rnn, cache_k, cache_v, w_ada, b_ada, w_in, conv_w, conv_b, w_ra, b_ra, w_ri, b_ri, rg_lambda, rel_bias, w_out, ln1_g, ln1_b, w_pq, peer_keys, peer_u, peer_v, ln2_g, ln2_b):
    bsz_p, t_p, d = x_prompt.shape
    bsz_s, t_s, _ = x_sample.shape
    depth = w_ada.shape[0]
    keep = min(ATT_REACH, t_p)
    y_p, y_s = x_prompt, x_sample
    outs = [[] for _ in range(8)]
    rows = bsz_p + bsz_s
    rows_pad = -(-rows // SUBLANES) * SUBLANES
    c_all = jnp.concatenate([c_prompt, c_sample, jnp.zeros((rows_pad - rows, d), F32)], axis=0)
    for l in range(depth):
        w = dict(w_in=w_in[l].astype(BF16), conv_w=conv_w[l], conv_b=conv_b[l],
                 w_ra=w_ra[l].astype(BF16), b_ra=b_ra[l], w_ri=w_ri[l].astype(BF16), b_ri=b_ri[l],
                 lam=rg_lambda[l], w_out=w_out[l].astype(BF16), ln1_g=ln1_g[l], ln1_b=ln1_b[l],
                 w_pq=w_pq[l].astype(BF16),
                 keys=peer_keys[l].astype(BF16).reshape(2 * PEER_HEADS, N_KEYS, -1),
                 peer_u=peer_u[l].astype(BF16), peer_vt=peer_v[l].astype(BF16).T,
                 ln2_g=ln2_g[l], ln2_b=ln2_b[l])
        mod = _ada(c_all, w_ada[l], b_ada[l])
        mods_p = [m[:, None, :] for m in jnp.split(mod[:bsz_p], 6, axis=-1)]
        mods_s = [m[:, None, :] for m in jnp.split(mod[bsz_p:rows], 6, axis=-1)]

        y_p, cs, hs, k, v = _layer(
            y_p, mods_p, jnp.zeros((bsz_p, CONV_W - 1, d), F32), jnp.zeros((bsz_p, d), F32),
            lambda proj: _attn_prompt(proj, rel_bias[l], d), w,
            keep_rows=keep, flatten=False)
        for dst, val in zip(outs[:4], (cs, hs, k, v)):
            dst.append(val)

        y_s, cs, hs, k, v = _layer(
            y_s, mods_s, state_conv[l], state_rnn[l],
            lambda proj: _attn_sample(proj, cache_k[l], cache_v[l], rel_bias[l], d), w,
            keep_rows=t_s, flatten=True)
        for dst, val in zip(outs[4:], (cs, hs, k, v)):
            dst.append(val)
    return (y_p, y_s) + tuple(jnp.stack(o) for o in outs)
```

```python
import functools

import jax
import jax.numpy as jnp
from jax import lax
from jax.experimental import pallas as pl
from jax.experimental.pallas import tpu as pltpu

F32 = jnp.float32
BF16 = jnp.bfloat16

CHUNK = 64
LEFT_CHUNKS = 8
ATT_REACH = LEFT_CHUNKS * CHUNK
MAX_REL = 128
REL_BUCKETS = MAX_REL + CHUNK
PAST_LEN = 1024
CONV_W = 4
RG_C = 8.0
RNN_BLOCK_DIM = 128
HEAD_DIM = 128
N_KEYS = 128
PEER_HEADS = 8
PEER_TOPK = 16
DEPTH = 1
ALPHA = (2.0 * DEPTH) ** 0.25
LN_EPS = 1e-5
NEG_INF = -1e30

LANES = 128
SUBLANES = 8
VMEM_LIMIT = 56 * 1024 * 1024

_CAND_COUNTS = tuple(PEER_TOPK // (r1 + 1) for r1 in range(PEER_TOPK))
_CAND_OFFSETS = tuple(sum(_CAND_COUNTS[:r1]) for r1 in range(PEER_TOPK))
_N_CAND = sum(_CAND_COUNTS)
_N_CAND_PAD = -(-_N_CAND // SUBLANES) * SUBLANES


def _params(n_grid):
    return pltpu.CompilerParams(dimension_semantics=("arbitrary",) * n_grid,
                                vmem_limit_bytes=VMEM_LIMIT)


def _normalize(x):
    mu = jnp.mean(x, axis=-1, keepdims=True)
    xc = x - mu
    var = jnp.mean(xc * xc, axis=-1, keepdims=True)
    return xc * lax.rsqrt(var + LN_EPS)


def _gelu(x):
    return 0.5 * x * (1.0 + lax.erf(x * (2.0 ** -0.5)))


def _ada_kernel(c_ref, w_ref, b_ref, o_ref):
    c = c_ref[...]
    s = c * jax.nn.sigmoid(c)
    o_ref[...] = jnp.dot(s, w_ref[...], preferred_element_type=F32) + b_ref[...]


def _ada(c, w_ada, b_ada):
    rows, d = c.shape
    n = w_ada.shape[1]
    tn = 1024
    return pl.pallas_call(
        _ada_kernel,
        grid=(n // tn,),
        in_specs=[pl.BlockSpec((rows, d), lambda j: (0, 0)),
                  pl.BlockSpec((d, tn), lambda j: (0, j)),
                  pl.BlockSpec((1, tn), lambda j: (0, j))],
        out_specs=pl.BlockSpec((rows, tn), lambda j: (0, j)),
        out_shape=jax.ShapeDtypeStruct((rows, n), F32),
        compiler_params=_params(1),
        name="ada",
    )(c, w_ada, b_ada.reshape(1, n))


def _proj_kernel(x_ref, sh_ref, sc_ref, w_ref, o_ref, h_ref):
    @pl.when(pl.program_id(2) == 0)
    def _():
        h = _normalize(x_ref[...]) * (1.0 + sc_ref[...]) + sh_ref[...]
        h_ref[...] = h.astype(BF16)

    o_ref[...] = jnp.dot(h_ref[...], w_ref[...], preferred_element_type=F32)


def _mod_spec(mod, tm, d):
    if mod.shape[1] == 1:
        return pl.BlockSpec((None, 1, d), lambda b, i, *_: (b, 0, 0))
    return pl.BlockSpec((None, tm, d), lambda b, i, *_: (b, i, 0))


def _proj(x, shift, scale, w_in_bf16, tm):
    bsz, t, d = x.shape
    n = w_in_bf16.shape[1]
    tn = 1024
    return pl.pallas_call(
        _proj_kernel,
        grid=(bsz, t // tm, n // tn),
        in_specs=[pl.BlockSpec((None, tm, d), lambda b, i, j: (b, i, 0)),
                  _mod_spec(shift, tm, d), _mod_spec(scale, tm, d),
                  pl.BlockSpec((d, tn), lambda b, i, j: (0, j))],
        out_specs=pl.BlockSpec((None, tm, tn), lambda b, i, j: (b, i, j)),
        out_shape=jax.ShapeDtypeStruct((bsz, t, n), F32),
        scratch_shapes=[pltpu.VMEM((tm, d), BF16)],
        compiler_params=_params(3),
        name="proj",
    )(x, shift, scale, w_in_bf16)


def _rnn_kernel(xr_ref, yr_ref, ga_ref, cbuf_ref, h0_ref, cw_ref, cb_ref, wra_ref, bra_ref,
                wri_ref, bri_ref, lam_ref, pa_ref, cst_ref, hst_ref,
                xbuf, a_s, u_s, hcar, *, tt, tc):
    t = pl.program_id(2)
    nt = pl.num_programs(2)
    pad = SUBLANES
    keep = CONV_W - 1

    @pl.when(t == 0)
    def _():
        xbuf[pad - keep:pad, :] = cbuf_ref[...]
        hcar[...] = h0_ref[...]

    xbuf[pad:pad + tt, :] = xr_ref[...]
    cw = cw_ref[...]
    xc = cb_ref[...] + cw[0:1, :] * xbuf[pad - keep:pad - keep + tt, :]
    for j in range(1, CONV_W):
        xc = xc + cw[j:j + 1, :] * xbuf[pad - keep + j:pad - keep + j + tt, :]
    tail = xbuf[pad + tt - keep:pad + tt, :]
    xbuf[pad - keep:pad, :] = tail

    lam = lam_ref[...]
    softplus_neg = jnp.maximum(-lam, 0.0) + jnp.log1p(jnp.exp(-jnp.abs(lam)))
    for blk in range(tc // RNN_BLOCK_DIM):
        cols = slice(blk * RNN_BLOCK_DIM, (blk + 1) * RNN_BLOCK_DIM)
        xb = xc[:, cols]
        xb16 = xb.astype(BF16)
        r = jax.nn.sigmoid(jnp.dot(xb16, wra_ref[blk], preferred_element_type=F32) + bra_ref[:, cols])
        i = jax.nn.sigmoid(jnp.dot(xb16, wri_ref[blk], preferred_element_type=F32) + bri_ref[:, cols])
        log_a = -RG_C * r * softplus_neg[:, cols]
        a = jnp.exp(log_a)
        u = jnp.sqrt(jnp.tanh(-log_a) * (1.0 + a * a)) * (i * xb)
        a_s[:, cols] = a
        u_s[:, cols] = u

    row = lax.broadcasted_iota(jnp.int32, (SUBLANES, tc), 0)

    def scan_group(g, h_prev):
        rows = pl.ds(pl.multiple_of(g * SUBLANES, SUBLANES), SUBLANES)
        a = a_s[rows, :]
        u = u_s[rows, :]
        for s in (1, 2, 4):
            a_sh = pltpu.roll(a, s, 0)
            u_sh = pltpu.roll(u, s, 0)
            live = row >= s
            u = jnp.where(live, a * u_sh + u, u)
            a = jnp.where(live, a * a_sh, a)
        h = a * h_prev + u
        u_s[rows, :] = h
        return h[SUBLANES - 1:SUBLANES, :]

    h_last = lax.fori_loop(0, tt // SUBLANES, scan_group, hcar[...])
    hcar[...] = h_last

    pa_ref[...] = u_s[...] * _gelu(yr_ref[...]) * jax.nn.sigmoid(ga_ref[...])

    @pl.when(t == nt - 1)
    def _():
        cst_ref[...] = tail
        hst_ref[...] = h_last


def _rnn(proj, conv_buf, h0, conv_w, conv_b, w_ra16, b_ra, w_ri16, b_ri, lam, d, tt, tc):
    bsz, t, _ = proj.shape
    nc = d // tc
    nb = tc // RNN_BLOCK_DIM
    keep = CONV_W - 1
    col = lambda off: (lambda b, c, i: (b, i, off * nc + c))
    vec = pl.BlockSpec((1, tc), lambda b, c, i: (0, c))
    blk = pl.BlockSpec((nb, RNN_BLOCK_DIM, RNN_BLOCK_DIM), lambda b, c, i: (c, 0, 0))
    pa, cst, hst = pl.pallas_call(
        functools.partial(_rnn_kernel, tt=tt, tc=tc),
        grid=(bsz, nc, t // tt),
        in_specs=[pl.BlockSpec((None, tt, tc), col(0)),
                  pl.BlockSpec((None, tt, tc), col(1)),
                  pl.BlockSpec((None, tt, tc), col(5)),
                  pl.BlockSpec((None, keep, tc), lambda b, c, i: (b, 0, c)),
                  pl.BlockSpec((None, 1, tc), lambda b, c, i: (b, 0, c)),
                  pl.BlockSpec((CONV_W, tc), lambda b, c, i: (0, c)),
                  vec, blk, vec, blk, vec, vec],
        out_specs=[pl.BlockSpec((None, tt, tc), lambda b, c, i: (b, i, c)),
                   pl.BlockSpec((None, keep, tc), lambda b, c, i: (b, 0, c)),
                   pl.BlockSpec((None, 1, tc), lambda b, c, i: (b, 0, c))],
        out_shape=[jax.ShapeDtypeStruct((bsz, t, d), F32),
                   jax.ShapeDtypeStruct((bsz, keep, d), F32),
                   jax.ShapeDtypeStruct((bsz, 1, d), F32)],
        scratch_shapes=[pltpu.VMEM((tt + SUBLANES, tc), F32),
                        pltpu.VMEM((tt, tc), F32),
                        pltpu.VMEM((tt, tc), F32),
                        pltpu.VMEM((1, tc), F32)],
        compiler_params=_params(3),
        name="rnn",
    )(proj, proj, proj, conv_buf, h0.reshape(bsz, 1, d), conv_w, conv_b.reshape(1, d),
      w_ra16, b_ra.reshape(1, d), w_ri16, b_ri.reshape(1, d), lam.reshape(1, d))
    return pa, cst, hst.reshape(bsz, d)


def _rel_bias_tile(rel_ref, head, dist, visible):
    idx = jnp.clip(dist, -(CHUNK - 1), MAX_REL) + (CHUNK - 1)

    def body(j, acc):
        return acc + jnp.where(idx == j, rel_ref[head, j], 0.0)

    bias = lax.fori_loop(0, REL_BUCKETS, body, jnp.zeros(dist.shape, F32))
    return jnp.where(visible, bias, NEG_INF)


def _softmax_pv(s, v16):
    m = jnp.max(s, axis=-1, keepdims=True)
    p = jnp.exp(s - m)
    l = jnp.sum(p, axis=-1, keepdims=True)
    return jnp.dot(p.astype(BF16), v16, preferred_element_type=F32) / l


_QSUB = 2 * CHUNK
_BAND = _QSUB + ATT_REACH


_ATTN_HEADS = 8


def _attn_prompt_kernel(rel_ref, q_ref, kp_ref, kc_ref, vp_ref, vc_ref, gb_ref, o_ref,
                        bias_s, kcat, vcat):
    group, b, i = pl.program_id(0), pl.program_id(1), pl.program_id(2)
    tq = ATT_REACH

    @pl.when((b == 0) & (i == 0))
    def _():
        r = lax.broadcasted_iota(jnp.int32, (_QSUB, _BAND), 0)
        c = lax.broadcasted_iota(jnp.int32, (_QSUB, _BAND), 1)
        qc = r // CHUNK + LEFT_CHUNKS
        kc = c // CHUNK
        visible = (kc <= qc) & (kc >= qc - LEFT_CHUNKS)
        for hh in range(_ATTN_HEADS):
            bias_s[hh] = _rel_bias_tile(rel_ref, group * _ATTN_HEADS + hh, r + ATT_REACH - c, visible)

    kcat[0:tq, :] = kp_ref[...].astype(BF16)
    kcat[tq:2 * tq, :] = kc_ref[...].astype(BF16)
    vcat[0:tq, :] = vp_ref[...].astype(BF16)
    vcat[tq:2 * tq, :] = vc_ref[...].astype(BF16)
    col = lax.broadcasted_iota(jnp.int32, (_QSUB, _BAND), 1)
    n_sub = tq // _QSUB
    for hh in range(_ATTN_HEADS):
        cols = slice(hh * HEAD_DIM, (hh + 1) * HEAD_DIM)
        scores = []
        for s in range(n_sub):
            q16 = q_ref[s * _QSUB:(s + 1) * _QSUB, cols].astype(BF16)
            sc = lax.dot_general(q16, kcat[s * _QSUB:s * _QSUB + _BAND, cols], (((1,), (1,)), ((), ())),
                                 preferred_element_type=F32) * (HEAD_DIM ** -0.5)
            sc = sc + bias_s[hh]
            k_pos = col + (i * tq + s * _QSUB - ATT_REACH)
            scores.append(jnp.where(k_pos >= 0, sc, NEG_INF))
        maxes = [jnp.max(sc, axis=-1, keepdims=True) for sc in scores]
        probs = [jnp.exp(sc - m) for sc, m in zip(scores, maxes)]
        sums = [jnp.sum(p, axis=-1, keepdims=True) for p in probs]
        outs = [jnp.dot(p.astype(BF16), vcat[s * _QSUB:s * _QSUB + _BAND, cols], preferred_element_type=F32)
                for s, p in enumerate(probs)]
        for s in range(n_sub):
            rows = slice(s * _QSUB, (s + 1) * _QSUB)
            o_ref[rows, cols] = outs[s] / sums[s] * jax.nn.sigmoid(gb_ref[rows, cols])


def _attn_prompt(proj, rel_bias, d):
    bsz, t, _ = proj.shape
    ng = d // (HEAD_DIM * _ATTN_HEADS)
    width = HEAD_DIM * _ATTN_HEADS
    tq = ATT_REACH
    cur = lambda off: (lambda g, b, i: (b, i, off * ng + g))
    prev = lambda off: (lambda g, b, i: (b, jnp.maximum(i - 1, 0), off * ng + g))
    blk = lambda f: pl.BlockSpec((None, tq, width), f)
    return pl.pallas_call(
        _attn_prompt_kernel,
        grid=(ng, bsz, t // tq),
        in_specs=[pl.BlockSpec(memory_space=pltpu.SMEM),
                  blk(cur(2)), blk(prev(3)), blk(cur(3)), blk(prev(4)), blk(cur(4)), blk(cur(6))],
        out_specs=pl.BlockSpec((None, tq, width), lambda g, b, i: (b, i, g)),
        out_shape=jax.ShapeDtypeStruct((bsz, t, d), F32),
        scratch_shapes=[pltpu.VMEM((_ATTN_HEADS, _QSUB, _BAND), F32),
                        pltpu.VMEM((2 * tq, width), BF16),
                        pltpu.VMEM((2 * tq, width), BF16)],
        compiler_params=_params(3),
        name="attn_prompt",
    )(rel_bias, proj, proj, proj, proj, proj, proj)


def _attn_sample_kernel(rel_ref, q_ref, kn_ref, vn_ref, gb_ref, ck_ref, cv_ref, o_ref,
                        bias_past_s, bias_new_s, *, t, n_past):
    group = pl.program_id(0)

    def bias(head, n_keys, first_pos):
        r = lax.broadcasted_iota(jnp.int32, (t, n_keys), 0)
        c = lax.broadcasted_iota(jnp.int32, (t, n_keys), 1)
        q_pos = PAST_LEN + r
        k_pos = first_pos + c
        qc = q_pos // CHUNK
        kc = k_pos // CHUNK
        visible = (k_pos >= 0) & (kc <= qc) & (kc >= qc - LEFT_CHUNKS)
        return _rel_bias_tile(rel_ref, head, q_pos - k_pos, visible)

    @pl.when(pl.program_id(1) == 0)
    def _():
        for hh in range(_ATTN_HEADS):
            bias_past_s[hh] = bias(group * _ATTN_HEADS + hh, n_past, PAST_LEN - n_past)
            bias_new_s[hh] = bias(group * _ATTN_HEADS + hh, t, PAST_LEN)

    scale = HEAD_DIM ** -0.5
    nt = (((1,), (1,)), ((), ()))
    for hh in range(_ATTN_HEADS):
        cols = slice(hh * HEAD_DIM, (hh + 1) * HEAD_DIM)
        q16 = q_ref[:, cols].astype(BF16)
        s_past = lax.dot_general(q16, ck_ref[:, cols].astype(BF16), nt, preferred_element_type=F32) * scale
        s_new = lax.dot_general(q16, kn_ref[:, cols].astype(BF16), nt, preferred_element_type=F32) * scale
        s_past = s_past + bias_past_s[hh]
        s_new = s_new + bias_new_s[hh]
        m = jnp.maximum(jnp.max(s_past, axis=-1, keepdims=True), jnp.max(s_new, axis=-1, keepdims=True))
        p_past = jnp.exp(s_past - m)
        p_new = jnp.exp(s_new - m)
        l = jnp.sum(p_past, axis=-1, keepdims=True) + jnp.sum(p_new, axis=-1, keepdims=True)
        o = (jnp.dot(p_past.astype(BF16), cv_ref[:, cols].astype(BF16), preferred_element_type=F32)
             + jnp.dot(p_new.astype(BF16), vn_ref[:, cols].astype(BF16), preferred_element_type=F32)) / l
        o_ref[:, cols] = o * jax.nn.sigmoid(gb_ref[:, cols])


def _attn_sample(proj, cache_k, cache_v, rel_bias, d):
    bsz, t, _ = proj.shape
    width = HEAD_DIM * _ATTN_HEADS
    ng = d // width
    n_past = cache_k.shape[1]
    new = lambda off: pl.BlockSpec((None, t, width), lambda g, b: (b, 0, off * ng + g))
    past = pl.BlockSpec((None, n_past, width), lambda g, b: (b, 0, g))
    return pl.pallas_call(
        functools.partial(_attn_sample_kernel, t=t, n_past=n_past),
        grid=(ng, bsz),
        in_specs=[pl.BlockSpec(memory_space=pltpu.SMEM), new(2), new(3), new(4), new(6), past, past],
        out_specs=pl.BlockSpec((None, t, width), lambda g, b: (b, 0, g)),
        out_shape=jax.ShapeDtypeStruct((bsz, t, d), F32),
        scratch_shapes=[pltpu.VMEM((_ATTN_HEADS, t, n_past), F32), pltpu.VMEM((_ATTN_HEADS, t, t), F32)],
        compiler_params=_params(2),
        name="attn_sample",
    )(rel_bias, proj, proj, proj, proj, cache_k.reshape(bsz, n_past, d), cache_v.reshape(bsz, n_past, d))


def _mix_kernel(pa_ref, pb_ref, x_ref, w_ref, g1_ref, sh2_ref, sc2_ref, lg_ref, lb_ref,
                x1_ref, h2_ref):
    mixed = jnp.dot((pa_ref[...] + pb_ref[...]).astype(BF16), w_ref[...], preferred_element_type=F32)
    x1 = _normalize(ALPHA * x_ref[...] + g1_ref[...] * mixed) * lg_ref[...] + lb_ref[...]
    x1_ref[...] = x1
    h2_ref[...] = (_normalize(x1) * (1.0 + sc2_ref[...]) + sh2_ref[...]).astype(BF16)


def _mix(pa, pb, x, w_out16, gate1, shift2, scale2, ln_g, ln_b, tm):
    bsz, t, d = x.shape
    tok = pl.BlockSpec((None, tm, d), lambda b, i: (b, i, 0))
    vec = pl.BlockSpec((1, d), lambda b, i: (0, 0))
    return pl.pallas_call(
        _mix_kernel,
        grid=(bsz, t // tm),
        in_specs=[tok, tok, tok, pl.BlockSpec((d, d), lambda b, i: (0, 0)),
                  _mod_spec(gate1, tm, d), _mod_spec(shift2, tm, d), _mod_spec(scale2, tm, d), vec, vec],
        out_specs=[tok, tok],
        out_shape=[jax.ShapeDtypeStruct((bsz, t, d), F32), jax.ShapeDtypeStruct((bsz, t, d), BF16)],
        compiler_params=_params(2),
        name="mix",
    )(pa, pb, x, w_out16, gate1, shift2, scale2, ln_g.reshape(1, d), ln_b.reshape(1, d))


def _take_top(s, n_iter, rows_f, on_pick):
    def body(it, s_rem):
        m = jnp.max(s_rem, axis=0, keepdims=True)
        first = jnp.min(jnp.where(s_rem == m, rows_f, float(s.shape[0])), axis=0, keepdims=True)
        pick = rows_f == first
        on_pick(it, m, pick)
        return jnp.where(pick, -jnp.inf, s_rem)

    lax.fori_loop(0, n_iter, body, s)


def _fill_candidates(vals_s, cand_s):
    cand_s[...] = jnp.full(cand_s.shape, -jnp.inf, F32)
    for r1 in range(PEER_TOPK):
        off, cnt = _CAND_OFFSETS[r1], _CAND_COUNTS[r1]
        cand_s[off:off + cnt, :] = vals_s[0, r1:r1 + 1, :] + vals_s[1, 0:cnt, :]


def _route_exact(h, score_s, rank_s, vals_s, cand_s, sel_s, lcol_ref, e1_ref, r2_ref, e2_ref):
    tm = score_s.shape[-1]
    rows_f = lax.broadcasted_iota(jnp.int32, (N_KEYS, tm), 0).astype(F32)
    crow_f = lax.broadcasted_iota(jnp.int32, (_N_CAND_PAD, tm), 0).astype(F32)
    for p in range(2):
        rank_s[p] = jnp.full((N_KEYS, tm), float(PEER_TOPK), F32)

        def on_pick(it, m, pick, p=p):
            vals_s[p, pl.ds(it, 1), :] = m
            rank_s[p] = jnp.where(pick, jnp.asarray(it, F32), rank_s[p])

        _take_top(score_s[p], PEER_TOPK, rows_f, on_pick)

    _fill_candidates(vals_s, cand_s)
    sel_s[...] = jnp.zeros((_N_CAND_PAD, tm), F32)

    def on_pick_cand(it, m, pick):
        sel_s[...] = jnp.where(pick, 1.0, sel_s[...])

    _take_top(cand_s[...], PEER_TOPK, crow_f, on_pick_cand)

    sel = sel_s[...]
    top = vals_s[0, 0:1, :] + vals_s[1, 0:1, :]
    z = jnp.sum(jnp.where(sel > 0.0, jnp.exp(cand_s[...] - top), 0.0), axis=0, keepdims=True)
    rank1 = rank_s[0]
    lcol = jnp.zeros((N_KEYS, tm), F32)
    for r1 in range(PEER_TOPK):
        off, cnt = _CAND_OFFSETS[r1], _CAND_COUNTS[r1]
        n_sel = jnp.sum(sel[off:off + cnt, :], axis=0, keepdims=True)
        lcol = lcol + jnp.where(rank1 == float(r1), n_sel, 0.0)
    lcol_ref[h] = lcol
    e1_ref[h] = jnp.exp(score_s[0] - vals_s[0, 0:1, :]) / z
    r2_ref[h] = rank_s[1].astype(BF16)
    e2_ref[h] = jnp.exp(score_s[1] - vals_s[1, 0:1, :]).astype(BF16)


def _compare_exchange(xs, i, l, descending):
    hi, lo = jnp.maximum(xs[i], xs[l]), jnp.minimum(xs[i], xs[l])
    xs[i], xs[l] = (hi, lo) if descending else (lo, hi)


def _bitonic_merge(xs):
    xs = list(xs)
    j = len(xs) // 2
    while j >= 1:
        for i in range(len(xs)):
            if i & j == 0:
                _compare_exchange(xs, i, i | j, True)
        j //= 2
    return xs


def _bitonic_sort(xs):
    xs = list(xs)
    k = 2
    while k <= len(xs):
        j = k // 2
        while j >= 1:
            for i in range(len(xs)):
                if i & j == 0:
                    _compare_exchange(xs, i, i | j, i & k == 0)
            j //= 2
        k *= 2
    return xs


def _sorted_top(pieces):
    xs = _bitonic_sort(pieces)
    for shift in (1, 2, 4):
        other = [pltpu.roll(x, shift, 0) for x in xs][::-1]
        if len(xs) < PEER_TOPK:
            xs = _bitonic_merge(xs + other)
        else:
            xs = _bitonic_merge([jnp.maximum(a, b) for a, b in zip(xs, other)])
    return xs


def _sublane_sum(x):
    for shift in (4, 2, 1):
        x = x + pltpu.roll(x, shift, 0)
    return x


def _route_fast(h, score_s, vals_s, cand_s, lcol_ref, e1_ref, r2_ref, e2_ref):
    tm = score_s.shape[-1]
    n_slab = N_KEYS // SUBLANES
    slabs = [[score_s[p, v * SUBLANES:(v + 1) * SUBLANES, :] for v in range(n_slab)] for p in range(2)]
    tops = [_sorted_top(slabs[p]) for p in range(2)]
    for p in range(2):
        for j in range(PEER_TOPK):
            vals_s[p, j:j + 1, :] = tops[p][j][0:1, :]
    _fill_candidates(vals_s, cand_s)
    cands = [cand_s[k * SUBLANES:(k + 1) * SUBLANES, :] for k in range(_N_CAND_PAD // SUBLANES)]
    pad = [jnp.full((SUBLANES, tm), -jnp.inf, F32)] * (SUBLANES - len(cands))
    tau = _sorted_top(cands + pad)[PEER_TOPK - 1]
    top = tops[0][0] + tops[1][0]
    n_picked = jnp.zeros((SUBLANES, tm), F32)
    z = jnp.zeros((SUBLANES, tm), F32)
    for c in cands:
        n_picked = n_picked + jnp.where(c >= tau, 1.0, 0.0)
        z = z + jnp.where(c >= tau, jnp.exp(c - top), 0.0)
    inv_z = 1.0 / _sublane_sum(z)
    tie = jnp.where(_sublane_sum(n_picked) != float(PEER_TOPK), 1.0, 0.0)
    for p in range(2):
        n_ge = jnp.zeros((SUBLANES, tm), F32)
        for s in slabs[p]:
            n_ge = n_ge + jnp.where(s >= tops[p][PEER_TOPK - 1], 1.0, 0.0)
        tie = jnp.maximum(tie, jnp.where(_sublane_sum(n_ge) != float(PEER_TOPK), 1.0, 0.0))
        for j in range(PEER_TOPK - 1):
            tie = jnp.maximum(tie, jnp.where(tops[p][j] == tops[p][j + 1], 1.0, 0.0))
    n_pick = []
    for r1 in range(PEER_TOPK):
        n = jnp.zeros((SUBLANES, tm), F32)
        for r2 in range(_CAND_COUNTS[r1]):
            n = n + jnp.where(tops[0][r1] + tops[1][r2] >= tau, 1.0, 0.0)
        n_pick.append(n)
    rank2, e2 = [], []
    for v in range(n_slab):
        rows = slice(v * SUBLANES, (v + 1) * SUBLANES)
        lc = jnp.zeros((SUBLANES, tm), F32)
        rk = jnp.full((SUBLANES, tm), float(PEER_TOPK), F32)
        for j in range(PEER_TOPK):
            lc = jnp.where(slabs[0][v] == tops[0][j], n_pick[j], lc)
            rk = jnp.where(slabs[1][v] == tops[1][j], float(j), rk)
        lcol_ref[h, rows, :] = lc
        e1_ref[h, rows, :] = jnp.exp(slabs[0][v] - tops[0][0]) * inv_z
        rank2.append(rk)
        e2.append(jnp.exp(slabs[1][v] - tops[1][0]))
    for u in range(n_slab // 2):
        rows = slice(2 * u * SUBLANES, (2 * u + 2) * SUBLANES)
        r2_ref[h, rows, :] = jnp.concatenate(rank2[2 * u:2 * u + 2], axis=0).astype(BF16)
        e2_ref[h, rows, :] = jnp.concatenate(e2[2 * u:2 * u + 2], axis=0).astype(BF16)
    return jnp.max(tie)


def _route_kernel(h2_ref, wpq_ref, keys_ref, lcol_ref, e1_ref, r2_ref, e2_ref,
                  q_s, score_s, rank_s, vals_s, cand_s, sel_s):
    q = jnp.dot(h2_ref[...], wpq_ref[...], preferred_element_type=F32)
    for hp in range(2 * PEER_HEADS):
        q_s[hp] = q[:, hp * N_KEYS:(hp + 1) * N_KEYS].astype(BF16)
    nt = (((1,), (1,)), ((), ()))

    def per_head(h, carry):
        for p in range(2):
            score_s[p] = lax.dot_general(keys_ref[h * 2 + p], q_s[h * 2 + p], nt,
                                         preferred_element_type=F32)
        tie = _route_fast(h, score_s, vals_s, cand_s, lcol_ref, e1_ref, r2_ref, e2_ref)

        @pl.when(tie > 0.0)
        def _():
            _route_exact(h, score_s, rank_s, vals_s, cand_s, sel_s, lcol_ref, e1_ref, r2_ref, e2_ref)

        return carry

    lax.fori_loop(0, PEER_HEADS, per_head, 0)


def _route(h2, w_pq16, keys16, tm):
    bsz, t, d = h2.shape
    nq = w_pq16.shape[1]
    out = pl.BlockSpec((None, PEER_HEADS, N_KEYS, tm), lambda b, i: (b, 0, 0, i))
    shape = jax.ShapeDtypeStruct((bsz, PEER_HEADS, N_KEYS, t), F32)
    shape16 = jax.ShapeDtypeStruct((bsz, PEER_HEADS, N_KEYS, t), BF16)
    return pl.pallas_call(
        _route_kernel,
        grid=(bsz, t // tm),
        in_specs=[pl.BlockSpec((None, tm, d), lambda b, i: (b, i, 0)),
                  pl.BlockSpec((d, nq), lambda b, i: (0, 0)),
                  pl.BlockSpec((2 * PEER_HEADS, N_KEYS, nq // (2 * PEER_HEADS)), lambda b, i: (0, 0, 0))],
        out_specs=[out, out, out, out],
        out_shape=[shape, shape, shape16, shape16],
        scratch_shapes=[pltpu.VMEM((2 * PEER_HEADS, tm, nq // (2 * PEER_HEADS)), BF16),
                        pltpu.VMEM((2, N_KEYS, tm), F32),
                        pltpu.VMEM((2, N_KEYS, tm), F32),
                        pltpu.VMEM((2, PEER_TOPK, tm), F32),
                        pltpu.VMEM((_N_CAND_PAD, tm), F32),
                        pltpu.VMEM((_N_CAND_PAD, tm), F32)],
        compiler_params=_params(2),
        name="route",
    )(h2, w_pq16, keys16)


_ACT_ROWS = 512


def _peer_kernel(h2_ref, u_ref, vt_ref, lcol_ref, e1_ref, r2_ref, e2_ref, x1_ref, g2_ref,
                 lg_ref, lb_ref, y_ref, acc, p_s, *, te):
    j = pl.program_id(2)

    @pl.when(j == 0)
    def _():
        acc[...] = jnp.zeros(acc.shape, F32)

    zero = jnp.zeros((), BF16)
    n_sub = _ACT_ROWS // N_KEYS
    for c in range(te // _ACT_ROWS):
        chunk = slice(c * _ACT_ROWS, (c + 1) * _ACT_ROWS)
        act = lax.dot_general(u_ref[chunk, :], h2_ref[...], (((1,), (1,)), ((), ())),
                              preferred_element_type=F32)
        for sub in range(n_sub):
            al = c * n_sub + sub
            w = None
            for h in range(PEER_HEADS):
                keep = r2_ref[h] < lcol_ref[h, al:al + 1, :].astype(BF16)
                term = jnp.where(keep, e2_ref[h] * e1_ref[h, al:al + 1, :].astype(BF16), zero)
                w = term if w is None else w + term
            g = _gelu(act[sub * N_KEYS:(sub + 1) * N_KEYS, :])
            p_s[al * N_KEYS:(al + 1) * N_KEYS, :] = (w.astype(F32) * g).astype(BF16)
    acc[...] += jnp.dot(vt_ref[...], p_s[...], preferred_element_type=F32)

    @pl.when(j == pl.num_programs(2) - 1)
    def _():
        ffn = acc[...].T
        y = _normalize(ALPHA * x1_ref[...] + g2_ref[...] * ffn)
        y_ref[...] = y * lg_ref[...] + lb_ref[...]


def _peer(h2, u16, vt16, lcol, e1, r2, e2, x1, gate2, ln_g, ln_b, tm, te):
    bsz, t, d = h2.shape
    n_exp = u16.shape[0]
    na = te // N_KEYS
    tok = pl.BlockSpec((None, tm, d), lambda b, i, j: (b, i, 0))
    tok1 = pl.BlockSpec((None, tm, d), lambda b, i, j: (b, i, 0), pipeline_mode=pl.Buffered(1))
    vec = pl.BlockSpec((1, d), lambda b, i, j: (0, 0))
    by_a = pl.BlockSpec((None, PEER_HEADS, na, tm), lambda b, i, j: (b, 0, j, i))
    by_b = pl.BlockSpec((None, PEER_HEADS, N_KEYS, tm), lambda b, i, j: (b, 0, 0, i))
    return pl.pallas_call(
        functools.partial(_peer_kernel, te=te),
        grid=(bsz, t // tm, n_exp // te),
        in_specs=[tok,
                  pl.BlockSpec((te, d), lambda b, i, j: (j, 0)),
                  pl.BlockSpec((d, te), lambda b, i, j: (0, j)),
                  by_a, by_a, by_b, by_b, tok1, _mod_spec(gate2, tm, d), vec, vec],
        out_specs=tok1,
        out_shape=jax.ShapeDtypeStruct((bsz, t, d), F32),
        scratch_shapes=[pltpu.VMEM((d, tm), F32), pltpu.VMEM((te, tm), BF16)],
        compiler_params=_params(3),
        name="peer",
    )(h2, u16, vt16, lcol, e1, r2, e2, x1, gate2, ln_g.reshape(1, d), ln_b.reshape(1, d))


def _tiles(n_tok, t):
    return dict(tm_proj=min(1024, n_tok),
                tm_mix=min(512, n_tok),
                tt=min(256, t), tc=512,
                tm_route=min(512, n_tok),
                tm_peer=min(512, n_tok), te=1024)


def _layer(x, mods, conv_buf, h0, attend, w, *, keep_rows, flatten):
    bsz, t, d = x.shape
    tl = _tiles(bsz * t if flatten else t, t)
    if flatten:
        mods = [jnp.broadcast_to(m, (bsz, t, d)).reshape(1, bsz * t, d) for m in mods]
        flat = lambda a: a.reshape(1, bsz * t, a.shape[-1])
    else:
        flat = lambda a: a
    shift1, scale1, gate1, shift2, scale2, gate2 = mods
    proj = _proj(flat(x), shift1, scale1, w["w_in"], tl["tm_proj"]).reshape(bsz, t, -1)
    pa, conv_state, rnn_state = _rnn(proj, conv_buf, h0, w["conv_w"], w["conv_b"], w["w_ra"], w["b_ra"],
                                     w["w_ri"], w["b_ri"], w["lam"], d, tl["tt"], tl["tc"])
    pb = attend(proj)
    x1, h2 = _mix(flat(pa), flat(pb), flat(x), w["w_out"], gate1, shift2, scale2, w["ln1_g"], w["ln1_b"],
                  tl["tm_mix"])
    lcol, e1, r2, e2 = _route(h2, w["w_pq"], w["keys"], tl["tm_route"])
    y = _peer(h2, w["peer_u"], w["peer_vt"], lcol, e1, r2, e2, x1, gate2, w["ln2_g"], w["ln2_b"],
              tl["tm_peer"], tl["te"])
    n_heads = d // HEAD_DIM
    k = proj[:, t - keep_rows:, 3 * d:4 * d].reshape(bsz, keep_rows, n_heads, HEAD_DIM)
    v = proj[:, t - keep_rows:, 4 * d:5 * d].reshape(bsz, keep_rows, n_heads, HEAD_DIM)
    return y.reshape(bsz, t, d), conv_state, rnn_state, k, v


def kernel(x_prompt, x_sample, c_prompt, c_sample, state_conv, state_rnn, cache_k, cache_v, w_ada, b_ada, w_in, conv_w, conv_b, w_ra, b_ra, w_ri, b_ri, rg_lambda, rel_bias, w_out, ln1_g, ln1_b, w_pq, peer_keys, peer_u, peer_v, ln2_g, ln2_b):
    bsz_p, t_p, d = x_prompt.shape
    bsz_s, t_s, _ = x_sample.shape
    depth = w_ada.shape[0]
    keep = min(ATT_REACH, t_p)
    y_p, y_s = x_prompt, x_sample
    outs = [[] for _ in range(8)]
    rows = bsz_p + bsz_s
    rows_pad = -(-rows // SUBLANES) * SUBLANES
    c_all = jnp.concatenate([c_prompt, c_sample, jnp.zeros((rows_pad - rows, d), F32)], axis=0)
    for l in range(depth):
        w = dict(w_in=w_in[l].astype(BF16), conv_w=conv_w[l], conv_b=conv_b[l],
                 w_ra=w_ra[l].astype(BF16), b_ra=b_ra[l], w_ri=w_ri[l].astype(BF16), b_ri=b_ri[l],
                 lam=rg_lambda[l], w_out=w_out[l].astype(BF16), ln1_g=ln1_g[l], ln1_b=ln1_b[l],
                 w_pq=w_pq[l].astype(BF16),
                 keys=peer_keys[l].astype(BF16).reshape(2 * PEER_HEADS, N_KEYS, -1),
                 peer_u=peer_u[l].astype(BF16), peer_vt=peer_v[l].astype(BF16).T,
                 ln2_g=ln2_g[l], ln2_b=ln2_b[l])
        mod = _ada(c_all, w_ada[l], b_ada[l])
        mods_p = [m[:, None, :] for m in jnp.split(mod[:bsz_p], 6, axis=-1)]
        mods_s = [m[:, None, :] for m in jnp.split(mod[bsz_p:rows], 6, axis=-1)]

        y_p, cs, hs, k, v = _layer(
            y_p, mods_p, jnp.zeros((bsz_p, CONV_W - 1, d), F32), jnp.zeros((bsz_p, d), F32),
            lambda proj: _attn_prompt(proj, rel_bias[l], d), w,
            keep_rows=keep, flatten=False)
        for dst, val in zip(outs[:4], (cs, hs, k, v)):
            dst.append(val)

        y_s, cs, hs, k, v = _layer(
            y_s, mods_s, state_conv[l], state_rnn[l],
            lambda proj: _attn_sample(proj, cache_k[l], cache_v[l], rel_bias[l], d), w,
            keep_rows=t_s, flatten=True)
        for dst, val in zip(outs[4:], (cs, hs, k, v)):
            dst.append(val)
    return (y_p, y_s) + tuple(jnp.stack(o) for o in outs)
```

```python
import functools

import jax
import jax.numpy as jnp
from jax import lax
from jax.experimental import pallas as pl
from jax.experimental.pallas import tpu as pltpu

F32 = jnp.float32
BF16 = jnp.bfloat16

CHUNK = 64
LEFT_CHUNKS = 8
ATT_REACH = LEFT_CHUNKS * CHUNK
MAX_REL = 128
REL_BUCKETS = MAX_REL + CHUNK
PAST_LEN = 1024
CONV_W = 4
RG_C = 8.0
RNN_BLOCK_DIM = 128
HEAD_DIM = 128
N_KEYS = 128
PEER_HEADS = 8
PEER_TOPK = 16
DEPTH = 1
ALPHA = (2.0 * DEPTH) ** 0.25
LN_EPS = 1e-5
NEG_INF = -1e30

LANES = 128
SUBLANES = 8
VMEM_LIMIT = 56 * 1024 * 1024

_CAND_COUNTS = tuple(PEER_TOPK // (r1 + 1) for r1 in range(PEER_TOPK))
_CAND_OFFSETS = tuple(sum(_CAND_COUNTS[:r1]) for r1 in range(PEER_TOPK))
_N_CAND = sum(_CAND_COUNTS)
_N_CAND_PAD = -(-_N_CAND // SUBLANES) * SUBLANES


def _params(n_grid):
    return pltpu.CompilerParams(dimension_semantics=("arbitrary",) * n_grid,
                                vmem_limit_bytes=VMEM_LIMIT)


def _normalize(x):
    mu = jnp.mean(x, axis=-1, keepdims=True)
    xc = x - mu
    var = jnp.mean(xc * xc, axis=-1, keepdims=True)
    return xc * lax.rsqrt(var + LN_EPS)


def _gelu(x):
    return 0.5 * x * (1.0 + lax.erf(x * (2.0 ** -0.5)))


def _ada_kernel(c_ref, w_ref, b_ref, o_ref):
    c = c_ref[...]
    s = c * jax.nn.sigmoid(c)
    o_ref[...] = jnp.dot(s, w_ref[...], preferred_element_type=F32) + b_ref[...]


def _ada(c, w_ada, b_ada):
    rows, d = c.shape
    n = w_ada.shape[1]
    tn = 1024
    return pl.pallas_call(
        _ada_kernel,
        grid=(n // tn,),
        in_specs=[pl.BlockSpec((rows, d), lambda j: (0, 0)),
                  pl.BlockSpec((d, tn), lambda j: (0, j)),
                  pl.BlockSpec((1, tn), lambda j: (0, j))],
        out_specs=pl.BlockSpec((rows, tn), lambda j: (0, j)),
        out_shape=jax.ShapeDtypeStruct((rows, n), F32),
        compiler_params=_params(1),
        name="ada",
    )(c, w_ada, b_ada.reshape(1, n))


def _proj_kernel(x_ref, sh_ref, sc_ref, w_ref, o_ref, h_ref):
    @pl.when(pl.program_id(2) == 0)
    def _():
        h = _normalize(x_ref[...]) * (1.0 + sc_ref[...]) + sh_ref[...]
        h_ref[...] = h.astype(BF16)

    o_ref[...] = jnp.dot(h_ref[...], w_ref[...], preferred_element_type=F32)


def _mod_spec(mod, tm, d):
    if mod.shape[1] == 1:
        return pl.BlockSpec((None, 1, d), lambda b, i, *_: (b, 0, 0))
    return pl.BlockSpec((None, tm, d), lambda b, i, *_: (b, i, 0))


def _proj(x, shift, scale, w_in_bf16, tm):
    bsz, t, d = x.shape
    n = w_in_bf16.shape[1]
    tn = 1024
    return pl.pallas_call(
        _proj_kernel,
        grid=(bsz, t // tm, n // tn),
        in_specs=[pl.BlockSpec((None, tm, d), lambda b, i, j: (b, i, 0)),
                  _mod_spec(shift, tm, d), _mod_spec(scale, tm, d),
                  pl.BlockSpec((d, tn), lambda b, i, j: (0, j))],
        out_specs=pl.BlockSpec((None, tm, tn), lambda b, i, j: (b, i, j)),
        out_shape=jax.ShapeDtypeStruct((bsz, t, n), F32),
        scratch_shapes=[pltpu.VMEM((tm, d), BF16)],
        compiler_params=_params(3),
        name="proj",
    )(x, shift, scale, w_in_bf16)


def _rnn_kernel(xr_ref, yr_ref, ga_ref, cbuf_ref, h0_ref, cw_ref, cb_ref, wra_ref, bra_ref,
                wri_ref, bri_ref, lam_ref, pa_ref, cst_ref, hst_ref,
                xbuf, a_s, u_s, hcar, *, tt, tc):
    t = pl.program_id(2)
    nt = pl.num_programs(2)
    pad = SUBLANES
    keep = CONV_W - 1

    @pl.when(t == 0)
    def _():
        xbuf[pad - keep:pad, :] = cbuf_ref[...]
        hcar[...] = h0_ref[...]

    xbuf[pad:pad + tt, :] = xr_ref[...]
    cw = cw_ref[...]
    xc = cb_ref[...] + cw[0:1, :] * xbuf[pad - keep:pad - keep + tt, :]
    for j in range(1, CONV_W):
        xc = xc + cw[j:j + 1, :] * xbuf[pad - keep + j:pad - keep + j + tt, :]
    tail = xbuf[pad + tt - keep:pad + tt, :]
    xbuf[pad - keep:pad, :] = tail

    lam = lam_ref[...]
    softplus_neg = jnp.maximum(-lam, 0.0) + jnp.log1p(jnp.exp(-jnp.abs(lam)))
    for blk in range(tc // RNN_BLOCK_DIM):
        cols = slice(blk * RNN_BLOCK_DIM, (blk + 1) * RNN_BLOCK_DIM)
        xb = xc[:, cols]
        xb16 = xb.astype(BF16)
        r = jax.nn.sigmoid(jnp.dot(xb16, wra_ref[blk], preferred_element_type=F32) + bra_ref[:, cols])
        i = jax.nn.sigmoid(jnp.dot(xb16, wri_ref[blk], preferred_element_type=F32) + bri_ref[:, cols])
        log_a = -RG_C * r * softplus_neg[:, cols]
        a = jnp.exp(log_a)
        u = jnp.sqrt(jnp.tanh(-log_a) * (1.0 + a * a)) * (i * xb)
        a_s[:, cols] = a
        u_s[:, cols] = u

    row = lax.broadcasted_iota(jnp.int32, (SUBLANES, tc), 0)

    def scan_group(g, h_prev):
        rows = pl.ds(pl.multiple_of(g * SUBLANES, SUBLANES), SUBLANES)
        a = a_s[rows, :]
        u = u_s[rows, :]
        for s in (1, 2, 4):
            a_sh = pltpu.roll(a, s, 0)
            u_sh = pltpu.roll(u, s, 0)
            live = row >= s
            u = jnp.where(live, a * u_sh + u, u)
            a = jnp.where(live, a * a_sh, a)
        h = a * h_prev + u
        u_s[rows, :] = h
        return h[SUBLANES - 1:SUBLANES, :]

    h_last = lax.fori_loop(0, tt // SUBLANES, scan_group, hcar[...])
    hcar[...] = h_last

    pa_ref[...] = u_s[...] * _gelu(yr_ref[...]) * jax.nn.sigmoid(ga_ref[...])

    @pl.when(t == nt - 1)
    def _():
        cst_ref[...] = tail
        hst_ref[...] = h_last


def _rnn(proj, conv_buf, h0, conv_w, conv_b, w_ra16, b_ra, w_ri16, b_ri, lam, d, tt, tc):
    bsz, t, _ = proj.shape
    nc = d // tc
    nb = tc // RNN_BLOCK_DIM
    keep = CONV_W - 1
    col = lambda off: (lambda b, c, i: (b, i, off * nc + c))
    vec = pl.BlockSpec((1, tc), lambda b, c, i: (0, c))
    blk = pl.BlockSpec((nb, RNN_BLOCK_DIM, RNN_BLOCK_DIM), lambda b, c, i: (c, 0, 0))
    pa, cst, hst = pl.pallas_call(
        functools.partial(_rnn_kernel, tt=tt, tc=tc),
        grid=(bsz, nc, t // tt),
        in_specs=[pl.BlockSpec((None, tt, tc), col(0)),
                  pl.BlockSpec((None, tt, tc), col(1)),
                  pl.BlockSpec((None, tt, tc), col(5)),
                  pl.BlockSpec((None, keep, tc), lambda b, c, i: (b, 0, c)),
                  pl.BlockSpec((None, 1, tc), lambda b, c, i: (b, 0, c)),
                  pl.BlockSpec((CONV_W, tc), lambda b, c, i: (0, c)),
                  vec, blk, vec, blk, vec, vec],
        out_specs=[pl.BlockSpec((None, tt, tc), lambda b, c, i: (b, i, c)),
                   pl.BlockSpec((None, keep, tc), lambda b, c, i: (b, 0, c)),
                   pl.BlockSpec((None, 1, tc), lambda b, c, i: (b, 0, c))],
        out_shape=[jax.ShapeDtypeStruct((bsz, t, d), F32),
                   jax.ShapeDtypeStruct((bsz, keep, d), F32),
                   jax.ShapeDtypeStruct((bsz, 1, d), F32)],
        scratch_shapes=[pltpu.VMEM((tt + SUBLANES, tc), F32),
                        pltpu.VMEM((tt, tc), F32),
                        pltpu.VMEM((tt, tc), F32),
                        pltpu.VMEM((1, tc), F32)],
        compiler_params=_params(3),
        name="rnn",
    )(proj, proj, proj, conv_buf, h0.reshape(bsz, 1, d), conv_w, conv_b.reshape(1, d),
      w_ra16, b_ra.reshape(1, d), w_ri16, b_ri.reshape(1, d), lam.reshape(1, d))
    return pa, cst, hst.reshape(bsz, d)


def _rel_bias_tile(rel_ref, head, dist, visible):
    idx = jnp.clip(dist, -(CHUNK - 1), MAX_REL) + (CHUNK - 1)

    def body(j, acc):
        return acc + jnp.where(idx == j, rel_ref[head, j], 0.0)

    bias = lax.fori_loop(0, REL_BUCKETS, body, jnp.zeros(dist.shape, F32))
    return jnp.where(visible, bias, NEG_INF)


def _softmax_pv(s, v16):
    m = jnp.max(s, axis=-1, keepdims=True)
    p = jnp.exp(s - m)
    l = jnp.sum(p, axis=-1, keepdims=True)
    return jnp.dot(p.astype(BF16), v16, preferred_element_type=F32) / l


_QSUB = 2 * CHUNK
_BAND = _QSUB + ATT_REACH


_ATTN_HEADS = 4


def _attn_prompt_kernel(rel_ref, q_ref, kp_ref, kc_ref, vp_ref, vc_ref, gb_ref, o_ref,
                        bias_s, kcat, vcat):
    group, b, i = pl.program_id(0), pl.program_id(1), pl.program_id(2)
    tq = ATT_REACH

    @pl.when((b == 0) & (i == 0))
    def _():
        r = lax.broadcasted_iota(jnp.int32, (_QSUB, _BAND), 0)
        c = lax.broadcasted_iota(jnp.int32, (_QSUB, _BAND), 1)
        qc = r // CHUNK + LEFT_CHUNKS
        kc = c // CHUNK
        visible = (kc <= qc) & (kc >= qc - LEFT_CHUNKS)
        for hh in range(_ATTN_HEADS):
            bias_s[hh] = _rel_bias_tile(rel_ref, group * _ATTN_HEADS + hh, r + ATT_REACH - c, visible)

    kcat[0:tq, :] = kp_ref[...].astype(BF16)
    kcat[tq:2 * tq, :] = kc_ref[...].astype(BF16)
    vcat[0:tq, :] = vp_ref[...].astype(BF16)
    vcat[tq:2 * tq, :] = vc_ref[...].astype(BF16)
    col = lax.broadcasted_iota(jnp.int32, (_QSUB, _BAND), 1)
    n_sub = tq // _QSUB
    for hh in range(_ATTN_HEADS):
        cols = slice(hh * HEAD_DIM, (hh + 1) * HEAD_DIM)
        scores = []
        for s in range(n_sub):
            q16 = q_ref[s * _QSUB:(s + 1) * _QSUB, cols].astype(BF16)
            sc = lax.dot_general(q16, kcat[s * _QSUB:s * _QSUB + _BAND, cols], (((1,), (1,)), ((), ())),
                                 preferred_element_type=F32) * (HEAD_DIM ** -0.5)
            sc = sc + bias_s[hh]
            k_pos = col + (i * tq + s * _QSUB - ATT_REACH)
            scores.append(jnp.where(k_pos >= 0, sc, NEG_INF))
        maxes = [jnp.max(sc, axis=-1, keepdims=True) for sc in scores]
        probs = [jnp.exp(sc - m) for sc, m in zip(scores, maxes)]
        sums = [jnp.sum(p, axis=-1, keepdims=True) for p in probs]
        outs = [jnp.dot(p.astype(BF16), vcat[s * _QSUB:s * _QSUB + _BAND, cols], preferred_element_type=F32)
                for s, p in enumerate(probs)]
        for s in range(n_sub):
            rows = slice(s * _QSUB, (s + 1) * _QSUB)
            o_ref[rows, cols] = outs[s] / sums[s] * jax.nn.sigmoid(gb_ref[rows, cols])


def _attn_prompt(proj, rel_bias, d):
    bsz, t, _ = proj.shape
    ng = d // (HEAD_DIM * _ATTN_HEADS)
    width = HEAD_DIM * _ATTN_HEADS
    tq = ATT_REACH
    cur = lambda off: (lambda g, b, i: (b, i, off * ng + g))
    prev = lambda off: (lambda g, b, i: (b, jnp.maximum(i - 1, 0), off * ng + g))
    blk = lambda f: pl.BlockSpec((None, tq, width), f)
    return pl.pallas_call(
        _attn_prompt_kernel,
        grid=(ng, bsz, t // tq),
        in_specs=[pl.BlockSpec(memory_space=pltpu.SMEM),
                  blk(cur(2)), blk(prev(3)), blk(cur(3)), blk(prev(4)), blk(cur(4)), blk(cur(6))],
        out_specs=pl.BlockSpec((None, tq, width), lambda g, b, i: (b, i, g)),
        out_shape=jax.ShapeDtypeStruct((bsz, t, d), F32),
        scratch_shapes=[pltpu.VMEM((_ATTN_HEADS, _QSUB, _BAND), F32),
                        pltpu.VMEM((2 * tq, width), BF16),
                        pltpu.VMEM((2 * tq, width), BF16)],
        compiler_params=_params(3),
        name="attn_prompt",
    )(rel_bias, proj, proj, proj, proj, proj, proj)


def _attn_sample_kernel(rel_ref, q_ref, kn_ref, vn_ref, gb_ref, ck_ref, cv_ref, o_ref,
                        bias_past_s, bias_new_s, *, t, n_past):
    group = pl.program_id(0)

    def bias(head, n_keys, first_pos):
        r = lax.broadcasted_iota(jnp.int32, (t, n_keys), 0)
        c = lax.broadcasted_iota(jnp.int32, (t, n_keys), 1)
        q_pos = PAST_LEN + r
        k_pos = first_pos + c
        qc = q_pos // CHUNK
        kc = k_pos // CHUNK
        visible = (k_pos >= 0) & (kc <= qc) & (kc >= qc - LEFT_CHUNKS)
        return _rel_bias_tile(rel_ref, head, q_pos - k_pos, visible)

    @pl.when(pl.program_id(1) == 0)
    def _():
        for hh in range(_ATTN_HEADS):
            bias_past_s[hh] = bias(group * _ATTN_HEADS + hh, n_past, PAST_LEN - n_past)
            bias_new_s[hh] = bias(group * _ATTN_HEADS + hh, t, PAST_LEN)

    scale = HEAD_DIM ** -0.5
    nt = (((1,), (1,)), ((), ()))
    for hh in range(_ATTN_HEADS):
        cols = slice(hh * HEAD_DIM, (hh + 1) * HEAD_DIM)
        q16 = q_ref[:, cols].astype(BF16)
        s_past = lax.dot_general(q16, ck_ref[:, cols].astype(BF16), nt, preferred_element_type=F32) * scale
        s_new = lax.dot_general(q16, kn_ref[:, cols].astype(BF16), nt, preferred_element_type=F32) * scale
        s_past = s_past + bias_past_s[hh]
        s_new = s_new + bias_new_s[hh]
        m = jnp.maximum(jnp.max(s_past, axis=-1, keepdims=True), jnp.max(s_new, axis=-1, keepdims=True))
        p_past = jnp.exp(s_past - m)
        p_new = jnp.exp(s_new - m)
        l = jnp.sum(p_past, axis=-1, keepdims=True) + jnp.sum(p_new, axis=-1, keepdims=True)
        o = (jnp.dot(p_past.astype(BF16), cv_ref[:, cols].astype(BF16), preferred_element_type=F32)
             + jnp.dot(p_new.astype(BF16), vn_ref[:, cols].astype(BF16), preferred_element_type=F32)) / l
        o_ref[:, cols] = o * jax.nn.sigmoid(gb_ref[:, cols])


def _attn_sample(proj, cache_k, cache_v, rel_bias, d):
    bsz, t, _ = proj.shape
    width = HEAD_DIM * _ATTN_HEADS
    ng = d // width
    n_past = cache_k.shape[1]
    new = lambda off: pl.BlockSpec((None, t, width), lambda g, b: (b, 0, off * ng + g))
    past = pl.BlockSpec((None, n_past, width), lambda g, b: (b, 0, g))
    return pl.pallas_call(
        functools.partial(_attn_sample_kernel, t=t, n_past=n_past),
        grid=(ng, bsz),
        in_specs=[pl.BlockSpec(memory_space=pltpu.SMEM), new(2), new(3), new(4), new(6), past, past],
        out_specs=pl.BlockSpec((None, t, width), lambda g, b: (b, 0, g)),
        out_shape=jax.ShapeDtypeStruct((bsz, t, d), F32),
        scratch_shapes=[pltpu.VMEM((_ATTN_HEADS, t, n_past), F32), pltpu.VMEM((_ATTN_HEADS, t, t), F32)],
        compiler_params=_params(2),
        name="attn_sample",
    )(rel_bias, proj, proj, proj, proj, cache_k.reshape(bsz, n_past, d), cache_v.reshape(bsz, n_past, d))


def _mix_kernel(pa_ref, pb_ref, x_ref, w_ref, g1_ref, sh2_ref, sc2_ref, lg_ref, lb_ref,
                x1_ref, h2_ref):
    mixed = jnp.dot((pa_ref[...] + pb_ref[...]).astype(BF16), w_ref[...], preferred_element_type=F32)
    x1 = _normalize(ALPHA * x_ref[...] + g1_ref[...] * mixed) * lg_ref[...] + lb_ref[...]
    x1_ref[...] = x1
    h2_ref[...] = (_normalize(x1) * (1.0 + sc2_ref[...]) + sh2_ref[...]).astype(BF16)


def _mix(pa, pb, x, w_out16, gate1, shift2, scale2, ln_g, ln_b, tm):
    bsz, t, d = x.shape
    tok = pl.BlockSpec((None, tm, d), lambda b, i: (b, i, 0))
    vec = pl.BlockSpec((1, d), lambda b, i: (0, 0))
    return pl.pallas_call(
        _mix_kernel,
        grid=(bsz, t // tm),
        in_specs=[tok, tok, tok, pl.BlockSpec((d, d), lambda b, i: (0, 0)),
                  _mod_spec(gate1, tm, d), _mod_spec(shift2, tm, d), _mod_spec(scale2, tm, d), vec, vec],
        out_specs=[tok, tok],
        out_shape=[jax.ShapeDtypeStruct((bsz, t, d), F32), jax.ShapeDtypeStruct((bsz, t, d), BF16)],
        compiler_params=_params(2),
        name="mix",
    )(pa, pb, x, w_out16, gate1, shift2, scale2, ln_g.reshape(1, d), ln_b.reshape(1, d))


def _take_top(s, n_iter, rows_f, on_pick):
    def body(it, s_rem):
        m = jnp.max(s_rem, axis=0, keepdims=True)
        first = jnp.min(jnp.where(s_rem == m, rows_f, float(s.shape[0])), axis=0, keepdims=True)
        pick = rows_f == first
        on_pick(it, m, pick)
        return jnp.where(pick, -jnp.inf, s_rem)

    lax.fori_loop(0, n_iter, body, s)


def _fill_candidates(vals_s, cand_s):
    cand_s[...] = jnp.full(cand_s.shape, -jnp.inf, F32)
    for r1 in range(PEER_TOPK):
        off, cnt = _CAND_OFFSETS[r1], _CAND_COUNTS[r1]
        cand_s[off:off + cnt, :] = vals_s[0, r1:r1 + 1, :] + vals_s[1, 0:cnt, :]


def _route_exact(h, score_s, rank_s, vals_s, cand_s, sel_s, lcol_ref, e1_ref, r2_ref, e2_ref):
    tm = score_s.shape[-1]
    rows_f = lax.broadcasted_iota(jnp.int32, (N_KEYS, tm), 0).astype(F32)
    crow_f = lax.broadcasted_iota(jnp.int32, (_N_CAND_PAD, tm), 0).astype(F32)
    for p in range(2):
        rank_s[p] = jnp.full((N_KEYS, tm), float(PEER_TOPK), F32)

        def on_pick(it, m, pick, p=p):
            vals_s[p, pl.ds(it, 1), :] = m
            rank_s[p] = jnp.where(pick, jnp.asarray(it, F32), rank_s[p])

        _take_top(score_s[p], PEER_TOPK, rows_f, on_pick)

    _fill_candidates(vals_s, cand_s)
    sel_s[...] = jnp.zeros((_N_CAND_PAD, tm), F32)

    def on_pick_cand(it, m, pick):
        sel_s[...] = jnp.where(pick, 1.0, sel_s[...])

    _take_top(cand_s[...], PEER_TOPK, crow_f, on_pick_cand)

    sel = sel_s[...]
    top = vals_s[0, 0:1, :] + vals_s[1, 0:1, :]
    z = jnp.sum(jnp.where(sel > 0.0, jnp.exp(cand_s[...] - top), 0.0), axis=0, keepdims=True)
    rank1 = rank_s[0]
    lcol = jnp.zeros((N_KEYS, tm), F32)
    for r1 in range(PEER_TOPK):
        off, cnt = _CAND_OFFSETS[r1], _CAND_COUNTS[r1]
        n_sel = jnp.sum(sel[off:off + cnt, :], axis=0, keepdims=True)
        lcol = lcol + jnp.where(rank1 == float(r1), n_sel, 0.0)
    lcol_ref[h] = lcol
    e1_ref[h] = jnp.exp(score_s[0] - vals_s[0, 0:1, :]) / z
    r2_ref[h] = rank_s[1].astype(BF16)
    e2_ref[h] = jnp.exp(score_s[1] - vals_s[1, 0:1, :]).astype(BF16)


def _compare_exchange(xs, i, l, descending):
    hi, lo = jnp.maximum(xs[i], xs[l]), jnp.minimum(xs[i], xs[l])
    xs[i], xs[l] = (hi, lo) if descending else (lo, hi)


def _bitonic_merge(xs):
    xs = list(xs)
    j = len(xs) // 2
    while j >= 1:
        for i in range(len(xs)):
            if i & j == 0:
                _compare_exchange(xs, i, i | j, True)
        j //= 2
    return xs


def _bitonic_sort(xs):
    xs = list(xs)
    k = 2
    while k <= len(xs):
        j = k // 2
        while j >= 1:
            for i in range(len(xs)):
                if i & j == 0:
                    _compare_exchange(xs, i, i | j, i & k == 0)
            j //= 2
        k *= 2
    return xs


def _sorted_top(pieces):
    xs = _bitonic_sort(pieces)
    for shift in (1, 2, 4):
        other = [pltpu.roll(x, shift, 0) for x in xs][::-1]
        if len(xs) < PEER_TOPK:
            xs = _bitonic_merge(xs + other)
        else:
            xs = _bitonic_merge([jnp.maximum(a, b) for a, b in zip(xs, other)])
    return xs


def _sublane_sum(x):
    for shift in (4, 2, 1):
        x = x + pltpu.roll(x, shift, 0)
    return x


def _route_fast(h, score_s, vals_s, cand_s, lcol_ref, e1_ref, r2_ref, e2_ref):
    tm = score_s.shape[-1]
    n_slab = N_KEYS // SUBLANES
    slabs = [[score_s[p, v * SUBLANES:(v + 1) * SUBLANES, :] for v in range(n_slab)] for p in range(2)]
    tops = [_sorted_top(slabs[p]) for p in range(2)]
    for p in range(2):
        for j in range(PEER_TOPK):
            vals_s[p, j:j + 1, :] = tops[p][j][0:1, :]
    _fill_candidates(vals_s, cand_s)
    cands = [cand_s[k * SUBLANES:(k + 1) * SUBLANES, :] for k in range(_N_CAND_PAD // SUBLANES)]
    pad = [jnp.full((SUBLANES, tm), -jnp.inf, F32)] * (SUBLANES - len(cands))
    tau = _sorted_top(cands + pad)[PEER_TOPK - 1]
    top = tops[0][0] + tops[1][0]
    n_picked = jnp.zeros((SUBLANES, tm), F32)
    z = jnp.zeros((SUBLANES, tm), F32)
    for c in cands:
        n_picked = n_picked + jnp.where(c >= tau, 1.0, 0.0)
        z = z + jnp.where(c >= tau, jnp.exp(c - top), 0.0)
    inv_z = 1.0 / _sublane_sum(z)
    tie = jnp.where(_sublane_sum(n_picked) != float(PEER_TOPK), 1.0, 0.0)
    for p in range(2):
        n_ge = jnp.zeros((SUBLANES, tm), F32)
        for s in slabs[p]:
            n_ge = n_ge + jnp.where(s >= tops[p][PEER_TOPK - 1], 1.0, 0.0)
        tie = jnp.maximum(tie, jnp.where(_sublane_sum(n_ge) != float(PEER_TOPK), 1.0, 0.0))
        for j in range(PEER_TOPK - 1):
            tie = jnp.maximum(tie, jnp.where(tops[p][j] == tops[p][j + 1], 1.0, 0.0))
    n_pick = []
    for r1 in range(PEER_TOPK):
        n = jnp.zeros((SUBLANES, tm), F32)
        for r2 in range(_CAND_COUNTS[r1]):
            n = n + jnp.where(tops[0][r1] + tops[1][r2] >= tau, 1.0, 0.0)
        n_pick.append(n)
    rank2, e2 = [], []
    for v in range(n_slab):
        rows = slice(v * SUBLANES, (v + 1) * SUBLANES)
        lc = jnp.zeros((SUBLANES, tm), F32)
        rk = jnp.full((SUBLANES, tm), float(PEER_TOPK), F32)
        for j in range(PEER_TOPK):
            lc = jnp.where(slabs[0][v] == tops[0][j], n_pick[j], lc)
            rk = jnp.where(slabs[1][v] == tops[1][j], float(j), rk)
        lcol_ref[h, rows, :] = lc
        e1_ref[h, rows, :] = jnp.exp(slabs[0][v] - tops[0][0]) * inv_z
        rank2.append(rk)
        e2.append(jnp.exp(slabs[1][v] - tops[1][0]))
    for u in range(n_slab // 2):
        rows = slice(2 * u * SUBLANES, (2 * u + 2) * SUBLANES)
        r2_ref[h, rows, :] = jnp.concatenate(rank2[2 * u:2 * u + 2], axis=0).astype(BF16)
        e2_ref[h, rows, :] = jnp.concatenate(e2[2 * u:2 * u + 2], axis=0).astype(BF16)
    return jnp.max(tie)


def _route_kernel(h2_ref, wpq_ref, keys_ref, lcol_ref, e1_ref, r2_ref, e2_ref,
                  q_s, score_s, rank_s, vals_s, cand_s, sel_s):
    q = jnp.dot(h2_ref[...], wpq_ref[...], preferred_element_type=F32)
    for hp in range(2 * PEER_HEADS):
        q_s[hp] = q[:, hp * N_KEYS:(hp + 1) * N_KEYS].astype(BF16)
    nt = (((1,), (1,)), ((), ()))

    def per_head(h, carry):
        for p in range(2):
            score_s[p] = lax.dot_general(keys_ref[h * 2 + p], q_s[h * 2 + p], nt,
                                         preferred_element_type=F32)
        tie = _route_fast(h, score_s, vals_s, cand_s, lcol_ref, e1_ref, r2_ref, e2_ref)

        @pl.when(tie > 0.0)
        def _():
            _route_exact(h, score_s, rank_s, vals_s, cand_s, sel_s, lcol_ref, e1_ref, r2_ref, e2_ref)

        return carry

    lax.fori_loop(0, PEER_HEADS, per_head, 0)


def _route(h2, w_pq16, keys16, tm):
    bsz, t, d = h2.shape
    nq = w_pq16.shape[1]
    out = pl.BlockSpec((None, PEER_HEADS, N_KEYS, tm), lambda b, i: (b, 0, 0, i))
    shape = jax.ShapeDtypeStruct((bsz, PEER_HEADS, N_KEYS, t), F32)
    shape16 = jax.ShapeDtypeStruct((bsz, PEER_HEADS, N_KEYS, t), BF16)
    return pl.pallas_call(
        _route_kernel,
        grid=(bsz, t // tm),
        in_specs=[pl.BlockSpec((None, tm, d), lambda b, i: (b, i, 0)),
                  pl.BlockSpec((d, nq), lambda b, i: (0, 0)),
                  pl.BlockSpec((2 * PEER_HEADS, N_KEYS, nq // (2 * PEER_HEADS)), lambda b, i: (0, 0, 0))],
        out_specs=[out, out, out, out],
        out_shape=[shape, shape, shape16, shape16],
        scratch_shapes=[pltpu.VMEM((2 * PEER_HEADS, tm, nq // (2 * PEER_HEADS)), BF16),
                        pltpu.VMEM((2, N_KEYS, tm), F32),
                        pltpu.VMEM((2, N_KEYS, tm), F32),
                        pltpu.VMEM((2, PEER_TOPK, tm), F32),
                        pltpu.VMEM((_N_CAND_PAD, tm), F32),
                        pltpu.VMEM((_N_CAND_PAD, tm), F32)],
        compiler_params=_params(2),
        name="route",
    )(h2, w_pq16, keys16)


_ACT_ROWS = 512


def _peer_kernel(h2_ref, u_ref, vt_ref, lcol_ref, e1_ref, r2_ref, e2_ref, x1_ref, g2_ref,
                 lg_ref, lb_ref, y_ref, acc, p_s, *, te):
    j = pl.program_id(2)

    @pl.when(j == 0)
    def _():
        acc[...] = jnp.zeros(acc.shape, F32)

    zero = jnp.zeros((), BF16)
    n_sub = _ACT_ROWS // N_KEYS
    for c in range(te // _ACT_ROWS):
        chunk = slice(c * _ACT_ROWS, (c + 1) * _ACT_ROWS)
        act = lax.dot_general(u_ref[chunk, :], h2_ref[...], (((1,), (1,)), ((), ())),
                              preferred_element_type=F32)
        for sub in range(n_sub):
            al = c * n_sub + sub
            w = None
            for h in range(PEER_HEADS):
                keep = r2_ref[h] < lcol_ref[h, al:al + 1, :].astype(BF16)
                term = jnp.where(keep, e2_ref[h] * e1_ref[h, al:al + 1, :].astype(BF16), zero)
                w = term if w is None else w + term
            g = _gelu(act[sub * N_KEYS:(sub + 1) * N_KEYS, :])
            p_s[al * N_KEYS:(al + 1) * N_KEYS, :] = (w.astype(F32) * g).astype(BF16)
    acc[...] += jnp.dot(vt_ref[...], p_s[...], preferred_element_type=F32)

    @pl.when(j == pl.num_programs(2) - 1)
    def _():
        ffn = acc[...].T
        y = _normalize(ALPHA * x1_ref[...] + g2_ref[...] * ffn)
        y_ref[...] = y * lg_ref[...] + lb_ref[...]


def _peer(h2, u16, vt16, lcol, e1, r2, e2, x1, gate2, ln_g, ln_b, tm, te):
    bsz, t, d = h2.shape
    n_exp = u16.shape[0]
    na = te // N_KEYS
    tok = pl.BlockSpec((None, tm, d), lambda b, i, j: (b, i, 0))
    tok1 = pl.BlockSpec((None, tm, d), lambda b, i, j: (b, i, 0), pipeline_mode=pl.Buffered(1))
    vec = pl.BlockSpec((1, d), lambda b, i, j: (0, 0))
    by_a = pl.BlockSpec((None, PEER_HEADS, na, tm), lambda b, i, j: (b, 0, j, i))
    by_b = pl.BlockSpec((None, PEER_HEADS, N_KEYS, tm), lambda b, i, j: (b, 0, 0, i))
    return pl.pallas_call(
        functools.partial(_peer_kernel, te=te),
        grid=(bsz, t // tm, n_exp // te),
        in_specs=[tok,
                  pl.BlockSpec((te, d), lambda b, i, j: (j, 0)),
                  pl.BlockSpec((d, te), lambda b, i, j: (0, j)),
                  by_a, by_a, by_b, by_b, tok1, _mod_spec(gate2, tm, d), vec, vec],
        out_specs=tok1,
        out_shape=jax.ShapeDtypeStruct((bsz, t, d), F32),
        scratch_shapes=[pltpu.VMEM((d, tm), F32), pltpu.VMEM((te, tm), BF16)],
        compiler_params=_params(3),
        name="peer",
    )(h2, u16, vt16, lcol, e1, r2, e2, x1, gate2, ln_g.reshape(1, d), ln_b.reshape(1, d))


def _tiles(n_tok, t):
    return dict(tm_proj=min(1024, n_tok),
                tm_mix=min(512, n_tok),
                tt=min(512, t), tc=512,
                tm_route=min(512, n_tok),
                tm_peer=min(512, n_tok), te=1024)


def _layer(x, mods, conv_buf, h0, attend, w, *, keep_rows, flatten):
    bsz, t, d = x.shape
    tl = _tiles(bsz * t if flatten else t, t)
    if flatten:
        mods = [jnp.broadcast_to(m, (bsz, t, d)).reshape(1, bsz * t, d) for m in mods]
        flat = lambda a: a.reshape(1, bsz * t, a.shape[-1])
    else:
        flat = lambda a: a
    shift1, scale1, gate1, shift2, scale2, gate2 = mods
    proj = _proj(flat(x), shift1, scale1, w["w_in"], tl["tm_proj"]).reshape(bsz, t, -1)
    pa, conv_state, rnn_state = _rnn(proj, conv_buf, h0, w["conv_w"], w["conv_b"], w["w_ra"], w["b_ra"],
                                     w["w_ri"], w["b_ri"], w["lam"], d, tl["tt"], tl["tc"])
    pb = attend(proj)
    x1, h2 = _mix(flat(pa), flat(pb), flat(x), w["w_out"], gate1, shift2, scale2, w["ln1_g"], w["ln1_b"],
                  tl["tm_mix"])
    lcol, e1, r2, e2 = _route(h2, w["w_pq"], w["keys"], tl["tm_route"])
    y = _peer(h2, w["peer_u"], w["peer_vt"], lcol, e1, r2, e2, x1, gate2, w["ln2_g"], w["ln2_b"],
              tl["tm_peer"], tl["te"])
    n_heads = d // HEAD_DIM
    k = proj[:, t - keep_rows:, 3 * d:4 * d].reshape(bsz, keep_rows, n_heads, HEAD_DIM)
    v = proj[:, t - keep_rows:, 4 * d:5 * d].reshape(bsz, keep_rows, n_heads, HEAD_DIM)
    return y.reshape(bsz, t, d), conv_state, rnn_state, k, v


def kernel(x_prompt, x_sample, c_prompt, c_sample, state_conv, state_rnn, cache_k, cache_v, w_ada, b_ada, w_in, conv_w, conv_b, w_ra, b_ra, w_ri, b_ri, rg_lambda, rel_bias, w_out, ln1_g, ln1_b, w_pq, peer_keys, peer_u, peer_v, ln2_g, ln2_b):
    bsz_p, t_p, d = x_prompt.shape
    bsz_s, t_s, _ = x_sample.shape
    depth = w_ada.shape[0]
    keep = min(ATT_REACH, t_p)
    y_p, y_s = x_prompt, x_sample
    outs = [[] for _ in range(8)]
    rows = bsz_p + bsz_s
    rows_pad = -(-rows // SUBLANES) * SUBLANES
    c_all = jnp.concatenate([c_prompt, c_sample, jnp.zeros((rows_pad - rows, d), F32)], axis=0)
    for l in range(depth):
        w = dict(w_in=w_in[l].astype(BF16), conv_w=conv_w[l], conv_b=conv_b[l],
                 w_ra=w_ra[l].astype(BF16), b_ra=b_ra[l], w_ri=w_ri[l].astype(BF16), b_ri=b_ri[l],
                 lam=rg_lambda[l], w_out=w_out[l].astype(BF16), ln1_g=ln1_g[l], ln1_b=ln1_b[l],
                 w_pq=w_pq[l].astype(BF16),
                 keys=peer_keys[l].astype(BF16).reshape(2 * PEER_HEADS, N_KEYS, -1),
                 peer_u=peer_u[l].astype(BF16), peer_vt=peer_v[l].astype(BF16).T,
                 ln2_g=ln2_g[l], ln2_b=ln2_b[l])
        mod = _ada(c_all, w_ada[l], b_ada[l])
        mods_p = [m[:, None, :] for m in jnp.split(mod[:bsz_p], 6, axis=-1)]
        mods_s = [m[:, None, :] for m in jnp.split(mod[bsz_p:rows], 6, axis=-1)]

        y_p, cs, hs, k, v = _layer(
            y_p, mods_p, jnp.zeros((bsz_p, CONV_W - 1, d), F32), jnp.zeros((bsz_p, d), F32),
            lambda proj: _attn_prompt(proj, rel_bias[l], d), w,
            keep_rows=keep, flatten=False)
        for dst, val in zip(outs[:4], (cs, hs, k, v)):
            dst.append(val)

        y_s, cs, hs, k, v = _layer(
            y_s, mods_s, state_conv[l], state_rnn[l],
            lambda proj: _attn_sample(proj, cache_k[l], cache_v[l], rel_bias[l], d), w,
            keep_rows=t_s, flatten=True)
        for dst, val in zip(outs[4:], (cs, hs, k, v)):
            dst.append(val)
    return (y_p, y_s) + tuple(jnp.stack(o) for o in outs)
```

```python
import functools

import jax
import jax.numpy as jnp
from jax import lax
from jax.experimental import pallas as pl
from jax.experimental.pallas import tpu as pltpu

F32 = jnp.float32
BF16 = jnp.bfloat16

CHUNK = 64
LEFT_CHUNKS = 8
ATT_REACH = LEFT_CHUNKS * CHUNK
MAX_REL = 128
REL_BUCKETS = MAX_REL + CHUNK
PAST_LEN = 1024
CONV_W = 4
RG_C = 8.0
RNN_BLOCK_DIM = 128
HEAD_DIM = 128
N_KEYS = 128
PEER_HEADS = 8
PEER_TOPK = 16
DEPTH = 1
ALPHA = (2.0 * DEPTH) ** 0.25
LN_EPS = 1e-5
NEG_INF = -1e30

LANES = 128
SUBLANES = 8
VMEM_LIMIT = 56 * 1024 * 1024

_CAND_COUNTS = tuple(PEER_TOPK // (r1 + 1) for r1 in range(PEER_TOPK))
_CAND_OFFSETS = tuple(sum(_CAND_COUNTS[:r1]) for r1 in range(PEER_TOPK))
_N_CAND = sum(_CAND_COUNTS)
_N_CAND_PAD = -(-_N_CAND // SUBLANES) * SUBLANES


def _params(n_grid):
    return pltpu.CompilerParams(dimension_semantics=("arbitrary",) * n_grid,
                                vmem_limit_bytes=VMEM_LIMIT)


def _normalize(x):
    mu = jnp.mean(x, axis=-1, keepdims=True)
    xc = x - mu
    var = jnp.mean(xc * xc, axis=-1, keepdims=True)
    return xc * lax.rsqrt(var + LN_EPS)


def _gelu(x):
    return 0.5 * x * (1.0 + lax.erf(x * (2.0 ** -0.5)))


def _ada_kernel(c_ref, w_ref, b_ref, o_ref):
    c = c_ref[...]
    s = c * jax.nn.sigmoid(c)
    o_ref[...] = jnp.dot(s, w_ref[...], preferred_element_type=F32) + b_ref[...]


def _ada(c, w_ada, b_ada):
    rows, d = c.shape
    n = w_ada.shape[1]
    tn = 1024
    return pl.pallas_call(
        _ada_kernel,
        grid=(n // tn,),
        in_specs=[pl.BlockSpec((rows, d), lambda j: (0, 0)),
                  pl.BlockSpec((d, tn), lambda j: (0, j)),
                  pl.BlockSpec((1, tn), lambda j: (0, j))],
        out_specs=pl.BlockSpec((rows, tn), lambda j: (0, j)),
        out_shape=jax.ShapeDtypeStruct((rows, n), F32),
        compiler_params=_params(1),
        name="ada",
    )(c, w_ada, b_ada.reshape(1, n))


def _proj_kernel(x_ref, sh_ref, sc_ref, w_ref, o_ref, h_ref):
    @pl.when(pl.program_id(2) == 0)
    def _():
        h = _normalize(x_ref[...]) * (1.0 + sc_ref[...]) + sh_ref[...]
        h_ref[...] = h.astype(BF16)

    o_ref[...] = jnp.dot(h_ref[...], w_ref[...], preferred_element_type=F32)


def _mod_spec(mod, tm, d):
    if mod.shape[1] == 1:
        return pl.BlockSpec((None, 1, d), lambda b, i, *_: (b, 0, 0))
    return pl.BlockSpec((None, tm, d), lambda b, i, *_: (b, i, 0))


def _proj(x, shift, scale, w_in_bf16, tm):
    bsz, t, d = x.shape
    n = w_in_bf16.shape[1]
    tn = 1024
    return pl.pallas_call(
        _proj_kernel,
        grid=(bsz, t // tm, n // tn),
        in_specs=[pl.BlockSpec((None, tm, d), lambda b, i, j: (b, i, 0)),
                  _mod_spec(shift, tm, d), _mod_spec(scale, tm, d),
                  pl.BlockSpec((d, tn), lambda b, i, j: (0, j))],
        out_specs=pl.BlockSpec((None, tm, tn), lambda b, i, j: (b, i, j)),
        out_shape=jax.ShapeDtypeStruct((bsz, t, n), F32),
        scratch_shapes=[pltpu.VMEM((tm, d), BF16)],
        compiler_params=_params(3),
        name="proj",
    )(x, shift, scale, w_in_bf16)


def _rnn_kernel(xr_ref, yr_ref, ga_ref, cbuf_ref, h0_ref, cw_ref, cb_ref, wra_ref, bra_ref,
                wri_ref, bri_ref, lam_ref, pa_ref, cst_ref, hst_ref,
                xbuf, a_s, u_s, hcar, *, tt, tc):
    t = pl.program_id(2)
    nt = pl.num_programs(2)
    pad = SUBLANES
    keep = CONV_W - 1

    @pl.when(t == 0)
    def _():
        xbuf[pad - keep:pad, :] = cbuf_ref[...]
        hcar[...] = h0_ref[...]

    xbuf[pad:pad + tt, :] = xr_ref[...]
    cw = cw_ref[...]
    xc = cb_ref[...] + cw[0:1, :] * xbuf[pad - keep:pad - keep + tt, :]
    for j in range(1, CONV_W):
        xc = xc + cw[j:j + 1, :] * xbuf[pad - keep + j:pad - keep + j + tt, :]
    tail = xbuf[pad + tt - keep:pad + tt, :]
    xbuf[pad - keep:pad, :] = tail

    lam = lam_ref[...]
    softplus_neg = jnp.maximum(-lam, 0.0) + jnp.log1p(jnp.exp(-jnp.abs(lam)))
    for blk in range(tc // RNN_BLOCK_DIM):
        cols = slice(blk * RNN_BLOCK_DIM, (blk + 1) * RNN_BLOCK_DIM)
        xb = xc[:, cols]
        xb16 = xb.astype(BF16)
        r = jax.nn.sigmoid(jnp.dot(xb16, wra_ref[blk], preferred_element_type=F32) + bra_ref[:, cols])
        i = jax.nn.sigmoid(jnp.dot(xb16, wri_ref[blk], preferred_element_type=F32) + bri_ref[:, cols])
        log_a = -RG_C * r * softplus_neg[:, cols]
        a = jnp.exp(log_a)
        u = jnp.sqrt(jnp.tanh(-log_a) * (1.0 + a * a)) * (i * xb)
        a_s[:, cols] = a
        u_s[:, cols] = u

    row = lax.broadcasted_iota(jnp.int32, (SUBLANES, tc), 0)

    def scan_group(g, h_prev):
        rows = pl.ds(pl.multiple_of(g * SUBLANES, SUBLANES), SUBLANES)
        a = a_s[rows, :]
        u = u_s[rows, :]
        for s in (1, 2, 4):
            a_sh = pltpu.roll(a, s, 0)
            u_sh = pltpu.roll(u, s, 0)
            live = row >= s
            u = jnp.where(live, a * u_sh + u, u)
            a = jnp.where(live, a * a_sh, a)
        h = a * h_prev + u
        u_s[rows, :] = h
        return h[SUBLANES - 1:SUBLANES, :]

    h_last = lax.fori_loop(0, tt // SUBLANES, scan_group, hcar[...])
    hcar[...] = h_last

    pa_ref[...] = (u_s[...] * _gelu(yr_ref[...]) * jax.nn.sigmoid(ga_ref[...])).astype(BF16)

    @pl.when(t == nt - 1)
    def _():
        cst_ref[...] = tail
        hst_ref[...] = h_last


def _rnn(proj, conv_buf, h0, conv_w, conv_b, w_ra16, b_ra, w_ri16, b_ri, lam, d, tt, tc):
    bsz, t, _ = proj.shape
    nc = d // tc
    nb = tc // RNN_BLOCK_DIM
    keep = CONV_W - 1
    col = lambda off: (lambda b, c, i: (b, i, off * nc + c))
    vec = pl.BlockSpec((1, tc), lambda b, c, i: (0, c))
    blk = pl.BlockSpec((nb, RNN_BLOCK_DIM, RNN_BLOCK_DIM), lambda b, c, i: (c, 0, 0))
    pa, cst, hst = pl.pallas_call(
        functools.partial(_rnn_kernel, tt=tt, tc=tc),
        grid=(bsz, nc, t // tt),
        in_specs=[pl.BlockSpec((None, tt, tc), col(0)),
                  pl.BlockSpec((None, tt, tc), col(1)),
                  pl.BlockSpec((None, tt, tc), col(5)),
                  pl.BlockSpec((None, keep, tc), lambda b, c, i: (b, 0, c)),
                  pl.BlockSpec((None, 1, tc), lambda b, c, i: (b, 0, c)),
                  pl.BlockSpec((CONV_W, tc), lambda b, c, i: (0, c)),
                  vec, blk, vec, blk, vec, vec],
        out_specs=[pl.BlockSpec((None, tt, tc), lambda b, c, i: (b, i, c)),
                   pl.BlockSpec((None, keep, tc), lambda b, c, i: (b, 0, c)),
                   pl.BlockSpec((None, 1, tc), lambda b, c, i: (b, 0, c))],
        out_shape=[jax.ShapeDtypeStruct((bsz, t, d), BF16),
                   jax.ShapeDtypeStruct((bsz, keep, d), F32),
                   jax.ShapeDtypeStruct((bsz, 1, d), F32)],
        scratch_shapes=[pltpu.VMEM((tt + SUBLANES, tc), F32),
                        pltpu.VMEM((tt, tc), F32),
                        pltpu.VMEM((tt, tc), F32),
                        pltpu.VMEM((1, tc), F32)],
        compiler_params=_params(3),
        name="rnn",
    )(proj, proj, proj, conv_buf, h0.reshape(bsz, 1, d), conv_w, conv_b.reshape(1, d),
      w_ra16, b_ra.reshape(1, d), w_ri16, b_ri.reshape(1, d), lam.reshape(1, d))
    return pa, cst, hst.reshape(bsz, d)


def _rel_bias_tile(rel_ref, head, dist, visible):
    idx = jnp.clip(dist, -(CHUNK - 1), MAX_REL) + (CHUNK - 1)

    def body(j, acc):
        return acc + jnp.where(idx == j, rel_ref[head, j], 0.0)

    bias = lax.fori_loop(0, REL_BUCKETS, body, jnp.zeros(dist.shape, F32))
    return jnp.where(visible, bias, NEG_INF)


def _softmax_pv(s, v16):
    m = jnp.max(s, axis=-1, keepdims=True)
    p = jnp.exp(s - m)
    l = jnp.sum(p, axis=-1, keepdims=True)
    return jnp.dot(p.astype(BF16), v16, preferred_element_type=F32) / l


_QSUB = 2 * CHUNK
_BAND = _QSUB + ATT_REACH


_ATTN_HEADS = 4


def _attn_prompt_kernel(rel_ref, q_ref, kc_ref, vc_ref, gb_ref, o_ref, bias_s, kcat, vcat):
    group, b, i = pl.program_id(0), pl.program_id(1), pl.program_id(2)
    tq = ATT_REACH

    @pl.when((b == 0) & (i == 0))
    def _():
        r = lax.broadcasted_iota(jnp.int32, (_QSUB, _BAND), 0)
        c = lax.broadcasted_iota(jnp.int32, (_QSUB, _BAND), 1)
        qc = r // CHUNK + LEFT_CHUNKS
        kc = c // CHUNK
        visible = (kc <= qc) & (kc >= qc - LEFT_CHUNKS)
        for hh in range(_ATTN_HEADS):
            bias_s[hh] = _rel_bias_tile(rel_ref, group * _ATTN_HEADS + hh, r + ATT_REACH - c, visible)

    @pl.when(i == 0)
    def _():
        kcat[0:tq, :] = jnp.zeros((tq, kcat.shape[1]), BF16)
        vcat[0:tq, :] = jnp.zeros((tq, vcat.shape[1]), BF16)

    @pl.when(i > 0)
    def _():
        kcat[0:tq, :] = kcat[tq:2 * tq, :]
        vcat[0:tq, :] = vcat[tq:2 * tq, :]

    kcat[tq:2 * tq, :] = kc_ref[...].astype(BF16)
    vcat[tq:2 * tq, :] = vc_ref[...].astype(BF16)
    col = lax.broadcasted_iota(jnp.int32, (_QSUB, _BAND), 1)
    n_sub = tq // _QSUB
    for hh in range(_ATTN_HEADS):
        cols = slice(hh * HEAD_DIM, (hh + 1) * HEAD_DIM)
        scores = []
        for s in range(n_sub):
            q16 = q_ref[s * _QSUB:(s + 1) * _QSUB, cols].astype(BF16)
            sc = lax.dot_general(q16, kcat[s * _QSUB:s * _QSUB + _BAND, cols], (((1,), (1,)), ((), ())),
                                 preferred_element_type=F32) * (HEAD_DIM ** -0.5)
            sc = sc + bias_s[hh]
            k_pos = col + (i * tq + s * _QSUB - ATT_REACH)
            scores.append(jnp.where(k_pos >= 0, sc, NEG_INF))
        maxes = [jnp.max(sc, axis=-1, keepdims=True) for sc in scores]
        probs = [jnp.exp(sc - m) for sc, m in zip(scores, maxes)]
        sums = [jnp.sum(p, axis=-1, keepdims=True) for p in probs]
        outs = [jnp.dot(p.astype(BF16), vcat[s * _QSUB:s * _QSUB + _BAND, cols], preferred_element_type=F32)
                for s, p in enumerate(probs)]
        for s in range(n_sub):
            rows = slice(s * _QSUB, (s + 1) * _QSUB)
            o_ref[rows, cols] = (outs[s] / sums[s] * jax.nn.sigmoid(gb_ref[rows, cols])).astype(BF16)


def _attn_prompt(proj, rel_bias, d):
    bsz, t, _ = proj.shape
    ng = d // (HEAD_DIM * _ATTN_HEADS)
    width = HEAD_DIM * _ATTN_HEADS
    tq = ATT_REACH
    cur = lambda off: (lambda g, b, i: (b, i, off * ng + g))
    blk = lambda f: pl.BlockSpec((None, tq, width), f)
    return pl.pallas_call(
        _attn_prompt_kernel,
        grid=(ng, bsz, t // tq),
        in_specs=[pl.BlockSpec(memory_space=pltpu.SMEM), blk(cur(2)), blk(cur(3)), blk(cur(4)), blk(cur(6))],
        out_specs=pl.BlockSpec((None, tq, width), lambda g, b, i: (b, i, g)),
        out_shape=jax.ShapeDtypeStruct((bsz, t, d), BF16),
        scratch_shapes=[pltpu.VMEM((_ATTN_HEADS, _QSUB, _BAND), F32),
                        pltpu.VMEM((2 * tq, width), BF16),
                        pltpu.VMEM((2 * tq, width), BF16)],
        compiler_params=_params(3),
        name="attn_prompt",
    )(rel_bias, proj, proj, proj, proj)


def _attn_sample_kernel(rel_ref, q_ref, kn_ref, vn_ref, gb_ref, ck_ref, cv_ref, o_ref,
                        bias_past_s, bias_new_s, *, t, n_past):
    group = pl.program_id(0)

    def bias(head, n_keys, first_pos):
        r = lax.broadcasted_iota(jnp.int32, (t, n_keys), 0)
        c = lax.broadcasted_iota(jnp.int32, (t, n_keys), 1)
        q_pos = PAST_LEN + r
        k_pos = first_pos + c
        qc = q_pos // CHUNK
        kc = k_pos // CHUNK
        visible = (k_pos >= 0) & (kc <= qc) & (kc >= qc - LEFT_CHUNKS)
        return _rel_bias_tile(rel_ref, head, q_pos - k_pos, visible)

    @pl.when(pl.program_id(1) == 0)
    def _():
        for hh in range(_ATTN_HEADS):
            bias_past_s[hh] = bias(group * _ATTN_HEADS + hh, n_past, PAST_LEN - n_past)
            bias_new_s[hh] = bias(group * _ATTN_HEADS + hh, t, PAST_LEN)

    scale = HEAD_DIM ** -0.5
    nt = (((1,), (1,)), ((), ()))
    for hh in range(_ATTN_HEADS):
        cols = slice(hh * HEAD_DIM, (hh + 1) * HEAD_DIM)
        q16 = q_ref[:, cols].astype(BF16)
        s_past = lax.dot_general(q16, ck_ref[:, cols].astype(BF16), nt, preferred_element_type=F32) * scale
        s_new = lax.dot_general(q16, kn_ref[:, cols].astype(BF16), nt, preferred_element_type=F32) * scale
        s_past = s_past + bias_past_s[hh]
        s_new = s_new + bias_new_s[hh]
        m = jnp.maximum(jnp.max(s_past, axis=-1, keepdims=True), jnp.max(s_new, axis=-1, keepdims=True))
        p_past = jnp.exp(s_past - m)
        p_new = jnp.exp(s_new - m)
        l = jnp.sum(p_past, axis=-1, keepdims=True) + jnp.sum(p_new, axis=-1, keepdims=True)
        o = (jnp.dot(p_past.astype(BF16), cv_ref[:, cols].astype(BF16), preferred_element_type=F32)
             + jnp.dot(p_new.astype(BF16), vn_ref[:, cols].astype(BF16), preferred_element_type=F32)) / l
        o_ref[:, cols] = (o * jax.nn.sigmoid(gb_ref[:, cols])).astype(BF16)


def _attn_sample(proj, cache_k, cache_v, rel_bias, d):
    bsz, t, _ = proj.shape
    width = HEAD_DIM * _ATTN_HEADS
    ng = d // width
    n_past = cache_k.shape[1]
    new = lambda off: pl.BlockSpec((None, t, width), lambda g, b: (b, 0, off * ng + g))
    past = pl.BlockSpec((None, n_past, width), lambda g, b: (b, 0, g))
    return pl.pallas_call(
        functools.partial(_attn_sample_kernel, t=t, n_past=n_past),
        grid=(ng, bsz),
        in_specs=[pl.BlockSpec(memory_space=pltpu.SMEM), new(2), new(3), new(4), new(6), past, past],
        out_specs=pl.BlockSpec((None, t, width), lambda g, b: (b, 0, g)),
        out_shape=jax.ShapeDtypeStruct((bsz, t, d), BF16),
        scratch_shapes=[pltpu.VMEM((_ATTN_HEADS, t, n_past), F32), pltpu.VMEM((_ATTN_HEADS, t, t), F32)],
        compiler_params=_params(2),
        name="attn_sample",
    )(rel_bias, proj, proj, proj, proj, cache_k.reshape(bsz, n_past, d), cache_v.reshape(bsz, n_past, d))


def _mix_kernel(pa_ref, pb_ref, x_ref, w_ref, g1_ref, sh2_ref, sc2_ref, lg_ref, lb_ref,
                x1_ref, h2_ref):
    mixed = jnp.dot(pa_ref[...] + pb_ref[...], w_ref[...], preferred_element_type=F32)
    x1 = _normalize(ALPHA * x_ref[...] + g1_ref[...] * mixed) * lg_ref[...] + lb_ref[...]
    x1_ref[...] = x1
    h2_ref[...] = (_normalize(x1) * (1.0 + sc2_ref[...]) + sh2_ref[...]).astype(BF16)


def _mix(pa, pb, x, w_out16, gate1, shift2, scale2, ln_g, ln_b, tm):
    bsz, t, d = x.shape
    tok = pl.BlockSpec((None, tm, d), lambda b, i: (b, i, 0))
    vec = pl.BlockSpec((1, d), lambda b, i: (0, 0))
    return pl.pallas_call(
        _mix_kernel,
        grid=(bsz, t // tm),
        in_specs=[tok, tok, tok, pl.BlockSpec((d, d), lambda b, i: (0, 0)),
                  _mod_spec(gate1, tm, d), _mod_spec(shift2, tm, d), _mod_spec(scale2, tm, d), vec, vec],
        out_specs=[tok, tok],
        out_shape=[jax.ShapeDtypeStruct((bsz, t, d), F32), jax.ShapeDtypeStruct((bsz, t, d), BF16)],
        compiler_params=_params(2),
        name="mix",
    )(pa, pb, x, w_out16, gate1, shift2, scale2, ln_g.reshape(1, d), ln_b.reshape(1, d))


def _take_top(s, n_iter, rows_f, on_pick):
    def body(it, s_rem):
        m = jnp.max(s_rem, axis=0, keepdims=True)
        first = jnp.min(jnp.where(s_rem == m, rows_f, float(s.shape[0])), axis=0, keepdims=True)
        pick = rows_f == first
        on_pick(it, m, pick)
        return jnp.where(pick, -jnp.inf, s_rem)

    lax.fori_loop(0, n_iter, body, s)


def _fill_candidates(vals_s, cand_s):
    cand_s[...] = jnp.full(cand_s.shape, -jnp.inf, F32)
    for r1 in range(PEER_TOPK):
        off, cnt = _CAND_OFFSETS[r1], _CAND_COUNTS[r1]
        cand_s[off:off + cnt, :] = vals_s[0, r1:r1 + 1, :] + vals_s[1, 0:cnt, :]


def _route_exact(h, score_s, rank_s, vals_s, cand_s, sel_s, lcol_ref, e1_ref, r2_ref, e2_ref):
    tm = score_s.shape[-1]
    rows_f = lax.broadcasted_iota(jnp.int32, (N_KEYS, tm), 0).astype(F32)
    crow_f = lax.broadcasted_iota(jnp.int32, (_N_CAND_PAD, tm), 0).astype(F32)
    for p in range(2):
        rank_s[p] = jnp.full((N_KEYS, tm), float(PEER_TOPK), F32)

        def on_pick(it, m, pick, p=p):
            vals_s[p, pl.ds(it, 1), :] = m
            rank_s[p] = jnp.where(pick, jnp.asarray(it, F32), rank_s[p])

        _take_top(score_s[p], PEER_TOPK, rows_f, on_pick)

    _fill_candidates(vals_s, cand_s)
    sel_s[...] = jnp.zeros((_N_CAND_PAD, tm), F32)

    def on_pick_cand(it, m, pick):
        sel_s[...] = jnp.where(pick, 1.0, sel_s[...])

    _take_top(cand_s[...], PEER_TOPK, crow_f, on_pick_cand)

    sel = sel_s[...]
    top = vals_s[0, 0:1, :] + vals_s[1, 0:1, :]
    z = jnp.sum(jnp.where(sel > 0.0, jnp.exp(cand_s[...] - top), 0.0), axis=0, keepdims=True)
    rank1 = rank_s[0]
    lcol = jnp.zeros((N_KEYS, tm), F32)
    for r1 in range(PEER_TOPK):
        off, cnt = _CAND_OFFSETS[r1], _CAND_COUNTS[r1]
        n_sel = jnp.sum(sel[off:off + cnt, :], axis=0, keepdims=True)
        lcol = lcol + jnp.where(rank1 == float(r1), n_sel, 0.0)
    lcol_ref[h] = lcol
    e1_ref[h] = jnp.exp(score_s[0] - vals_s[0, 0:1, :]) / z
    r2_ref[h] = rank_s[1].astype(BF16)
    e2_ref[h] = jnp.exp(score_s[1] - vals_s[1, 0:1, :]).astype(BF16)


def _compare_exchange(xs, i, l, descending):
    hi, lo = jnp.maximum(xs[i], xs[l]), jnp.minimum(xs[i], xs[l])
    xs[i], xs[l] = (hi, lo) if descending else (lo, hi)


def _bitonic_merge(xs):
    xs = list(xs)
    j = len(xs) // 2
    while j >= 1:
        for i in range(len(xs)):
            if i & j == 0:
                _compare_exchange(xs, i, i | j, True)
        j //= 2
    return xs


def _bitonic_sort(xs):
    xs = list(xs)
    k = 2
    while k <= len(xs):
        j = k // 2
        while j >= 1:
            for i in range(len(xs)):
                if i & j == 0:
                    _compare_exchange(xs, i, i | j, i & k == 0)
            j //= 2
        k *= 2
    return xs


def _sorted_top(pieces):
    xs = _bitonic_sort(pieces)
    for shift in (1, 2, 4):
        other = [pltpu.roll(x, shift, 0) for x in xs][::-1]
        if len(xs) < PEER_TOPK:
            xs = _bitonic_merge(xs + other)
        else:
            xs = _bitonic_merge([jnp.maximum(a, b) for a, b in zip(xs, other)])
    return xs


def _sublane_sum(x):
    for shift in (4, 2, 1):
        x = x + pltpu.roll(x, shift, 0)
    return x


def _route_fast(h, score_s, vals_s, cand_s, lcol_ref, e1_ref, r2_ref, e2_ref):
    tm = score_s.shape[-1]
    n_slab = N_KEYS // SUBLANES
    slabs = [[score_s[p, v * SUBLANES:(v + 1) * SUBLANES, :] for v in range(n_slab)] for p in range(2)]
    tops = [_sorted_top(slabs[p]) for p in range(2)]
    for p in range(2):
        for j in range(PEER_TOPK):
            vals_s[p, j:j + 1, :] = tops[p][j][0:1, :]
    _fill_candidates(vals_s, cand_s)
    cands = [cand_s[k * SUBLANES:(k + 1) * SUBLANES, :] for k in range(_N_CAND_PAD // SUBLANES)]
    pad = [jnp.full((SUBLANES, tm), -jnp.inf, F32)] * (SUBLANES - len(cands))
    tau = _sorted_top(cands + pad)[PEER_TOPK - 1]
    top = tops[0][0] + tops[1][0]
    n_picked = jnp.zeros((SUBLANES, tm), F32)
    z = jnp.zeros((SUBLANES, tm), F32)
    for c in cands:
        n_picked = n_picked + jnp.where(c >= tau, 1.0, 0.0)
        z = z + jnp.where(c >= tau, jnp.exp(c - top), 0.0)
    inv_z = 1.0 / _sublane_sum(z)
    tie = jnp.where(_sublane_sum(n_picked) != float(PEER_TOPK), 1.0, 0.0)
    for p in range(2):
        n_ge = jnp.zeros((SUBLANES, tm), F32)
        for s in slabs[p]:
            n_ge = n_ge + jnp.where(s >= tops[p][PEER_TOPK - 1], 1.0, 0.0)
        tie = jnp.maximum(tie, jnp.where(_sublane_sum(n_ge) != float(PEER_TOPK), 1.0, 0.0))
        for j in range(PEER_TOPK - 1):
            tie = jnp.maximum(tie, jnp.where(tops[p][j] == tops[p][j + 1], 1.0, 0.0))
    n_pick = []
    for r1 in range(PEER_TOPK):
        n = jnp.zeros((SUBLANES, tm), F32)
        for r2 in range(_CAND_COUNTS[r1]):
            n = n + jnp.where(tops[0][r1] + tops[1][r2] >= tau, 1.0, 0.0)
        n_pick.append(n)
    rank2, e2 = [], []
    for v in range(n_slab):
        rows = slice(v * SUBLANES, (v + 1) * SUBLANES)
        lc = jnp.zeros((SUBLANES, tm), F32)
        rk = jnp.full((SUBLANES, tm), float(PEER_TOPK), F32)
        for j in range(PEER_TOPK):
            lc = jnp.where(slabs[0][v] == tops[0][j], n_pick[j], lc)
            rk = jnp.where(slabs[1][v] == tops[1][j], float(j), rk)
        lcol_ref[h, rows, :] = lc
        e1_ref[h, rows, :] = jnp.exp(slabs[0][v] - tops[0][0]) * inv_z
        rank2.append(rk)
        e2.append(jnp.exp(slabs[1][v] - tops[1][0]))
    for u in range(n_slab // 2):
        rows = slice(2 * u * SUBLANES, (2 * u + 2) * SUBLANES)
        r2_ref[h, rows, :] = jnp.concatenate(rank2[2 * u:2 * u + 2], axis=0).astype(BF16)
        e2_ref[h, rows, :] = jnp.concatenate(e2[2 * u:2 * u + 2], axis=0).astype(BF16)
    return jnp.max(tie)


def _route_kernel(h2_ref, wpq_ref, keys_ref, lcol_ref, e1_ref, r2_ref, e2_ref,
                  q_s, score_s, rank_s, vals_s, cand_s, sel_s):
    q = jnp.dot(h2_ref[...], wpq_ref[...], preferred_element_type=F32)
    for hp in range(2 * PEER_HEADS):
        q_s[hp] = q[:, hp * N_KEYS:(hp + 1) * N_KEYS].astype(BF16)
    nt = (((1,), (1,)), ((), ()))

    def per_head(h, carry):
        for p in range(2):
            score_s[p] = lax.dot_general(keys_ref[h * 2 + p], q_s[h * 2 + p], nt,
                                         preferred_element_type=F32)
        tie = _route_fast(h, score_s, vals_s, cand_s, lcol_ref, e1_ref, r2_ref, e2_ref)

        @pl.when(tie > 0.0)
        def _():
            _route_exact(h, score_s, rank_s, vals_s, cand_s, sel_s, lcol_ref, e1_ref, r2_ref, e2_ref)

        return carry

    lax.fori_loop(0, PEER_HEADS, per_head, 0)


def _route(h2, w_pq16, keys16, tm):
    bsz, t, d = h2.shape
    nq = w_pq16.shape[1]
    out = pl.BlockSpec((None, PEER_HEADS, N_KEYS, tm), lambda b, i: (b, 0, 0, i))
    shape = jax.ShapeDtypeStruct((bsz, PEER_HEADS, N_KEYS, t), F32)
    shape16 = jax.ShapeDtypeStruct((bsz, PEER_HEADS, N_KEYS, t), BF16)
    return pl.pallas_call(
        _route_kernel,
        grid=(bsz, t // tm),
        in_specs=[pl.BlockSpec((None, tm, d), lambda b, i: (b, i, 0)),
                  pl.BlockSpec((d, nq), lambda b, i: (0, 0)),
                  pl.BlockSpec((2 * PEER_HEADS, N_KEYS, nq // (2 * PEER_HEADS)), lambda b, i: (0, 0, 0))],
        out_specs=[out, out, out, out],
        out_shape=[shape, shape, shape16, shape16],
        scratch_shapes=[pltpu.VMEM((2 * PEER_HEADS, tm, nq // (2 * PEER_HEADS)), BF16),
                        pltpu.VMEM((2, N_KEYS, tm), F32),
                        pltpu.VMEM((2, N_KEYS, tm), F32),
                        pltpu.VMEM((2, PEER_TOPK, tm), F32),
                        pltpu.VMEM((_N_CAND_PAD, tm), F32),
                        pltpu.VMEM((_N_CAND_PAD, tm), F32)],
        compiler_params=_params(2),
        name="route",
    )(h2, w_pq16, keys16)


_ACT_ROWS = 512


def _peer_kernel(h2_ref, u_ref, vt_ref, lcol_ref, e1_ref, r2_ref, e2_ref, x1_ref, g2_ref,
                 lg_ref, lb_ref, y_ref, acc, p_s, *, te):
    j = pl.program_id(2)

    @pl.when(j == 0)
    def _():
        acc[...] = jnp.zeros(acc.shape, F32)

    zero = jnp.zeros((), BF16)
    n_sub = _ACT_ROWS // N_KEYS
    for c in range(te // _ACT_ROWS):
        chunk = slice(c * _ACT_ROWS, (c + 1) * _ACT_ROWS)
        act = lax.dot_general(u_ref[chunk, :], h2_ref[...], (((1,), (1,)), ((), ())),
                              preferred_element_type=F32)
        for sub in range(n_sub):
            al = c * n_sub + sub
            w = None
            for h in range(PEER_HEADS):
                keep = r2_ref[h] < lcol_ref[h, al:al + 1, :].astype(BF16)
                term = jnp.where(keep, e2_ref[h] * e1_ref[h, al:al + 1, :].astype(BF16), zero)
                w = term if w is None else w + term
            g = _gelu(act[sub * N_KEYS:(sub + 1) * N_KEYS, :])
            p_s[al * N_KEYS:(al + 1) * N_KEYS, :] = (w.astype(F32) * g).astype(BF16)
    acc[...] += jnp.dot(vt_ref[...], p_s[...], preferred_element_type=F32)

    @pl.when(j == pl.num_programs(2) - 1)
    def _():
        ffn = acc[...].T
        y = _normalize(ALPHA * x1_ref[...] + g2_ref[...] * ffn)
        y_ref[...] = y * lg_ref[...] + lb_ref[...]


def _peer(h2, u16, vt16, lcol, e1, r2, e2, x1, gate2, ln_g, ln_b, tm, te):
    bsz, t, d = h2.shape
    n_exp = u16.shape[0]
    na = te // N_KEYS
    tok = pl.BlockSpec((None, tm, d), lambda b, i, j: (b, i, 0))
    tok1 = pl.BlockSpec((None, tm, d), lambda b, i, j: (b, i, 0), pipeline_mode=pl.Buffered(1))
    vec = pl.BlockSpec((1, d), lambda b, i, j: (0, 0))
    by_a = pl.BlockSpec((None, PEER_HEADS, na, tm), lambda b, i, j: (b, 0, j, i))
    by_b = pl.BlockSpec((None, PEER_HEADS, N_KEYS, tm), lambda b, i, j: (b, 0, 0, i))
    return pl.pallas_call(
        functools.partial(_peer_kernel, te=te),
        grid=(bsz, t // tm, n_exp // te),
        in_specs=[tok,
                  pl.BlockSpec((te, d), lambda b, i, j: (j, 0)),
                  pl.BlockSpec((d, te), lambda b, i, j: (0, j)),
                  by_a, by_a, by_b, by_b, tok, _mod_spec(gate2, tm, d), vec, vec],
        out_specs=tok1,
        out_shape=jax.ShapeDtypeStruct((bsz, t, d), F32),
        scratch_shapes=[pltpu.VMEM((d, tm), F32), pltpu.VMEM((te, tm), BF16)],
        compiler_params=_params(3),
        name="peer",
    )(h2, u16, vt16, lcol, e1, r2, e2, x1, gate2, ln_g.reshape(1, d), ln_b.reshape(1, d))


def _tiles(n_tok, t):
    return dict(tm_proj=min(1024, n_tok),
                tm_mix=min(512, n_tok),
                tt=min(512, t), tc=512,
                tm_route=min(512, n_tok),
                tm_peer=min(512, n_tok), te=1024)


def _layer(x, mods, conv_buf, h0, attend, w, *, keep_rows, flatten):
    bsz, t, d = x.shape
    tl = _tiles(bsz * t if flatten else t, t)
    if flatten:
        mods = [jnp.broadcast_to(m, (bsz, t, d)).reshape(1, bsz * t, d) for m in mods]
        flat = lambda a: a.reshape(1, bsz * t, a.shape[-1])
    else:
        flat = lambda a: a
    shift1, scale1, gate1, shift2, scale2, gate2 = mods
    proj = _proj(flat(x), shift1, scale1, w["w_in"], tl["tm_proj"]).reshape(bsz, t, -1)
    pa, conv_state, rnn_state = _rnn(proj, conv_buf, h0, w["conv_w"], w["conv_b"], w["w_ra"], w["b_ra"],
                                     w["w_ri"], w["b_ri"], w["lam"], d, tl["tt"], tl["tc"])
    pb = attend(proj)
    x1, h2 = _mix(flat(pa), flat(pb), flat(x), w["w_out"], gate1, shift2, scale2, w["ln1_g"], w["ln1_b"],
                  tl["tm_mix"])
    lcol, e1, r2, e2 = _route(h2, w["w_pq"], w["keys"], tl["tm_route"])
    y = _peer(h2, w["peer_u"], w["peer_vt"], lcol, e1, r2, e2, x1, gate2, w["ln2_g"], w["ln2_b"],
              tl["tm_peer"], tl["te"])
    n_heads = d // HEAD_DIM
    k = proj[:, t - keep_rows:, 3 * d:4 * d].reshape(bsz, keep_rows, n_heads, HEAD_DIM)
    v = proj[:, t - keep_rows:, 4 * d:5 * d].reshape(bsz, keep_rows, n_heads, HEAD_DIM)
    return y.reshape(bsz, t, d), conv_state, rnn_state, k, v


def kernel(x_prompt, x_sample, c_prompt, c_sample, state_conv, state_rnn, cache_k, cache_v, w_ada, b_ada, w_in, conv_w, conv_b, w_ra, b_ra, w_ri, b_ri, rg_lambda, rel_bias, w_out, ln1_g, ln1_b, w_pq, peer_keys, peer_u, peer_v, ln2_g, ln2_b):
    bsz_p, t_p, d = x_prompt.shape
    bsz_s, t_s, _ = x_sample.shape
    depth = w_ada.shape[0]
    keep = min(ATT_REACH, t_p)
    y_p, y_s = x_prompt, x_sample
    outs = [[] for _ in range(8)]
    rows = bsz_p + bsz_s
    rows_pad = -(-rows // SUBLANES) * SUBLANES
    c_all = jnp.concatenate([c_prompt, c_sample, jnp.zeros((rows_pad - rows, d), F32)], axis=0)
    for l in range(depth):
        w = dict(w_in=w_in[l].astype(BF16), conv_w=conv_w[l], conv_b=conv_b[l],
                 w_ra=w_ra[l].astype(BF16), b_ra=b_ra[l], w_ri=w_ri[l].astype(BF16), b_ri=b_ri[l],
                 lam=rg_lambda[l], w_out=w_out[l].astype(BF16), ln1_g=ln1_g[l], ln1_b=ln1_b[l],
                 w_pq=w_pq[l].astype(BF16),
                 keys=peer_keys[l].astype(BF16).reshape(2 * PEER_HEADS, N_KEYS, -1),
                 peer_u=peer_u[l].astype(BF16), peer_vt=peer_v[l].astype(BF16).T,
                 ln2_g=ln2_g[l], ln2_b=ln2_b[l])
        mod = _ada(c_all, w_ada[l], b_ada[l])
        mods_p = [m[:, None, :] for m in jnp.split(mod[:bsz_p], 6, axis=-1)]
        mods_s = [m[:, None, :] for m in jnp.split(mod[bsz_p:rows], 6, axis=-1)]

        y_p, cs, hs, k, v = _layer(
            y_p, mods_p, jnp.zeros((bsz_p, CONV_W - 1, d), F32), jnp.zeros((bsz_p, d), F32),
            lambda proj: _attn_prompt(proj, rel_bias[l], d), w,
            keep_rows=keep, flatten=False)
        for dst, val in zip(outs[:4], (cs, hs, k, v)):
            dst.append(val)

        y_s, cs, hs, k, v = _layer(
            y_s, mods_s, state_conv[l], state_rnn[l],
            lambda proj: _attn_sample(proj, cache_k[l], cache_v[l], rel_bias[l], d), w,
            keep_rows=t_s, flatten=True)
        for dst, val in zip(outs[4:], (cs, hs, k, v)):
            dst.append(val)
    return (y_p, y_s) + tuple(jnp.stack(o) for o in outs)
```

```python
import functools

import jax
import jax.numpy as jnp
from jax import lax
from jax.experimental import pallas as pl
from jax.experimental.pallas import tpu as pltpu

F32 = jnp.float32
BF16 = jnp.bfloat16

CHUNK = 64
LEFT_CHUNKS = 8
ATT_REACH = LEFT_CHUNKS * CHUNK
MAX_REL = 128
REL_BUCKETS = MAX_REL + CHUNK
PAST_LEN = 1024
CONV_W = 4
RG_C = 8.0
RNN_BLOCK_DIM = 128
HEAD_DIM = 128
N_KEYS = 128
PEER_HEADS = 8
PEER_TOPK = 16
DEPTH = 1
ALPHA = (2.0 * DEPTH) ** 0.25
LN_EPS = 1e-5
NEG_INF = -1e30

SUBLANES = 8
VMEM_LIMIT = 56 * 1024 * 1024

_CAND_COUNTS = tuple(PEER_TOPK // (r1 + 1) for r1 in range(PEER_TOPK))
_CAND_OFFSETS = tuple(sum(_CAND_COUNTS[:r1]) for r1 in range(PEER_TOPK))
_N_CAND = sum(_CAND_COUNTS)
_N_CAND_PAD = -(-_N_CAND // SUBLANES) * SUBLANES


def _params(n_grid):
    return pltpu.CompilerParams(dimension_semantics=("arbitrary",) * n_grid,
                                vmem_limit_bytes=VMEM_LIMIT)


def _normalize(x):
    mu = jnp.mean(x, axis=-1, keepdims=True)
    xc = x - mu
    var = jnp.mean(xc * xc, axis=-1, keepdims=True)
    return xc * lax.rsqrt(var + LN_EPS)


def _gelu(x):
    return 0.5 * x * (1.0 + lax.erf(x * (2.0 ** -0.5)))


def _ada_kernel(c_ref, w_ref, b_ref, o_ref):
    c = c_ref[...]
    s = c * jax.nn.sigmoid(c)
    o_ref[...] = jnp.dot(s, w_ref[...], preferred_element_type=F32) + b_ref[...]


def _ada(c, w_ada, b_ada):
    rows, d = c.shape
    n = w_ada.shape[1]
    tn = 1024
    return pl.pallas_call(
        _ada_kernel,
        grid=(n // tn,),
        in_specs=[pl.BlockSpec((rows, d), lambda j: (0, 0)),
                  pl.BlockSpec((d, tn), lambda j: (0, j)),
                  pl.BlockSpec((1, tn), lambda j: (0, j))],
        out_specs=pl.BlockSpec((rows, tn), lambda j: (0, j)),
        out_shape=jax.ShapeDtypeStruct((rows, n), F32),
        compiler_params=_params(1),
        name="ada",
    )(c, w_ada, b_ada.reshape(1, n))


def _proj_kernel(x_ref, sh_ref, sc_ref, w_ref, o_ref, h_ref):
    @pl.when(pl.program_id(2) == 0)
    def _():
        h = _normalize(x_ref[...]) * (1.0 + sc_ref[...]) + sh_ref[...]
        h_ref[...] = h.astype(BF16)

    o_ref[...] = jnp.dot(h_ref[...], w_ref[...], preferred_element_type=F32)


def _mod_spec(mod, tm, d):
    if mod.shape[1] == 1:
        return pl.BlockSpec((None, 1, d), lambda b, i, *_: (b, 0, 0))
    return pl.BlockSpec((None, tm, d), lambda b, i, *_: (b, i, 0))


def _proj(x, shift, scale, w_in_bf16, tm):
    bsz, t, d = x.shape
    n = w_in_bf16.shape[1]
    tn = 1024
    return pl.pallas_call(
        _proj_kernel,
        grid=(bsz, t // tm, n // tn),
        in_specs=[pl.BlockSpec((None, tm, d), lambda b, i, j: (b, i, 0)),
                  _mod_spec(shift, tm, d), _mod_spec(scale, tm, d),
                  pl.BlockSpec((d, tn), lambda b, i, j: (0, j))],
        out_specs=pl.BlockSpec((None, tm, tn), lambda b, i, j: (b, i, j)),
        out_shape=jax.ShapeDtypeStruct((bsz, t, n), F32),
        scratch_shapes=[pltpu.VMEM((tm, d), BF16)],
        compiler_params=_params(3),
        name="proj",
    )(x, shift, scale, w_in_bf16)


def _rnn_kernel(xr_ref, yr_ref, ga_ref, cbuf_ref, h0_ref, cw_ref, cb_ref, wra_ref, bra_ref,
                wri_ref, bri_ref, lam_ref, pa_ref, cst_ref, hst_ref,
                xbuf, a_s, u_s, hcar, *, tt, tc):
    t = pl.program_id(2)
    nt = pl.num_programs(2)
    pad = SUBLANES
    keep = CONV_W - 1

    @pl.when(t == 0)
    def _():
        xbuf[pad - keep:pad, :] = cbuf_ref[...]
        hcar[...] = h0_ref[...]

    xbuf[pad:pad + tt, :] = xr_ref[...]
    cw = cw_ref[...]
    xc = cb_ref[...] + cw[0:1, :] * xbuf[pad - keep:pad - keep + tt, :]
    for j in range(1, CONV_W):
        xc = xc + cw[j:j + 1, :] * xbuf[pad - keep + j:pad - keep + j + tt, :]
    tail = xbuf[pad + tt - keep:pad + tt, :]
    xbuf[pad - keep:pad, :] = tail

    lam = lam_ref[...]
    softplus_neg = jnp.maximum(-lam, 0.0) + jnp.log1p(jnp.exp(-jnp.abs(lam)))
    for blk in range(tc // RNN_BLOCK_DIM):
        cols = slice(blk * RNN_BLOCK_DIM, (blk + 1) * RNN_BLOCK_DIM)
        xb = xc[:, cols]
        xb16 = xb.astype(BF16)
        r = jax.nn.sigmoid(jnp.dot(xb16, wra_ref[blk], preferred_element_type=F32) + bra_ref[:, cols])
        i = jax.nn.sigmoid(jnp.dot(xb16, wri_ref[blk], preferred_element_type=F32) + bri_ref[:, cols])
        log_a = -RG_C * r * softplus_neg[:, cols]
        a = jnp.exp(log_a)
        u = jnp.sqrt(jnp.tanh(-log_a) * (1.0 + a * a)) * (i * xb)
        a_s[:, cols] = a
        u_s[:, cols] = u

    row = lax.broadcasted_iota(jnp.int32, (SUBLANES, tc), 0)

    def scan_group(g, h_prev):
        rows = pl.ds(pl.multiple_of(g * SUBLANES, SUBLANES), SUBLANES)
        a = a_s[rows, :]
        u = u_s[rows, :]
        for s in (1, 2, 4):
            a_sh = pltpu.roll(a, s, 0)
            u_sh = pltpu.roll(u, s, 0)
            live = row >= s
            u = jnp.where(live, a * u_sh + u, u)
            a = jnp.where(live, a * a_sh, a)
        h = a * h_prev + u
        u_s[rows, :] = h
        return h[SUBLANES - 1:SUBLANES, :]

    h_last = lax.fori_loop(0, tt // SUBLANES, scan_group, hcar[...])
    hcar[...] = h_last

    pa_ref[...] = (u_s[...] * _gelu(yr_ref[...]) * jax.nn.sigmoid(ga_ref[...])).astype(BF16)

    @pl.when(t == nt - 1)
    def _():
        cst_ref[...] = tail
        hst_ref[...] = h_last


def _rnn(proj, conv_buf, h0, conv_w, conv_b, w_ra16, b_ra, w_ri16, b_ri, lam, d, tt, tc):
    bsz, t, _ = proj.shape
    nc = d // tc
    nb = tc // RNN_BLOCK_DIM
    keep = CONV_W - 1
    col = lambda off: (lambda b, c, i: (b, i, off * nc + c))
    vec = pl.BlockSpec((1, tc), lambda b, c, i: (0, c))
    blk = pl.BlockSpec((nb, RNN_BLOCK_DIM, RNN_BLOCK_DIM), lambda b, c, i: (c, 0, 0))
    pa, cst, hst = pl.pallas_call(
        functools.partial(_rnn_kernel, tt=tt, tc=tc),
        grid=(bsz, nc, t // tt),
        in_specs=[pl.BlockSpec((None, tt, tc), col(0)),
                  pl.BlockSpec((None, tt, tc), col(1)),
                  pl.BlockSpec((None, tt, tc), col(5)),
                  pl.BlockSpec((None, keep, tc), lambda b, c, i: (b, 0, c)),
                  pl.BlockSpec((None, 1, tc), lambda b, c, i: (b, 0, c)),
                  pl.BlockSpec((CONV_W, tc), lambda b, c, i: (0, c)),
                  vec, blk, vec, blk, vec, vec],
        out_specs=[pl.BlockSpec((None, tt, tc), lambda b, c, i: (b, i, c)),
                   pl.BlockSpec((None, keep, tc), lambda b, c, i: (b, 0, c)),
                   pl.BlockSpec((None, 1, tc), lambda b, c, i: (b, 0, c))],
        out_shape=[jax.ShapeDtypeStruct((bsz, t, d), BF16),
                   jax.ShapeDtypeStruct((bsz, keep, d), F32),
                   jax.ShapeDtypeStruct((bsz, 1, d), F32)],
        scratch_shapes=[pltpu.VMEM((tt + SUBLANES, tc), F32),
                        pltpu.VMEM((tt, tc), F32),
                        pltpu.VMEM((tt, tc), F32),
                        pltpu.VMEM((1, tc), F32)],
        compiler_params=_params(3),
        name="rnn",
    )(proj, proj, proj, conv_buf, h0.reshape(bsz, 1, d), conv_w, conv_b.reshape(1, d),
      w_ra16, b_ra.reshape(1, d), w_ri16, b_ri.reshape(1, d), lam.reshape(1, d))
    return pa, cst, hst.reshape(bsz, d)


def _rel_bias_tile(rel_ref, head, dist, visible):
    idx = jnp.clip(dist, -(CHUNK - 1), MAX_REL) + (CHUNK - 1)

    def body(j, acc):
        return acc + jnp.where(idx == j, rel_ref[head, j], 0.0)

    bias = lax.fori_loop(0, REL_BUCKETS, body, jnp.zeros(dist.shape, F32))
    return jnp.where(visible, bias, NEG_INF)


_QSUB = 2 * CHUNK
_BAND = _QSUB + ATT_REACH
_ATTN_HEADS = 4


def _attn_prompt_kernel(rel_ref, q_ref, kc_ref, vc_ref, gb_ref, o_ref, bias_s, kcat, vcat):
    group, b, i = pl.program_id(0), pl.program_id(1), pl.program_id(2)
    tq = ATT_REACH

    @pl.when((b == 0) & (i == 0))
    def _():
        r = lax.broadcasted_iota(jnp.int32, (_QSUB, _BAND), 0)
        c = lax.broadcasted_iota(jnp.int32, (_QSUB, _BAND), 1)
        qc = r // CHUNK + LEFT_CHUNKS
        kc = c // CHUNK
        visible = (kc <= qc) & (kc >= qc - LEFT_CHUNKS)
        for hh in range(_ATTN_HEADS):
            bias_s[hh] = _rel_bias_tile(rel_ref, group * _ATTN_HEADS + hh, r + ATT_REACH - c, visible)

    @pl.when(i == 0)
    def _():
        kcat[0:tq, :] = jnp.zeros((tq, kcat.shape[1]), BF16)
        vcat[0:tq, :] = jnp.zeros((tq, vcat.shape[1]), BF16)

    @pl.when(i > 0)
    def _():
        kcat[0:tq, :] = kcat[tq:2 * tq, :]
        vcat[0:tq, :] = vcat[tq:2 * tq, :]

    kcat[tq:2 * tq, :] = kc_ref[...].astype(BF16)
    vcat[tq:2 * tq, :] = vc_ref[...].astype(BF16)
    col = lax.broadcasted_iota(jnp.int32, (_QSUB, _BAND), 1)
    n_sub = tq // _QSUB
    for hh in range(_ATTN_HEADS):
        cols = slice(hh * HEAD_DIM, (hh + 1) * HEAD_DIM)
        scores = []
        for s in range(n_sub):
            q16 = q_ref[s * _QSUB:(s + 1) * _QSUB, cols].astype(BF16)
            sc = lax.dot_general(q16, kcat[s * _QSUB:s * _QSUB + _BAND, cols], (((1,), (1,)), ((), ())),
                                 preferred_element_type=F32) * (HEAD_DIM ** -0.5)
            sc = sc + bias_s[hh]
            k_pos = col + (i * tq + s * _QSUB - ATT_REACH)
            scores.append(jnp.where(k_pos >= 0, sc, NEG_INF))
        maxes = [jnp.max(sc, axis=-1, keepdims=True) for sc in scores]
        probs = [jnp.exp(sc - m) for sc, m in zip(scores, maxes)]
        sums = [jnp.sum(p, axis=-1, keepdims=True) for p in probs]
        outs = [jnp.dot(p.astype(BF16), vcat[s * _QSUB:s * _QSUB + _BAND, cols], preferred_element_type=F32)
                for s, p in enumerate(probs)]
        for s in range(n_sub):
            rows = slice(s * _QSUB, (s + 1) * _QSUB)
            o_ref[rows, cols] = (outs[s] / sums[s] * jax.nn.sigmoid(gb_ref[rows, cols])).astype(BF16)


def _attn_prompt(proj, rel_bias, d):
    bsz, t, _ = proj.shape
    ng = d // (HEAD_DIM * _ATTN_HEADS)
    width = HEAD_DIM * _ATTN_HEADS
    tq = ATT_REACH
    cur = lambda off: (lambda g, b, i: (b, i, off * ng + g))
    blk = lambda f: pl.BlockSpec((None, tq, width), f)
    return pl.pallas_call(
        _attn_prompt_kernel,
        grid=(ng, bsz, t // tq),
        in_specs=[pl.BlockSpec(memory_space=pltpu.SMEM), blk(cur(2)), blk(cur(3)), blk(cur(4)), blk(cur(6))],
        out_specs=pl.BlockSpec((None, tq, width), lambda g, b, i: (b, i, g)),
        out_shape=jax.ShapeDtypeStruct((bsz, t, d), BF16),
        scratch_shapes=[pltpu.VMEM((_ATTN_HEADS, _QSUB, _BAND), F32),
                        pltpu.VMEM((2 * tq, width), BF16),
                        pltpu.VMEM((2 * tq, width), BF16)],
        compiler_params=_params(3),
        name="attn_prompt",
    )(rel_bias, proj, proj, proj, proj)


def _attn_sample_kernel(rel_ref, q_ref, kn_ref, vn_ref, gb_ref, ck_ref, cv_ref, o_ref,
                        bias_past_s, bias_new_s, *, t, n_past):
    group = pl.program_id(0)

    def bias(head, n_keys, first_pos):
        r = lax.broadcasted_iota(jnp.int32, (t, n_keys), 0)
        c = lax.broadcasted_iota(jnp.int32, (t, n_keys), 1)
        q_pos = PAST_LEN + r
        k_pos = first_pos + c
        qc = q_pos // CHUNK
        kc = k_pos // CHUNK
        visible = (k_pos >= 0) & (kc <= qc) & (kc >= qc - LEFT_CHUNKS)
        return _rel_bias_tile(rel_ref, head, q_pos - k_pos, visible)

    @pl.when(pl.program_id(1) == 0)
    def _():
        for hh in range(_ATTN_HEADS):
            bias_past_s[hh] = bias(group * _ATTN_HEADS + hh, n_past, PAST_LEN - n_past)
            bias_new_s[hh] = bias(group * _ATTN_HEADS + hh, t, PAST_LEN)

    scale = HEAD_DIM ** -0.5
    nt = (((1,), (1,)), ((), ()))
    for hh in range(_ATTN_HEADS):
        cols = slice(hh * HEAD_DIM, (hh + 1) * HEAD_DIM)
        q16 = q_ref[:, cols].astype(BF16)
        s_past = lax.dot_general(q16, ck_ref[:, cols].astype(BF16), nt, preferred_element_type=F32) * scale
        s_new = lax.dot_general(q16, kn_ref[:, cols].astype(BF16), nt, preferred_element_type=F32) * scale
        s_past = s_past + bias_past_s[hh]
        s_new = s_new + bias_new_s[hh]
        m = jnp.maximum(jnp.max(s_past, axis=-1, keepdims=True), jnp.max(s_new, axis=-1, keepdims=True))
        p_past = jnp.exp(s_past - m)
        p_new = jnp.exp(s_new - m)
        l = jnp.sum(p_past, axis=-1, keepdims=True) + jnp.sum(p_new, axis=-1, keepdims=True)
        o = (jnp.dot(p_past.astype(BF16), cv_ref[:, cols].astype(BF16), preferred_element_type=F32)
             + jnp.dot(p_new.astype(BF16), vn_ref[:, cols].astype(BF16), preferred_element_type=F32)) / l
        o_ref[:, cols] = (o * jax.nn.sigmoid(gb_ref[:, cols])).astype(BF16)


def _attn_sample(proj, cache_k, cache_v, rel_bias, d):
    bsz, t, _ = proj.shape
    width = HEAD_DIM * _ATTN_HEADS
    ng = d // width
    n_past = cache_k.shape[1]
    new = lambda off: pl.BlockSpec((None, t, width), lambda g, b: (b, 0, off * ng + g))
    past = pl.BlockSpec((None, n_past, width), lambda g, b: (b, 0, g))
    return pl.pallas_call(
        functools.partial(_attn_sample_kernel, t=t, n_past=n_past),
        grid=(ng, bsz),
        in_specs=[pl.BlockSpec(memory_space=pltpu.SMEM), new(2), new(3), new(4), new(6), past, past],
        out_specs=pl.BlockSpec((None, t, width), lambda g, b: (b, 0, g)),
        out_shape=jax.ShapeDtypeStruct((bsz, t, d), BF16),
        scratch_shapes=[pltpu.VMEM((_ATTN_HEADS, t, n_past), F32), pltpu.VMEM((_ATTN_HEADS, t, t), F32)],
        compiler_params=_params(2),
        name="attn_sample",
    )(rel_bias, proj, proj, proj, proj, cache_k.reshape(bsz, n_past, d), cache_v.reshape(bsz, n_past, d))


def _mix_kernel(pa_ref, pb_ref, x_ref, w_ref, g1_ref, sh2_ref, sc2_ref, lg_ref, lb_ref,
                x1_ref, h2_ref):
    mixed = jnp.dot(pa_ref[...] + pb_ref[...], w_ref[...], preferred_element_type=F32)
    x1 = _normalize(ALPHA * x_ref[...] + g1_ref[...] * mixed) * lg_ref[...] + lb_ref[...]
    x1_ref[...] = x1
    h2_ref[...] = (_normalize(x1) * (1.0 + sc2_ref[...]) + sh2_ref[...]).astype(BF16)


def _mix(pa, pb, x, w_out16, gate1, shift2, scale2, ln_g, ln_b, tm):
    bsz, t, d = x.shape
    tok = pl.BlockSpec((None, tm, d), lambda b, i: (b, i, 0))
    vec = pl.BlockSpec((1, d), lambda b, i: (0, 0))
    return pl.pallas_call(
        _mix_kernel,
        grid=(bsz, t // tm),
        in_specs=[tok, tok, tok, pl.BlockSpec((d, d), lambda b, i: (0, 0)),
                  _mod_spec(gate1, tm, d), _mod_spec(shift2, tm, d), _mod_spec(scale2, tm, d), vec, vec],
        out_specs=[tok, tok],
        out_shape=[jax.ShapeDtypeStruct((bsz, t, d), F32), jax.ShapeDtypeStruct((bsz, t, d), BF16)],
        compiler_params=_params(2),
        name="mix",
    )(pa, pb, x, w_out16, gate1, shift2, scale2, ln_g.reshape(1, d), ln_b.reshape(1, d))


def _take_top(s, n_iter, rows_f, on_pick):
    def body(it, s_rem):
        m = jnp.max(s_rem, axis=0, keepdims=True)
        first = jnp.min(jnp.where(s_rem == m, rows_f, float(s.shape[0])), axis=0, keepdims=True)
        pick = rows_f == first
        on_pick(it, m, pick)
        return jnp.where(pick, -jnp.inf, s_rem)

    lax.fori_loop(0, n_iter, body, s)


def _fill_candidates(vals_s, cand_s):
    cand_s[...] = jnp.full(cand_s.shape, -jnp.inf, F32)
    for r1 in range(PEER_TOPK):
        off, cnt = _CAND_OFFSETS[r1], _CAND_COUNTS[r1]
        cand_s[off:off + cnt, :] = vals_s[0, r1:r1 + 1, :] + vals_s[1, 0:cnt, :]


def _route_exact(h, score_s, rank_s, vals_s, cand_s, sel_s, lcol_ref, e1_ref, r2_ref, e2_ref):
    tm = score_s.shape[-1]
    rows_f = lax.broadcasted_iota(jnp.int32, (N_KEYS, tm), 0).astype(F32)
    crow_f = lax.broadcasted_iota(jnp.int32, (_N_CAND_PAD, tm), 0).astype(F32)
    for p in range(2):
        rank_s[p] = jnp.full((N_KEYS, tm), float(PEER_TOPK), F32)

        def on_pick(it, m, pick, p=p):
            vals_s[p, pl.ds(it, 1), :] = m
            rank_s[p] = jnp.where(pick, jnp.asarray(it, F32), rank_s[p])

        _take_top(score_s[p], PEER_TOPK, rows_f, on_pick)

    _fill_candidates(vals_s, cand_s)
    sel_s[...] = jnp.zeros((_N_CAND_PAD, tm), F32)

    def on_pick_cand(it, m, pick):
        sel_s[...] = jnp.where(pick, 1.0, sel_s[...])

    _take_top(cand_s[...], PEER_TOPK, crow_f, on_pick_cand)

    sel = sel_s[...]
    top = vals_s[0, 0:1, :] + vals_s[1, 0:1, :]
    z = jnp.sum(jnp.where(sel > 0.0, jnp.exp(cand_s[...] - top), 0.0), axis=0, keepdims=True)
    rank1 = rank_s[0]
    lcol = jnp.zeros((N_KEYS, tm), F32)
    for r1 in range(PEER_TOPK):
        off, cnt = _CAND_OFFSETS[r1], _CAND_COUNTS[r1]
        n_sel = jnp.sum(sel[off:off + cnt, :], axis=0, keepdims=True)
        lcol = lcol + jnp.where(rank1 == float(r1), n_sel, 0.0)
    lcol_ref[h] = lcol
    e1_ref[h] = jnp.exp(score_s[0] - vals_s[0, 0:1, :]) / z
    r2_ref[h] = rank_s[1].astype(BF16)
    e2_ref[h] = jnp.exp(score_s[1] - vals_s[1, 0:1, :]).astype(BF16)


def _compare_exchange(xs, i, l, descending):
    hi, lo = jnp.maximum(xs[i], xs[l]), jnp.minimum(xs[i], xs[l])
    xs[i], xs[l] = (hi, lo) if descending else (lo, hi)


def _bitonic_merge(xs):
    xs = list(xs)
    j = len(xs) // 2
    while j >= 1:
        for i in range(len(xs)):
            if i & j == 0:
                _compare_exchange(xs, i, i | j, True)
        j //= 2
    return xs


def _bitonic_sort(xs):
    xs = list(xs)
    k = 2
    while k <= len(xs):
        j = k // 2
        while j >= 1:
            for i in range(len(xs)):
                if i & j == 0:
                    _compare_exchange(xs, i, i | j, i & k == 0)
            j //= 2
        k *= 2
    return xs


def _sorted_top(pieces):
    xs = _bitonic_sort(pieces)
    for shift in (1, 2, 4):
        other = [pltpu.roll(x, shift, 0) for x in xs][::-1]
        if len(xs) < PEER_TOPK:
            xs = _bitonic_merge(xs + other)
        else:
            xs = _bitonic_merge([jnp.maximum(a, b) for a, b in zip(xs, other)])
    return xs


def _sublane_sum(x):
    for shift in (4, 2, 1):
        x = x + pltpu.roll(x, shift, 0)
    return x


def _route_fast(h, score_s, vals_s, cand_s, lcol_ref, e1_ref, r2_ref, e2_ref):
    tm = score_s.shape[-1]
    n_slab = N_KEYS // SUBLANES
    slabs = [[score_s[p, v * SUBLANES:(v + 1) * SUBLANES, :] for v in range(n_slab)] for p in range(2)]
    tops = [_sorted_top(slabs[p]) for p in range(2)]
    for p in range(2):
        for j in range(PEER_TOPK):
            vals_s[p, j:j + 1, :] = tops[p][j][0:1, :]
    _fill_candidates(vals_s, cand_s)
    cands = [cand_s[k * SUBLANES:(k + 1) * SUBLANES, :] for k in range(_N_CAND_PAD // SUBLANES)]
    pad = [jnp.full((SUBLANES, tm), -jnp.inf, F32)] * (SUBLANES - len(cands))
    tau = _sorted_top(cands + pad)[PEER_TOPK - 1]
    top = tops[0][0] + tops[1][0]
    n_picked = jnp.zeros((SUBLANES, tm), F32)
    z = jnp.zeros((SUBLANES, tm), F32)
    for c in cands:
        n_picked = n_picked + jnp.where(c >= tau, 1.0, 0.0)
        z = z + jnp.where(c >= tau, jnp.exp(c - top), 0.0)
    inv_z = 1.0 / _sublane_sum(z)
    tie = jnp.where(_sublane_sum(n_picked) != float(PEER_TOPK), 1.0, 0.0)
    for p in range(2):
        n_ge = jnp.zeros((SUBLANES, tm), F32)
        for s in slabs[p]:
            n_ge = n_ge + jnp.where(s >= tops[p][PEER_TOPK - 1], 1.0, 0.0)
        tie = jnp.maximum(tie, jnp.where(_sublane_sum(n_ge) != float(PEER_TOPK), 1.0, 0.0))
        for j in range(PEER_TOPK - 1):
            tie = jnp.maximum(tie, jnp.where(tops[p][j] == tops[p][j + 1], 1.0, 0.0))
    n_pick = []
    for r1 in range(PEER_TOPK):
        n = jnp.zeros((SUBLANES, tm), F32)
        for r2 in range(_CAND_COUNTS[r1]):
            n = n + jnp.where(tops[0][r1] + tops[1][r2] >= tau, 1.0, 0.0)
        n_pick.append(n)
    rank2, e2 = [], []
    for v in range(n_slab):
        rows = slice(v * SUBLANES, (v + 1) * SUBLANES)
        lc = jnp.zeros((SUBLANES, tm), F32)
        rk = jnp.full((SUBLANES, tm), float(PEER_TOPK), F32)
        for j in range(PEER_TOPK):
            lc = jnp.where(slabs[0][v] == tops[0][j], n_pick[j], lc)
            rk = jnp.where(slabs[1][v] == tops[1][j], float(j), rk)
        lcol_ref[h, rows, :] = lc
        e1_ref[h, rows, :] = jnp.exp(slabs[0][v] - tops[0][0]) * inv_z
        rank2.append(rk)
        e2.append(jnp.exp(slabs[1][v] - tops[1][0]))
    for u in range(n_slab // 2):
        rows = slice(2 * u * SUBLANES, (2 * u + 2) * SUBLANES)
        r2_ref[h, rows, :] = jnp.concatenate(rank2[2 * u:2 * u + 2], axis=0).astype(BF16)
        e2_ref[h, rows, :] = jnp.concatenate(e2[2 * u:2 * u + 2], axis=0).astype(BF16)
    return jnp.max(tie)


def _route_kernel(h2_ref, wpq_ref, keys_ref, lcol_ref, e1_ref, r2_ref, e2_ref,
                  q_s, score_s, rank_s, vals_s, cand_s, sel_s):
    q = jnp.dot(h2_ref[...], wpq_ref[...], preferred_element_type=F32)
    for hp in range(2 * PEER_HEADS):
        q_s[hp] = q[:, hp * N_KEYS:(hp + 1) * N_KEYS].astype(BF16)
    nt = (((1,), (1,)), ((), ()))

    def per_head(h, carry):
        for p in range(2):
            score_s[p] = lax.dot_general(keys_ref[h * 2 + p], q_s[h * 2 + p], nt,
                                         preferred_element_type=F32)
        tie = _route_fast(h, score_s, vals_s, cand_s, lcol_ref, e1_ref, r2_ref, e2_ref)

        @pl.when(tie > 0.0)
        def _():
            _route_exact(h, score_s, rank_s, vals_s, cand_s, sel_s, lcol_ref, e1_ref, r2_ref, e2_ref)

        return carry

    lax.fori_loop(0, PEER_HEADS, per_head, 0)


def _route(h2, w_pq16, keys16, tm):
    bsz, t, d = h2.shape
    nq = w_pq16.shape[1]
    out = pl.BlockSpec((None, PEER_HEADS, N_KEYS, tm), lambda b, i: (b, 0, 0, i))
    shape = jax.ShapeDtypeStruct((bsz, PEER_HEADS, N_KEYS, t), F32)
    shape16 = jax.ShapeDtypeStruct((bsz, PEER_HEADS, N_KEYS, t), BF16)
    return pl.pallas_call(
        _route_kernel,
        grid=(bsz, t // tm),
        in_specs=[pl.BlockSpec((None, tm, d), lambda b, i: (b, i, 0)),
                  pl.BlockSpec((d, nq), lambda b, i: (0, 0)),
                  pl.BlockSpec((2 * PEER_HEADS, N_KEYS, nq // (2 * PEER_HEADS)), lambda b, i: (0, 0, 0))],
        out_specs=[out, out, out, out],
        out_shape=[shape, shape, shape16, shape16],
        scratch_shapes=[pltpu.VMEM((2 * PEER_HEADS, tm, nq // (2 * PEER_HEADS)), BF16),
                        pltpu.VMEM((2, N_KEYS, tm), F32),
                        pltpu.VMEM((2, N_KEYS, tm), F32),
                        pltpu.VMEM((2, PEER_TOPK, tm), F32),
                        pltpu.VMEM((_N_CAND_PAD, tm), F32),
                        pltpu.VMEM((_N_CAND_PAD, tm), F32)],
        compiler_params=_params(2),
        name="route",
    )(h2, w_pq16, keys16)


_ACT_ROWS = 512


def _peer_kernel(h2_ref, u_ref, vt_ref, lcol_ref, e1_ref, r2_ref, e2_ref, x1_ref, g2_ref,
                 lg_ref, lb_ref, y_ref, acc, p_s, *, te):
    j = pl.program_id(2)

    @pl.when(j == 0)
    def _():
        acc[...] = jnp.zeros(acc.shape, F32)

    zero = jnp.zeros((), BF16)
    n_sub = _ACT_ROWS // N_KEYS
    for c in range(te // _ACT_ROWS):
        chunk = slice(c * _ACT_ROWS, (c + 1) * _ACT_ROWS)
        act = lax.dot_general(u_ref[chunk, :], h2_ref[...], (((1,), (1,)), ((), ())),
                              preferred_element_type=F32)
        for sub in range(n_sub):
            al = c * n_sub + sub
            w = None
            for h in range(PEER_HEADS):
                keep = r2_ref[h] < lcol_ref[h, al:al + 1, :].astype(BF16)
                term = jnp.where(keep, e2_ref[h] * e1_ref[h, al:al + 1, :].astype(BF16), zero)
                w = term if w is None else w + term
            g = _gelu(act[sub * N_KEYS:(sub + 1) * N_KEYS, :])
            p_s[al * N_KEYS:(al + 1) * N_KEYS, :] = (w.astype(F32) * g).astype(BF16)
    acc[...] += jnp.dot(vt_ref[...], p_s[...], preferred_element_type=F32)

    @pl.when(j == pl.num_programs(2) - 1)
    def _():
        ffn = acc[...].T
        y = _normalize(ALPHA * x1_ref[...] + g2_ref[...] * ffn)
        y_ref[...] = y * lg_ref[...] + lb_ref[...]


def _peer(h2, u16, vt16, lcol, e1, r2, e2, x1, gate2, ln_g, ln_b, tm, te):
    bsz, t, d = h2.shape
    n_exp = u16.shape[0]
    na = te // N_KEYS
    tok = pl.BlockSpec((None, tm, d), lambda b, i, j: (b, i, 0))
    vec = pl.BlockSpec((1, d), lambda b, i, j: (0, 0))
    by_a = pl.BlockSpec((None, PEER_HEADS, na, tm), lambda b, i, j: (b, 0, j, i))
    by_b = pl.BlockSpec((None, PEER_HEADS, N_KEYS, tm), lambda b, i, j: (b, 0, 0, i))
    return pl.pallas_call(
        functools.partial(_peer_kernel, te=te),
        grid=(bsz, t // tm, n_exp // te),
        in_specs=[tok,
                  pl.BlockSpec((te, d), lambda b, i, j: (j, 0)),
                  pl.BlockSpec((d, te), lambda b, i, j: (0, j)),
                  by_a, by_a, by_b, by_b, tok, _mod_spec(gate2, tm, d), vec, vec],
        out_specs=tok,
        out_shape=jax.ShapeDtypeStruct((bsz, t, d), F32),
        scratch_shapes=[pltpu.VMEM((d, tm), F32), pltpu.VMEM((te, tm), BF16)],
        compiler_params=_params(3),
        name="peer",
    )(h2, u16, vt16, lcol, e1, r2, e2, x1, gate2, ln_g.reshape(1, d), ln_b.reshape(1, d))


def _tiles(n_tok, t):
    return dict(tm_proj=min(1024, n_tok),
                tm_mix=min(512, n_tok),
                tt=min(512, t), tc=512,
                tm_route=min(512, n_tok),
                tm_peer=min(512, n_tok), te=1024)


def _layer(x, mods, conv_buf, h0, attend, w, *, keep_rows, flatten):
    bsz, t, d = x.shape
    tl = _tiles(bsz * t if flatten else t, t)
    if flatten:
        mods = [jnp.broadcast_to(m, (bsz, t, d)).reshape(1, bsz * t, d) for m in mods]
        flat = lambda a: a.reshape(1, bsz * t, a.shape[-1])
    else:
        flat = lambda a: a
    shift1, scale1, gate1, shift2, scale2, gate2 = mods
    proj = _proj(flat(x), shift1, scale1, w["w_in"], tl["tm_proj"]).reshape(bsz, t, -1)
    pa, conv_state, rnn_state = _rnn(proj, conv_buf, h0, w["conv_w"], w["conv_b"], w["w_ra"], w["b_ra"],
                                     w["w_ri"], w["b_ri"], w["lam"], d, tl["tt"], tl["tc"])
    pb = attend(proj)
    x1, h2 = _mix(flat(pa), flat(pb), flat(x), w["w_out"], gate1, shift2, scale2, w["ln1_g"], w["ln1_b"],
                  tl["tm_mix"])
    lcol, e1, r2, e2 = _route(h2, w["w_pq"], w["keys"], tl["tm_route"])
    y = _peer(h2, w["peer_u"], w["peer_vt"], lcol, e1, r2, e2, x1, gate2, w["ln2_g"], w["ln2_b"],
              tl["tm_peer"], tl["te"])
    n_heads = d // HEAD_DIM
    k = proj[:, t - keep_rows:, 3 * d:4 * d].reshape(bsz, keep_rows, n_heads, HEAD_DIM)
    v = proj[:, t - keep_rows:, 4 * d:5 * d].reshape(bsz, keep_rows, n_heads, HEAD_DIM)
    return y.reshape(bsz, t, d), conv_state, rnn_state, k, v


def kernel(x_prompt, x_sample, c_prompt, c_sample, state_conv, state_rnn, cache_k, cache_v, w_ada, b_ada, w_in, conv_w, conv_b, w_ra, b_ra, w_ri, b_ri, rg_lambda, rel_bias, w_out, ln1_g, ln1_b, w_pq, peer_keys, peer_u, peer_v, ln2_g, ln2_b):
    bsz_p, t_p, d = x_prompt.shape
    bsz_s, t_s, _ = x_sample.shape
    depth = w_ada.shape[0]
    keep = min(ATT_REACH, t_p)
    y_p, y_s = x_prompt, x_sample
    outs = [[] for _ in range(8)]
    rows = bsz_p + bsz_s
    rows_pad = -(-rows // SUBLANES) * SUBLANES
    c_all = jnp.concatenate([c_prompt, c_sample, jnp.zeros((rows_pad - rows, d), F32)], axis=0)
    for l in range(depth):
        w = dict(w_in=w_in[l].astype(BF16), conv_w=conv_w[l], conv_b=conv_b[l],
                 w_ra=w_ra[l].astype(BF16), b_ra=b_ra[l], w_ri=w_ri[l].astype(BF16), b_ri=b_ri[l],
                 lam=rg_lambda[l], w_out=w_out[l].astype(BF16), ln1_g=ln1_g[l], ln1_b=ln1_b[l],
                 w_pq=w_pq[l].astype(BF16),
                 keys=peer_keys[l].astype(BF16).reshape(2 * PEER_HEADS, N_KEYS, -1),
                 peer_u=peer_u[l].astype(BF16), peer_vt=peer_v[l].astype(BF16).T,
                 ln2_g=ln2_g[l], ln2_b=ln2_b[l])
        mod = _ada(c_all, w_ada[l], b_ada[l])
        mods_p = [m[:, None, :] for m in jnp.split(mod[:bsz_p], 6, axis=-1)]
        mods_s = [m[:, None, :] for m in jnp.split(mod[bsz_p:rows], 6, axis=-1)]

        y_p, cs, hs, k, v = _layer(
            y_p, mods_p, jnp.zeros((bsz_p, CONV_W - 1, d), F32), jnp.zeros((bsz_p, d), F32),
            lambda proj: _attn_prompt(proj, rel_bias[l], d), w,
            keep_rows=keep, flatten=False)
        for dst, val in zip(outs[:4], (cs, hs, k, v)):
            dst.append(val)

        y_s, cs, hs, k, v = _layer(
            y_s, mods_s, state_conv[l], state_rnn[l],
            lambda proj: _attn_sample(proj, cache_k[l], cache_v[l], rel_bias[l], d), w,
            keep_rows=t_s, flatten=True)
        for dst, val in zip(outs[4:], (cs, hs, k, v)):
            dst.append(val)
    return (y_p, y_s) + tuple(jnp.stack(o) for o in outs)
```

```python
import functools

import jax
import jax.numpy as jnp
from jax import lax
from jax.experimental import pallas as pl
from jax.experimental.pallas import tpu as pltpu

F32 = jnp.float32
BF16 = jnp.bfloat16

CHUNK = 64
LEFT_CHUNKS = 8
ATT_REACH = LEFT_CHUNKS * CHUNK
MAX_REL = 128
REL_BUCKETS = MAX_REL + CHUNK
PAST_LEN = 1024
CONV_W = 4
RG_C = 8.0
RNN_BLOCK_DIM = 128
HEAD_DIM = 128
N_KEYS = 128
PEER_HEADS = 8
PEER_TOPK = 16
DEPTH = 1
ALPHA = (2.0 * DEPTH) ** 0.25
LN_EPS = 1e-5
NEG_INF = -1e30

SUBLANES = 8
VMEM_LIMIT = 56 * 1024 * 1024

_CAND_COUNTS = tuple(PEER_TOPK // (r1 + 1) for r1 in range(PEER_TOPK))
_CAND_OFFSETS = tuple(sum(_CAND_COUNTS[:r1]) for r1 in range(PEER_TOPK))
_N_CAND = sum(_CAND_COUNTS)
_N_CAND_PAD = -(-_N_CAND // SUBLANES) * SUBLANES


def _params(n_grid, flags=None):
    return pltpu.CompilerParams(dimension_semantics=("arbitrary",) * n_grid,
                                vmem_limit_bytes=VMEM_LIMIT, flags=flags)


def _normalize(x):
    mu = jnp.mean(x, axis=-1, keepdims=True)
    xc = x - mu
    var = jnp.mean(xc * xc, axis=-1, keepdims=True)
    return xc * lax.rsqrt(var + LN_EPS)


def _gelu(x):
    return 0.5 * x * (1.0 + lax.erf(x * (2.0 ** -0.5)))


def _ada_kernel(c_ref, w_ref, b_ref, o_ref):
    c = c_ref[...]
    s = c * jax.nn.sigmoid(c)
    o_ref[...] = jnp.dot(s, w_ref[...], preferred_element_type=F32) + b_ref[...]


def _ada(c, w_ada, b_ada):
    rows, d = c.shape
    n = w_ada.shape[1]
    tn = 1024
    return pl.pallas_call(
        _ada_kernel,
        grid=(n // tn,),
        in_specs=[pl.BlockSpec((rows, d), lambda j: (0, 0)),
                  pl.BlockSpec((d, tn), lambda j: (0, j)),
                  pl.BlockSpec((1, tn), lambda j: (0, j))],
        out_specs=pl.BlockSpec((rows, tn), lambda j: (0, j)),
        out_shape=jax.ShapeDtypeStruct((rows, n), F32),
        compiler_params=_params(1),
        name="ada",
    )(c, w_ada, b_ada.reshape(1, n))


def _proj_kernel(x_ref, sh_ref, sc_ref, w_ref, o_ref, h_ref):
    @pl.when(pl.program_id(2) == 0)
    def _():
        h = _normalize(x_ref[...]) * (1.0 + sc_ref[...]) + sh_ref[...]
        h_ref[...] = h.astype(BF16)

    o_ref[...] = jnp.dot(h_ref[...], w_ref[...], preferred_element_type=F32)


def _mod_spec(mod, tm, d):
    if mod.shape[1] == 1:
        return pl.BlockSpec((None, 1, d), lambda b, i, *_: (b, 0, 0))
    return pl.BlockSpec((None, tm, d), lambda b, i, *_: (b, i, 0))


def _proj(x, shift, scale, w_in_bf16, tm):
    bsz, t, d = x.shape
    n = w_in_bf16.shape[1]
    tn = 1024
    return pl.pallas_call(
        _proj_kernel,
        grid=(bsz, t // tm, n // tn),
        in_specs=[pl.BlockSpec((None, tm, d), lambda b, i, j: (b, i, 0)),
                  _mod_spec(shift, tm, d), _mod_spec(scale, tm, d),
                  pl.BlockSpec((d, tn), lambda b, i, j: (0, j))],
        out_specs=pl.BlockSpec((None, tm, tn), lambda b, i, j: (b, i, j)),
        out_shape=jax.ShapeDtypeStruct((bsz, t, n), F32),
        scratch_shapes=[pltpu.VMEM((tm, d), BF16)],
        compiler_params=_params(3),
        name="proj",
    )(x, shift, scale, w_in_bf16)


def _rnn_kernel(xr_ref, yr_ref, ga_ref, cbuf_ref, h0_ref, cw_ref, cb_ref, wra_ref, bra_ref,
                wri_ref, bri_ref, lam_ref, pa_ref, cst_ref, hst_ref,
                xbuf, a_s, u_s, hcar, *, tt, tc):
    t = pl.program_id(2)
    nt = pl.num_programs(2)
    pad = SUBLANES
    keep = CONV_W - 1

    @pl.when(t == 0)
    def _():
        xbuf[pad - keep:pad, :] = cbuf_ref[...]
        hcar[...] = h0_ref[...]

    xbuf[pad:pad + tt, :] = xr_ref[...]
    cw = cw_ref[...]
    xc = cb_ref[...] + cw[0:1, :] * xbuf[pad - keep:pad - keep + tt, :]
    for j in range(1, CONV_W):
        xc = xc + cw[j:j + 1, :] * xbuf[pad - keep + j:pad - keep + j + tt, :]
    tail = xbuf[pad + tt - keep:pad + tt, :]
    xbuf[pad - keep:pad, :] = tail

    lam = lam_ref[...]
    softplus_neg = jnp.maximum(-lam, 0.0) + jnp.log1p(jnp.exp(-jnp.abs(lam)))
    for blk in range(tc // RNN_BLOCK_DIM):
        cols = slice(blk * RNN_BLOCK_DIM, (blk + 1) * RNN_BLOCK_DIM)
        xb = xc[:, cols]
        xb16 = xb.astype(BF16)
        r = jax.nn.sigmoid(jnp.dot(xb16, wra_ref[blk], preferred_element_type=F32) + bra_ref[:, cols])
        i = jax.nn.sigmoid(jnp.dot(xb16, wri_ref[blk], preferred_element_type=F32) + bri_ref[:, cols])
        log_a = -RG_C * r * softplus_neg[:, cols]
        a = jnp.exp(log_a)
        u = jnp.sqrt(jnp.tanh(-log_a) * (1.0 + a * a)) * (i * xb)
        a_s[:, cols] = a
        u_s[:, cols] = u

    row = lax.broadcasted_iota(jnp.int32, (SUBLANES, tc), 0)

    def scan_group(g, h_prev):
        rows = pl.ds(pl.multiple_of(g * SUBLANES, SUBLANES), SUBLANES)
        a = a_s[rows, :]
        u = u_s[rows, :]
        for s in (1, 2, 4):
            a_sh = pltpu.roll(a, s, 0)
            u_sh = pltpu.roll(u, s, 0)
            live = row >= s
            u = jnp.where(live, a * u_sh + u, u)
            a = jnp.where(live, a * a_sh, a)
        h = a * h_prev + u
        u_s[rows, :] = h
        return h[SUBLANES - 1:SUBLANES, :]

    h_last = lax.fori_loop(0, tt // SUBLANES, scan_group, hcar[...])
    hcar[...] = h_last

    pa_ref[...] = (u_s[...] * _gelu(yr_ref[...]) * jax.nn.sigmoid(ga_ref[...])).astype(BF16)

    @pl.when(t == nt - 1)
    def _():
        cst_ref[...] = tail
        hst_ref[...] = h_last


def _rnn(proj, conv_buf, h0, conv_w, conv_b, w_ra16, b_ra, w_ri16, b_ri, lam, d, tt, tc):
    bsz, t, _ = proj.shape
    nc = d // tc
    nb = tc // RNN_BLOCK_DIM
    keep = CONV_W - 1
    col = lambda off: (lambda b, c, i: (b, i, off * nc + c))
    vec = pl.BlockSpec((1, tc), lambda b, c, i: (0, c))
    blk = pl.BlockSpec((nb, RNN_BLOCK_DIM, RNN_BLOCK_DIM), lambda b, c, i: (c, 0, 0))
    pa, cst, hst = pl.pallas_call(
        functools.partial(_rnn_kernel, tt=tt, tc=tc),
        grid=(bsz, nc, t // tt),
        in_specs=[pl.BlockSpec((None, tt, tc), col(0)),
                  pl.BlockSpec((None, tt, tc), col(1)),
                  pl.BlockSpec((None, tt, tc), col(5)),
                  pl.BlockSpec((None, keep, tc), lambda b, c, i: (b, 0, c)),
                  pl.BlockSpec((None, 1, tc), lambda b, c, i: (b, 0, c)),
                  pl.BlockSpec((CONV_W, tc), lambda b, c, i: (0, c)),
                  vec, blk, vec, blk, vec, vec],
        out_specs=[pl.BlockSpec((None, tt, tc), lambda b, c, i: (b, i, c)),
                   pl.BlockSpec((None, keep, tc), lambda b, c, i: (b, 0, c)),
                   pl.BlockSpec((None, 1, tc), lambda b, c, i: (b, 0, c))],
        out_shape=[jax.ShapeDtypeStruct((bsz, t, d), BF16),
                   jax.ShapeDtypeStruct((bsz, keep, d), F32),
                   jax.ShapeDtypeStruct((bsz, 1, d), F32)],
        scratch_shapes=[pltpu.VMEM((tt + SUBLANES, tc), F32),
                        pltpu.VMEM((tt, tc), F32),
                        pltpu.VMEM((tt, tc), F32),
                        pltpu.VMEM((1, tc), F32)],
        compiler_params=_params(3),
        name="rnn",
    )(proj, proj, proj, conv_buf, h0.reshape(bsz, 1, d), conv_w, conv_b.reshape(1, d),
      w_ra16, b_ra.reshape(1, d), w_ri16, b_ri.reshape(1, d), lam.reshape(1, d))
    return pa, cst, hst.reshape(bsz, d)


def _rel_bias_tile(rel_ref, head, dist, visible):
    idx = jnp.clip(dist, -(CHUNK - 1), MAX_REL) + (CHUNK - 1)

    def body(j, acc):
        return acc + jnp.where(idx == j, rel_ref[head, j], 0.0)

    bias = lax.fori_loop(0, REL_BUCKETS, body, jnp.zeros(dist.shape, F32))
    return jnp.where(visible, bias, NEG_INF)


_QSUB = 2 * CHUNK
_BAND = _QSUB + ATT_REACH
_ATTN_HEADS = 4


def _attn_prompt_kernel(rel_ref, q_ref, kc_ref, vc_ref, gb_ref, o_ref, bias_s, kcat, vcat):
    group, b, i = pl.program_id(0), pl.program_id(1), pl.program_id(2)
    tq = ATT_REACH

    @pl.when((b == 0) & (i == 0))
    def _():
        r = lax.broadcasted_iota(jnp.int32, (_QSUB, _BAND), 0)
        c = lax.broadcasted_iota(jnp.int32, (_QSUB, _BAND), 1)
        qc = r // CHUNK + LEFT_CHUNKS
        kc = c // CHUNK
        visible = (kc <= qc) & (kc >= qc - LEFT_CHUNKS)
        for hh in range(_ATTN_HEADS):
            bias_s[hh] = _rel_bias_tile(rel_ref, group * _ATTN_HEADS + hh, r + ATT_REACH - c, visible)

    @pl.when(i == 0)
    def _():
        kcat[0:tq, :] = jnp.zeros((tq, kcat.shape[1]), BF16)
        vcat[0:tq, :] = jnp.zeros((tq, vcat.shape[1]), BF16)

    @pl.when(i > 0)
    def _():
        kcat[0:tq, :] = kcat[tq:2 * tq, :]
        vcat[0:tq, :] = vcat[tq:2 * tq, :]

    kcat[tq:2 * tq, :] = kc_ref[...].astype(BF16)
    vcat[tq:2 * tq, :] = vc_ref[...].astype(BF16)
    col = lax.broadcasted_iota(jnp.int32, (_QSUB, _BAND), 1)
    n_sub = tq // _QSUB
    for hh in range(_ATTN_HEADS):
        cols = slice(hh * HEAD_DIM, (hh + 1) * HEAD_DIM)
        scores = []
        for s in range(n_sub):
            q16 = q_ref[s * _QSUB:(s + 1) * _QSUB, cols].astype(BF16)
            sc = lax.dot_general(q16, kcat[s * _QSUB:s * _QSUB + _BAND, cols], (((1,), (1,)), ((), ())),
                                 preferred_element_type=F32) * (HEAD_DIM ** -0.5)
            sc = sc + bias_s[hh]
            k_pos = col + (i * tq + s * _QSUB - ATT_REACH)
            scores.append(jnp.where(k_pos >= 0, sc, NEG_INF))
        maxes = [jnp.max(sc, axis=-1, keepdims=True) for sc in scores]
        probs = [jnp.exp(sc - m) for sc, m in zip(scores, maxes)]
        sums = [jnp.sum(p, axis=-1, keepdims=True) for p in probs]
        outs = [jnp.dot(p.astype(BF16), vcat[s * _QSUB:s * _QSUB + _BAND, cols], preferred_element_type=F32)
                for s, p in enumerate(probs)]
        for s in range(n_sub):
            rows = slice(s * _QSUB, (s + 1) * _QSUB)
            o_ref[rows, cols] = (outs[s] / sums[s] * jax.nn.sigmoid(gb_ref[rows, cols])).astype(BF16)


def _attn_prompt(proj, rel_bias, d):
    bsz, t, _ = proj.shape
    ng = d // (HEAD_DIM * _ATTN_HEADS)
    width = HEAD_DIM * _ATTN_HEADS
    tq = ATT_REACH
    cur = lambda off: (lambda g, b, i: (b, i, off * ng + g))
    blk = lambda f: pl.BlockSpec((None, tq, width), f)
    return pl.pallas_call(
        _attn_prompt_kernel,
        grid=(ng, bsz, t // tq),
        in_specs=[pl.BlockSpec(memory_space=pltpu.SMEM), blk(cur(2)), blk(cur(3)), blk(cur(4)), blk(cur(6))],
        out_specs=pl.BlockSpec((None, tq, width), lambda g, b, i: (b, i, g)),
        out_shape=jax.ShapeDtypeStruct((bsz, t, d), BF16),
        scratch_shapes=[pltpu.VMEM((_ATTN_HEADS, _QSUB, _BAND), F32),
                        pltpu.VMEM((2 * tq, width), BF16),
                        pltpu.VMEM((2 * tq, width), BF16)],
        compiler_params=_params(3),
        name="attn_prompt",
    )(rel_bias, proj, proj, proj, proj)


def _attn_sample_kernel(rel_ref, q_ref, kn_ref, vn_ref, gb_ref, ck_ref, cv_ref, o_ref,
                        bias_past_s, bias_new_s, *, t, n_past):
    group = pl.program_id(0)

    def bias(head, n_keys, first_pos):
        r = lax.broadcasted_iota(jnp.int32, (t, n_keys), 0)
        c = lax.broadcasted_iota(jnp.int32, (t, n_keys), 1)
        q_pos = PAST_LEN + r
        k_pos = first_pos + c
        qc = q_pos // CHUNK
        kc = k_pos // CHUNK
        visible = (k_pos >= 0) & (kc <= qc) & (kc >= qc - LEFT_CHUNKS)
        return _rel_bias_tile(rel_ref, head, q_pos - k_pos, visible)

    @pl.when(pl.program_id(1) == 0)
    def _():
        for hh in range(_ATTN_HEADS):
            bias_past_s[hh] = bias(group * _ATTN_HEADS + hh, n_past, PAST_LEN - n_past)
            bias_new_s[hh] = bias(group * _ATTN_HEADS + hh, t, PAST_LEN)

    scale = HEAD_DIM ** -0.5
    nt = (((1,), (1,)), ((), ()))
    for hh in range(_ATTN_HEADS):
        cols = slice(hh * HEAD_DIM, (hh + 1) * HEAD_DIM)
        q16 = q_ref[:, cols].astype(BF16)
        s_past = lax.dot_general(q16, ck_ref[:, cols].astype(BF16), nt, preferred_element_type=F32) * scale
        s_new = lax.dot_general(q16, kn_ref[:, cols].astype(BF16), nt, preferred_element_type=F32) * scale
        s_past = s_past + bias_past_s[hh]
        s_new = s_new + bias_new_s[hh]
        m = jnp.maximum(jnp.max(s_past, axis=-1, keepdims=True), jnp.max(s_new, axis=-1, keepdims=True))
        p_past = jnp.exp(s_past - m)
        p_new = jnp.exp(s_new - m)
        l = jnp.sum(p_past, axis=-1, keepdims=True) + jnp.sum(p_new, axis=-1, keepdims=True)
        o = (jnp.dot(p_past.astype(BF16), cv_ref[:, cols].astype(BF16), preferred_element_type=F32)
             + jnp.dot(p_new.astype(BF16), vn_ref[:, cols].astype(BF16), preferred_element_type=F32)) / l
        o_ref[:, cols] = (o * jax.nn.sigmoid(gb_ref[:, cols])).astype(BF16)


def _attn_sample(proj, cache_k, cache_v, rel_bias, d):
    bsz, t, _ = proj.shape
    width = HEAD_DIM * _ATTN_HEADS
    ng = d // width
    n_past = cache_k.shape[1]
    new = lambda off: pl.BlockSpec((None, t, width), lambda g, b: (b, 0, off * ng + g))
    past = pl.BlockSpec((None, n_past, width), lambda g, b: (b, 0, g))
    return pl.pallas_call(
        functools.partial(_attn_sample_kernel, t=t, n_past=n_past),
        grid=(ng, bsz),
        in_specs=[pl.BlockSpec(memory_space=pltpu.SMEM), new(2), new(3), new(4), new(6), past, past],
        out_specs=pl.BlockSpec((None, t, width), lambda g, b: (b, 0, g)),
        out_shape=jax.ShapeDtypeStruct((bsz, t, d), BF16),
        scratch_shapes=[pltpu.VMEM((_ATTN_HEADS, t, n_past), F32), pltpu.VMEM((_ATTN_HEADS, t, t), F32)],
        compiler_params=_params(2),
        name="attn_sample",
    )(rel_bias, proj, proj, proj, proj, cache_k.reshape(bsz, n_past, d), cache_v.reshape(bsz, n_past, d))


def _mix_kernel(pa_ref, pb_ref, x_ref, w_ref, g1_ref, sh2_ref, sc2_ref, lg_ref, lb_ref,
                x1_ref, h2_ref):
    mixed = jnp.dot(pa_ref[...] + pb_ref[...], w_ref[...], preferred_element_type=F32)
    x1 = _normalize(ALPHA * x_ref[...] + g1_ref[...] * mixed) * lg_ref[...] + lb_ref[...]
    x1_ref[...] = x1
    h2_ref[...] = (_normalize(x1) * (1.0 + sc2_ref[...]) + sh2_ref[...]).astype(BF16)


def _mix(pa, pb, x, w_out16, gate1, shift2, scale2, ln_g, ln_b, tm):
    bsz, t, d = x.shape
    tok = pl.BlockSpec((None, tm, d), lambda b, i: (b, i, 0))
    vec = pl.BlockSpec((1, d), lambda b, i: (0, 0))
    return pl.pallas_call(
        _mix_kernel,
        grid=(bsz, t // tm),
        in_specs=[tok, tok, tok, pl.BlockSpec((d, d), lambda b, i: (0, 0)),
                  _mod_spec(gate1, tm, d), _mod_spec(shift2, tm, d), _mod_spec(scale2, tm, d), vec, vec],
        out_specs=[tok, tok],
        out_shape=[jax.ShapeDtypeStruct((bsz, t, d), F32), jax.ShapeDtypeStruct((bsz, t, d), BF16)],
        compiler_params=_params(2),
        name="mix",
    )(pa, pb, x, w_out16, gate1, shift2, scale2, ln_g.reshape(1, d), ln_b.reshape(1, d))


def _take_top(s, n_iter, rows_f, on_pick):
    def body(it, s_rem):
        m = jnp.max(s_rem, axis=0, keepdims=True)
        first = jnp.min(jnp.where(s_rem == m, rows_f, float(s.shape[0])), axis=0, keepdims=True)
        pick = rows_f == first
        on_pick(it, m, pick)
        return jnp.where(pick, -jnp.inf, s_rem)

    lax.fori_loop(0, n_iter, body, s)


def _fill_candidates(vals_s, cand_s):
    cand_s[...] = jnp.full(cand_s.shape, -jnp.inf, F32)
    for r1 in range(PEER_TOPK):
        off, cnt = _CAND_OFFSETS[r1], _CAND_COUNTS[r1]
        cand_s[off:off + cnt, :] = vals_s[0, r1:r1 + 1, :] + vals_s[1, 0:cnt, :]


def _route_exact(h, score_s, rank_s, vals_s, cand_s, sel_s, lcol_ref, e1_ref, r2_ref, e2_ref):
    tm = score_s.shape[-1]
    rows_f = lax.broadcasted_iota(jnp.int32, (N_KEYS, tm), 0).astype(F32)
    crow_f = lax.broadcasted_iota(jnp.int32, (_N_CAND_PAD, tm), 0).astype(F32)
    for p in range(2):
        rank_s[p] = jnp.full((N_KEYS, tm), float(PEER_TOPK), F32)

        def on_pick(it, m, pick, p=p):
            vals_s[p, pl.ds(it, 1), :] = m
            rank_s[p] = jnp.where(pick, jnp.asarray(it, F32), rank_s[p])

        _take_top(score_s[p], PEER_TOPK, rows_f, on_pick)

    _fill_candidates(vals_s, cand_s)
    sel_s[...] = jnp.zeros((_N_CAND_PAD, tm), F32)

    def on_pick_cand(it, m, pick):
        sel_s[...] = jnp.where(pick, 1.0, sel_s[...])

    _take_top(cand_s[...], PEER_TOPK, crow_f, on_pick_cand)

    sel = sel_s[...]
    top = vals_s[0, 0:1, :] + vals_s[1, 0:1, :]
    z = jnp.sum(jnp.where(sel > 0.0, jnp.exp(cand_s[...] - top), 0.0), axis=0, keepdims=True)
    rank1 = rank_s[0]
    lcol = jnp.zeros((N_KEYS, tm), F32)
    for r1 in range(PEER_TOPK):
        off, cnt = _CAND_OFFSETS[r1], _CAND_COUNTS[r1]
        n_sel = jnp.sum(sel[off:off + cnt, :], axis=0, keepdims=True)
        lcol = lcol + jnp.where(rank1 == float(r1), n_sel, 0.0)
    lcol_ref[h] = lcol
    e1_ref[h] = jnp.exp(score_s[0] - vals_s[0, 0:1, :]) / z
    r2_ref[h] = rank_s[1].astype(BF16)
    e2_ref[h] = jnp.exp(score_s[1] - vals_s[1, 0:1, :]).astype(BF16)


def _compare_exchange(xs, i, l, descending):
    hi, lo = jnp.maximum(xs[i], xs[l]), jnp.minimum(xs[i], xs[l])
    xs[i], xs[l] = (hi, lo) if descending else (lo, hi)


def _bitonic_merge(xs):
    xs = list(xs)
    j = len(xs) // 2
    while j >= 1:
        for i in range(len(xs)):
            if i & j == 0:
                _compare_exchange(xs, i, i | j, True)
        j //= 2
    return xs


def _bitonic_sort(xs):
    xs = list(xs)
    k = 2
    while k <= len(xs):
        j = k // 2
        while j >= 1:
            for i in range(len(xs)):
                if i & j == 0:
                    _compare_exchange(xs, i, i | j, i & k == 0)
            j //= 2
        k *= 2
    return xs


def _sorted_top(pieces):
    xs = _bitonic_sort(pieces)
    for shift in (1, 2, 4):
        other = [pltpu.roll(x, shift, 0) for x in xs][::-1]
        if len(xs) < PEER_TOPK:
            xs = _bitonic_merge(xs + other)
        else:
            xs = _bitonic_merge([jnp.maximum(a, b) for a, b in zip(xs, other)])
    return xs


def _sublane_sum(x):
    for shift in (4, 2, 1):
        x = x + pltpu.roll(x, shift, 0)
    return x


def _route_fast(h, score_s, vals_s, cand_s, lcol_ref, e1_ref, r2_ref, e2_ref):
    tm = score_s.shape[-1]
    n_slab = N_KEYS // SUBLANES
    slabs = [[score_s[p, v * SUBLANES:(v + 1) * SUBLANES, :] for v in range(n_slab)] for p in range(2)]
    tops = [_sorted_top(slabs[p]) for p in range(2)]
    for p in range(2):
        for j in range(PEER_TOPK):
            vals_s[p, j:j + 1, :] = tops[p][j][0:1, :]
    _fill_candidates(vals_s, cand_s)
    cands = [cand_s[k * SUBLANES:(k + 1) * SUBLANES, :] for k in range(_N_CAND_PAD // SUBLANES)]
    pad = [jnp.full((SUBLANES, tm), -jnp.inf, F32)] * (SUBLANES - len(cands))
    tau = _sorted_top(cands + pad)[PEER_TOPK - 1]
    top = tops[0][0] + tops[1][0]
    n_picked = jnp.zeros((SUBLANES, tm), F32)
    z = jnp.zeros((SUBLANES, tm), F32)
    for c in cands:
        n_picked = n_picked + jnp.where(c >= tau, 1.0, 0.0)
        z = z + jnp.where(c >= tau, jnp.exp(c - top), 0.0)
    inv_z = 1.0 / _sublane_sum(z)
    tie = jnp.where(_sublane_sum(n_picked) != float(PEER_TOPK), 1.0, 0.0)
    for p in range(2):
        n_ge = jnp.zeros((SUBLANES, tm), F32)
        for s in slabs[p]:
            n_ge = n_ge + jnp.where(s >= tops[p][PEER_TOPK - 1], 1.0, 0.0)
        tie = jnp.maximum(tie, jnp.where(_sublane_sum(n_ge) != float(PEER_TOPK), 1.0, 0.0))
        for j in range(PEER_TOPK - 1):
            tie = jnp.maximum(tie, jnp.where(tops[p][j] == tops[p][j + 1], 1.0, 0.0))
    n_pick = []
    for r1 in range(PEER_TOPK):
        n = jnp.zeros((SUBLANES, tm), F32)
        for r2 in range(_CAND_COUNTS[r1]):
            n = n + jnp.where(tops[0][r1] + tops[1][r2] >= tau, 1.0, 0.0)
        n_pick.append(n)
    rank2, e2 = [], []
    for v in range(n_slab):
        rows = slice(v * SUBLANES, (v + 1) * SUBLANES)
        lc = jnp.zeros((SUBLANES, tm), F32)
        rk = jnp.full((SUBLANES, tm), float(PEER_TOPK), F32)
        for j in range(PEER_TOPK):
            lc = jnp.where(slabs[0][v] == tops[0][j], n_pick[j], lc)
            rk = jnp.where(slabs[1][v] == tops[1][j], float(j), rk)
        lcol_ref[h, rows, :] = lc
        e1_ref[h, rows, :] = jnp.exp(slabs[0][v] - tops[0][0]) * inv_z
        rank2.append(rk)
        e2.append(jnp.exp(slabs[1][v] - tops[1][0]))
    for u in range(n_slab // 2):
        rows = slice(2 * u * SUBLANES, (2 * u + 2) * SUBLANES)
        r2_ref[h, rows, :] = jnp.concatenate(rank2[2 * u:2 * u + 2], axis=0).astype(BF16)
        e2_ref[h, rows, :] = jnp.concatenate(e2[2 * u:2 * u + 2], axis=0).astype(BF16)
    return jnp.max(tie)


def _route_kernel(h2_ref, wpq_ref, keys_ref, lcol_ref, e1_ref, r2_ref, e2_ref,
                  q_s, score_s, rank_s, vals_s, cand_s, sel_s):
    q = jnp.dot(h2_ref[...], wpq_ref[...], preferred_element_type=F32)
    for hp in range(2 * PEER_HEADS):
        q_s[hp] = q[:, hp * N_KEYS:(hp + 1) * N_KEYS].astype(BF16)
    nt = (((1,), (1,)), ((), ()))

    def per_head(h, carry):
        for p in range(2):
            score_s[p] = lax.dot_general(keys_ref[h * 2 + p], q_s[h * 2 + p], nt,
                                         preferred_element_type=F32)
        tie = _route_fast(h, score_s, vals_s, cand_s, lcol_ref, e1_ref, r2_ref, e2_ref)

        @pl.when(tie > 0.0)
        def _():
            _route_exact(h, score_s, rank_s, vals_s, cand_s, sel_s, lcol_ref, e1_ref, r2_ref, e2_ref)

        return carry

    lax.fori_loop(0, PEER_HEADS, per_head, 0)


def _route(h2, w_pq16, keys16, tm):
    bsz, t, d = h2.shape
    nq = w_pq16.shape[1]
    out = pl.BlockSpec((None, PEER_HEADS, N_KEYS, tm), lambda b, i: (b, 0, 0, i))
    shape = jax.ShapeDtypeStruct((bsz, PEER_HEADS, N_KEYS, t), F32)
    shape16 = jax.ShapeDtypeStruct((bsz, PEER_HEADS, N_KEYS, t), BF16)
    return pl.pallas_call(
        _route_kernel,
        grid=(bsz, t // tm),
        in_specs=[pl.BlockSpec((None, tm, d), lambda b, i: (b, i, 0)),
                  pl.BlockSpec((d, nq), lambda b, i: (0, 0)),
                  pl.BlockSpec((2 * PEER_HEADS, N_KEYS, nq // (2 * PEER_HEADS)), lambda b, i: (0, 0, 0))],
        out_specs=[out, out, out, out],
        out_shape=[shape, shape, shape16, shape16],
        scratch_shapes=[pltpu.VMEM((2 * PEER_HEADS, tm, nq // (2 * PEER_HEADS)), BF16),
                        pltpu.VMEM((2, N_KEYS, tm), F32),
                        pltpu.VMEM((2, N_KEYS, tm), F32),
                        pltpu.VMEM((2, PEER_TOPK, tm), F32),
                        pltpu.VMEM((_N_CAND_PAD, tm), F32),
                        pltpu.VMEM((_N_CAND_PAD, tm), F32)],
        compiler_params=_params(2),
        name="route",
    )(h2, w_pq16, keys16)


_ACT_ROWS = 512
_PACK = 16


def _peer_kernel(h2_ref, u_ref, vt_ref, lcol_ref, e1_ref, r2_ref, e2_ref, x1_ref, g2_ref,
                 lg_ref, lb_ref, y_ref, acc, p_s, *, te):
    j = pl.program_id(2)

    @pl.when(j == 0)
    def _():
        acc[...] = jnp.zeros(acc.shape, F32)

    zero = jnp.zeros((), BF16)
    n_sub = _ACT_ROWS // N_KEYS
    tm = h2_ref.shape[0]
    for c in range(te // _ACT_ROWS):
        chunk = slice(c * _ACT_ROWS, (c + 1) * _ACT_ROWS)
        act = lax.dot_general(u_ref[chunk, :], h2_ref[...], (((1,), (1,)), ((), ())),
                              preferred_element_type=F32)
        for sub in range(n_sub):
            al = c * n_sub + sub
            w = None
            for h in range(PEER_HEADS):
                lc = jnp.broadcast_to(lcol_ref[h, al:al + 1, :], (_PACK, tm)).astype(BF16)
                e1 = jnp.broadcast_to(e1_ref[h, al:al + 1, :], (_PACK, tm)).astype(BF16)
                r2 = r2_ref[h].reshape(N_KEYS // _PACK, _PACK, tm)
                e2 = e2_ref[h].reshape(N_KEYS // _PACK, _PACK, tm)
                term = jnp.where(r2 < lc[None], e2 * e1[None], zero)
                w = term if w is None else w + term
            g = _gelu(act[sub * N_KEYS:(sub + 1) * N_KEYS, :])
            p_s[al * N_KEYS:(al + 1) * N_KEYS, :] = (w.reshape(N_KEYS, tm).astype(F32) * g).astype(BF16)
    acc[...] += jnp.dot(vt_ref[...], p_s[...], preferred_element_type=F32)

    @pl.when(j == pl.num_programs(2) - 1)
    def _():
        ffn = acc[...].T
        y = _normalize(ALPHA * x1_ref[...] + g2_ref[...] * ffn)
        y_ref[...] = y * lg_ref[...] + lb_ref[...]


def _peer(h2, u16, vt16, lcol, e1, r2, e2, x1, gate2, ln_g, ln_b, tm, te):
    bsz, t, d = h2.shape
    n_exp = u16.shape[0]
    na = te // N_KEYS
    tok = pl.BlockSpec((None, tm, d), lambda b, i, j: (b, i, 0))
    vec = pl.BlockSpec((1, d), lambda b, i, j: (0, 0))
    by_a = pl.BlockSpec((None, PEER_HEADS, na, tm), lambda b, i, j: (b, 0, j, i))
    by_b = pl.BlockSpec((None, PEER_HEADS, N_KEYS, tm), lambda b, i, j: (b, 0, 0, i))
    return pl.pallas_call(
        functools.partial(_peer_kernel, te=te),
        grid=(bsz, t // tm, n_exp // te),
        in_specs=[tok,
                  pl.BlockSpec((te, d), lambda b, i, j: (j, 0)),
                  pl.BlockSpec((d, te), lambda b, i, j: (0, j)),
                  by_a, by_a, by_b, by_b, tok, _mod_spec(gate2, tm, d), vec, vec],
        out_specs=tok,
        out_shape=jax.ShapeDtypeStruct((bsz, t, d), F32),
        scratch_shapes=[pltpu.VMEM((d, tm), F32), pltpu.VMEM((te, tm), BF16)],
        compiler_params=_params(3),
        name="peer",
    )(h2, u16, vt16, lcol, e1, r2, e2, x1, gate2, ln_g.reshape(1, d), ln_b.reshape(1, d))


def _tiles(n_tok, t):
    return dict(tm_proj=min(1024, n_tok),
                tm_mix=min(512, n_tok),
                tt=min(512, t), tc=512,
                tm_route=min(512, n_tok),
                tm_peer=min(512, n_tok), te=1024)


def _layer(x, mods, conv_buf, h0, attend, w, *, keep_rows, flatten):
    bsz, t, d = x.shape
    tl = _tiles(bsz * t if flatten else t, t)
    if flatten:
        mods = [jnp.broadcast_to(m, (bsz, t, d)).reshape(1, bsz * t, d) for m in mods]
        flat = lambda a: a.reshape(1, bsz * t, a.shape[-1])
    else:
        flat = lambda a: a
    shift1, scale1, gate1, shift2, scale2, gate2 = mods
    proj = _proj(flat(x), shift1, scale1, w["w_in"], tl["tm_proj"]).reshape(bsz, t, -1)
    pa, conv_state, rnn_state = _rnn(proj, conv_buf, h0, w["conv_w"], w["conv_b"], w["w_ra"], w["b_ra"],
                                     w["w_ri"], w["b_ri"], w["lam"], d, tl["tt"], tl["tc"])
    pb = attend(proj)
    x1, h2 = _mix(flat(pa), flat(pb), flat(x), w["w_out"], gate1, shift2, scale2, w["ln1_g"], w["ln1_b"],
                  tl["tm_mix"])
    lcol, e1, r2, e2 = _route(h2, w["w_pq"], w["keys"], tl["tm_route"])
    y = _peer(h2, w["peer_u"], w["peer_vt"], lcol, e1, r2, e2, x1, gate2, w["ln2_g"], w["ln2_b"],
              tl["tm_peer"], tl["te"])
    n_heads = d // HEAD_DIM
    k = proj[:, t - keep_rows:, 3 * d:4 * d].reshape(bsz, keep_rows, n_heads, HEAD_DIM)
    v = proj[:, t - keep_rows:, 4 * d:5 * d].reshape(bsz, keep_rows, n_heads, HEAD_DIM)
    return y.reshape(bsz, t, d), conv_state, rnn_state, k, v


def kernel(x_prompt, x_sample, c_prompt, c_sample, state_conv, state_rnn, cache_k, cache_v, w_ada, b_ada, w_in, conv_w, conv_b, w_ra, b_ra, w_ri, b_ri, rg_lambda, rel_bias, w_out, ln1_g, ln1_b, w_pq, peer_keys, peer_u, peer_v, ln2_g, ln2_b):
    bsz_p, t_p, d = x_prompt.shape
    bsz_s, t_s, _ = x_sample.shape
    depth = w_ada.shape[0]
    keep = min(ATT_REACH, t_p)
    y_p, y_s = x_prompt, x_sample
    outs = [[] for _ in range(8)]
    rows = bsz_p + bsz_s
    rows_pad = -(-rows // SUBLANES) * SUBLANES
    c_all = jnp.concatenate([c_prompt, c_sample, jnp.zeros((rows_pad - rows, d), F32)], axis=0)
    for l in range(depth):
        w = dict(w_in=w_in[l].astype(BF16), conv_w=conv_w[l], conv_b=conv_b[l],
                 w_ra=w_ra[l].astype(BF16), b_ra=b_ra[l], w_ri=w_ri[l].astype(BF16), b_ri=b_ri[l],
                 lam=rg_lambda[l], w_out=w_out[l].astype(BF16), ln1_g=ln1_g[l], ln1_b=ln1_b[l],
                 w_pq=w_pq[l].astype(BF16),
                 keys=peer_keys[l].astype(BF16).reshape(2 * PEER_HEADS, N_KEYS, -1),
                 peer_u=peer_u[l].astype(BF16), peer_vt=peer_v[l].astype(BF16).T,
                 ln2_g=ln2_g[l], ln2_b=ln2_b[l])
        mod = _ada(c_all, w_ada[l], b_ada[l])
        mods_p = [m[:, None, :] for m in jnp.split(mod[:bsz_p], 6, axis=-1)]
        mods_s = [m[:, None, :] for m in jnp.split(mod[bsz_p:rows], 6, axis=-1)]

        y_p, cs, hs, k, v = _layer(
            y_p, mods_p, jnp.zeros((bsz_p, CONV_W - 1, d), F32), jnp.zeros((bsz_p, d), F32),
            lambda proj: _attn_prompt(proj, rel_bias[l], d), w,
            keep_rows=keep, flatten=False)
        for dst, val in zip(outs[:4], (cs, hs, k, v)):
            dst.append(val)

        y_s, cs, hs, k, v = _layer(
            y_s, mods_s, state_conv[l], state_rnn[l],
            lambda proj: _attn_sample(proj, cache_k[l], cache_v[l], rel_bias[l], d), w,
            keep_rows=t_s, flatten=True)
        for dst, val in zip(outs[4:], (cs, hs, k, v)):
            dst.append(val)
    return (y_p, y_s) + tuple(jnp.stack(o) for o in outs)
```
